```python
import math
import jax, jax.numpy as jnp
from jax import lax
import numpy as np

D_MODEL = 1024
BATCH = 4
SEQ = 8192
DEPTH = 4

HEAD_DIM = 64
BLK = 128
A_HEADS = 8
A_PATTERNS = ((128, 1), (512, 4), (2048, 16))
A_PAD = BLK * 16
B_HEADS = 8
B_KV_HEADS = 2
B_WINDOW = 128
C_HEADS = 16
IDX_HEADS = 8
IDX_DIM = 64
TOPK_MAX = 256
REL_BUCKETS = 32
REL_MAX_DIST = 2048
REL_HEADS = 16
D_FF = 3584
N_EXPERTS = 8
TOP_K = 2
MOE_BLOCK = 512
ALPHA = (2 * DEPTH) ** 0.25
BETA = (8 * DEPTH) ** -0.25
LN_EPS = 1e-5

N_EVEN = (DEPTH + 1) // 2
N_ODD = DEPTH // 2
EVEN_SIZES = (A_HEADS * HEAD_DIM, A_HEADS * HEAD_DIM, A_HEADS * HEAD_DIM,
              B_HEADS * HEAD_DIM, B_KV_HEADS * HEAD_DIM, B_KV_HEADS * HEAD_DIM)
EVEN_IN = sum(EVEN_SIZES)
EVEN_MIX = (A_HEADS + B_HEADS) * HEAD_DIM
ODD_SIZES = (C_HEADS * HEAD_DIM, HEAD_DIM, HEAD_DIM, IDX_HEADS * IDX_DIM, IDX_DIM, IDX_HEADS)
ODD_IN = sum(ODD_SIZES)
ODD_MIX = C_HEADS * HEAD_DIM

kernel_name = "hybrid_dilated_swa_dsa_moe_trunk"


def _split(t, sizes):
    return jnp.split(t, [int(s) for s in np.cumsum(sizes)[:-1]], axis=-1)


def t5_bucket(dist):
    n = jnp.maximum(dist, 0)
    max_exact = REL_BUCKETS // 2
    nf = jnp.maximum(n, 1).astype(jnp.float32)
    large = max_exact + (jnp.log(nf / max_exact) / math.log(REL_MAX_DIST / max_exact)
                         * (REL_BUCKETS - max_exact)).astype(jnp.int32)
    large = jnp.minimum(large, REL_BUCKETS - 1)
    return jnp.where(n < max_exact, n, large)


def layer_norm(x, g, b):
    xf = x.astype(jnp.float32)
    mu = jnp.mean(xf, axis=-1, keepdims=True)
    var = jnp.mean(jnp.square(xf - mu), axis=-1, keepdims=True)
    return ((xf - mu) * lax.rsqrt(var + LN_EPS) * g + b).astype(x.dtype)


def ada_modulation(c, w, b):
    mod = jax.nn.silu(c) @ w + b
    shift, scale, gate = jnp.split(mod, 3, axis=-1)
    return shift[:, None], 1.0 + scale[:, None], 1.0 + gate[:, None]


def banded_attention(q, k, v, rel_table, dilation, max_dist, sinks=None):
    N, L, Hq, hd = q.shape
    Hk = k.shape[2]
    G = Hq // Hk
    nb = L // BLK
    qb = q.reshape(N, nb, BLK, Hk, G, hd)

    def band(t):
        t = t.reshape(N, nb, BLK, Hk, hd)
        prev = jnp.concatenate([jnp.zeros_like(t[:, :1]), t[:, :-1]], axis=1)
        return jnp.concatenate([prev, t], axis=2)

    kband, vband = band(k), band(v)
    logits = jnp.einsum('nbqhgd,nbkhd->nbhgqk', qb, kband,
                        preferred_element_type=jnp.float32) * (hd ** -0.5)
    qi = jnp.arange(BLK)[:, None]
    kj = jnp.arange(2 * BLK)[None, :]
    dist = qi + BLK - kj
    bias = rel_table[t5_bucket(dist * dilation)]
    bias = bias.reshape(BLK, 2 * BLK, Hk, G).transpose(2, 3, 0, 1).astype(jnp.float32)
    blk_id = jnp.arange(nb)[:, None, None]
    mask = (dist >= 0) & (dist <= max_dist) & ((blk_id > 0) | (kj >= BLK))
    logits = jnp.where(mask[None, :, None, None], logits + bias, -jnp.inf)
    if sinks is not None:
        s = jnp.broadcast_to(sinks.astype(jnp.float32).reshape(Hk, G, 1, 1), logits.shape[:-1] + (1,))
        lse = jax.nn.logsumexp(jnp.concatenate([logits, s], axis=-1), axis=-1)
    else:
        lse = jax.nn.logsumexp(logits, axis=-1)
    p = jnp.exp(logits - lse[..., None])
    o = jnp.einsum('nbhgqk,nbkhd->nbqhgd', p.astype(v.dtype), vband).reshape(N, L, Hq, hd)
    lse = lse.transpose(0, 1, 4, 2, 3).reshape(N, L, Hq)
    return o, lse


def dilated_attention(q, k, v, rel_table):
    B, S, H, hd = q.shape
    Sp = -(-S // A_PAD) * A_PAD
    padw = ((0, 0), (0, Sp - S), (0, 0), (0, 0))
    qp, kp, vp = jnp.pad(q, padw), jnp.pad(k, padw), jnp.pad(v, padw)
    outs, lses = [], []
    for window, d in A_PATTERNS:
        Ld = Sp // d

        def regroup(t):
            return t.reshape(B, Ld, d, H, hd).transpose(0, 2, 1, 3, 4).reshape(B * d, Ld, H, hd)

        o, lse = banded_attention(regroup(qp), regroup(kp), regroup(vp), rel_table, d, window // d)
        outs.append(o.reshape(B, d, Ld, H, hd).transpose(0, 2, 1, 3, 4).reshape(B, Sp, H, hd))
        lses.append(lse.reshape(B, d, Ld, H).transpose(0, 2, 1, 3).reshape(B, Sp, H))
    wts = jax.nn.softmax(jnp.stack(lses), axis=0)
    o = jnp.einsum('pbsh,pbshd->bshd', wts, jnp.stack(outs).astype(jnp.float32))
    return o[:, :S].astype(q.dtype)


def even_mixer(h, w_in, w_out, rel_table, sinks):
    B, S, _ = h.shape
    qa, ka, va, qb, kb, vb = _split(h @ w_in, EVEN_SIZES)
    heads = lambda t, n: t.reshape(B, S, n, HEAD_DIM)
    oa = dilated_attention(heads(qa, A_HEADS), heads(ka, A_HEADS), heads(va, A_HEADS),
                           rel_table[:, :A_HEADS])
    ob, _ = banded_attention(heads(qb, B_HEADS), heads(kb, B_KV_HEADS), heads(vb, B_KV_HEADS),
                             rel_table[:, A_HEADS:A_HEADS + B_HEADS], 1, B_WINDOW - 1, sinks)
    o = jnp.concatenate([oa.reshape(B, S, -1), ob.reshape(B, S, -1)], axis=-1)
    return o @ w_out


def dsa_mixer(h, w_in, w_out, rel_table):
    B, S, _ = h.shape
    nb = S // BLK
    k_sel = min(TOPK_MAX, S // 4)
    q, kc, vc, qi, ki, wi = _split(h @ w_in, ODD_SIZES)
    q = q.reshape(B, S, C_HEADS, HEAD_DIM)
    qi = qi.reshape(B, S, IDX_HEADS, IDX_DIM)
    wi = wi * (IDX_HEADS ** -0.5 * IDX_DIM ** -0.5)
    table = rel_table[:, :C_HEADS]
    gather = jax.vmap(lambda t, i: t[i])

    def to_blocks(t):
        return t.reshape((B, nb, BLK) + t.shape[2:]).swapaxes(0, 1)

    def block(args):
        b, qb, qib, wib = args
        tpos = b * BLK + jnp.arange(BLK)
        dots = jnp.einsum('bqhd,bsd->bqhs', qib, ki, preferred_element_type=jnp.float32)
        scores = jnp.einsum('bqhs,bqh->bqs', jax.nn.relu(dots), wib.astype(jnp.float32))
        admissible = jnp.arange(S)[None, :] <= tpos[:, None]
        scores = jnp.where(admissible[None], scores, -jnp.inf)
        _, idx = lax.top_k(scores, k_sel)
        valid = idx <= tpos[None, :, None]
        ks, vs = gather(kc, idx), gather(vc, idx)
        logits = jnp.einsum('bqhd,bqkd->bhqk', qb, ks,
                            preferred_element_type=jnp.float32) * (HEAD_DIM ** -0.5)
        bias = table[t5_bucket(tpos[None, :, None] - idx)]
        logits = logits + bias.transpose(0, 3, 1, 2).astype(jnp.float32)
        logits = jnp.where(valid[:, None], logits, -jnp.inf)
        p = jax.nn.softmax(logits, axis=-1)
        return jnp.einsum('bhqk,bqkd->bqhd', p.astype(vs.dtype), vs)

    o = lax.map(block, (jnp.arange(nb), to_blocks(q), to_blocks(qi), to_blocks(wi)))
    o = o.swapaxes(0, 1).reshape(B, S, ODD_MIX)
    return o @ w_out


def swiglu(h, w1, w3, w2):
    return (jax.nn.silu(h @ w1) * (h @ w3)) @ w2


def moe_swiglu(h, w_router, w1, w3, w2):
    B, S, D = h.shape
    T = B * S
    A = T * TOP_K
    xt = h.reshape(T, D)
    logits = jnp.dot(xt, w_router, preferred_element_type=jnp.float32)
    top_vals, top_idx = lax.top_k(logits, TOP_K)
    gates = jax.nn.softmax(top_vals, axis=-1)
    e_flat = top_idx.reshape(-1)
    tok_flat = jnp.arange(A) // TOP_K
    order = jnp.argsort(e_flat)
    e_sorted = e_flat[order]
    counts = jnp.bincount(e_flat, length=N_EXPERTS)
    starts = jnp.cumsum(counts) - counts
    padded = (counts + MOE_BLOCK - 1) // MOE_BLOCK * MOE_BLOCK
    pends = jnp.cumsum(padded)
    dest = (pends - padded)[e_sorted] + jnp.arange(A) - starts[e_sorted]
    n_blocks = -(-A // MOE_BLOCK) + N_EXPERTS
    P = n_blocks * MOE_BLOCK
    slot_tok = jnp.full((P,), T, jnp.int32).at[dest].set(tok_flat[order])
    slot_gate = jnp.zeros((P,), jnp.float32).at[dest].set(gates.reshape(-1)[order])
    block_expert = jnp.minimum(jnp.searchsorted(pends, jnp.arange(n_blocks) * MOE_BLOCK, side='right'),
                               N_EXPERTS - 1)
    x_pad = jnp.concatenate([xt, jnp.zeros((1, D), xt.dtype)], axis=0)
    xb = x_pad[slot_tok].reshape(n_blocks, MOE_BLOCK, D)

    def expert_block(args):
        xblk, e = args
        return swiglu(xblk, w1[e], w3[e], w2[e])

    yb = lax.map(expert_block, (xb, block_expert))
    y = yb.reshape(P, D) * slot_gate[:, None].astype(yb.dtype)
    out = jnp.zeros((T + 1, D), yb.dtype).at[slot_tok].add(y)[:T]
    return out.reshape(B, S, D)


def setup_inputs(seed: int = 0) -> dict:
    key = jax.random.key(seed)
    ks = jax.random.split(key, 20)
    nrm = lambda k, shape, s: jax.random.normal(k, shape, jnp.float32) * s
    D = D_MODEL
    return {
        "x": nrm(ks[0], (BATCH, SEQ, D), 1.0),
        "c": nrm(ks[1], (BATCH, D), 1.0),
        "rel_table": nrm(ks[2], (REL_BUCKETS, REL_HEADS), 0.3),
        "w_in_even": nrm(ks[3], (N_EVEN, D, EVEN_IN), D ** -0.5),
        "w_out_even": nrm(ks[4], (N_EVEN, EVEN_MIX, D), EVEN_MIX ** -0.5 * BETA),
        "sinks": nrm(ks[5], (N_EVEN, B_HEADS), 0.5),
        "w_in_odd": nrm(ks[6], (N_ODD, D, ODD_IN), D ** -0.5),
        "w_out_odd": nrm(ks[7], (N_ODD, ODD_MIX, D), ODD_MIX ** -0.5 * BETA),
        "ffn_w1": nrm(ks[8], (N_EVEN, D, D_FF), D ** -0.5),
        "ffn_w3": nrm(ks[9], (N_EVEN, D, D_FF), D ** -0.5),
        "ffn_w2": nrm(ks[10], (N_EVEN, D_FF, D), D_FF ** -0.5 * BETA),
        "router": nrm(ks[11], (N_ODD, D, N_EXPERTS), D ** -0.5),
        "exp_w1": nrm(ks[12], (N_ODD, N_EXPERTS, D, D_FF), D ** -0.5),
        "exp_w3": nrm(ks[13], (N_ODD, N_EXPERTS, D, D_FF), D ** -0.5),
        "exp_w2": nrm(ks[14], (N_ODD, N_EXPERTS, D_FF, D), D_FF ** -0.5 * BETA),
        "ada_w": nrm(ks[15], (DEPTH, 2, D, 3 * D), 0.1 * D ** -0.5),
        "ada_b": nrm(ks[16], (DEPTH, 2, 3 * D), 0.02),
        "ln_g": 1.0 + nrm(ks[17], (DEPTH, 2, D), 0.05),
        "ln_b": nrm(ks[18], (DEPTH, 2, D), 0.02),
    }


def reference(x, c, rel_table, w_in_even, w_out_even, sinks, w_in_odd, w_out_odd,
              ffn_w1, ffn_w3, ffn_w2, router, exp_w1, exp_w3, exp_w2,
              ada_w, ada_b, ln_g, ln_b):
    for layer in range(DEPTH):
        i = layer // 2
        shift, scale, gate = ada_modulation(c, ada_w[layer, 0], ada_b[layer, 0])
        h = x * scale + shift
        if layer % 2 == 0:
            mix = even_mixer(h, w_in_even[i], w_out_even[i], rel_table, sinks[i])
        else:
            mix = dsa_mixer(h, w_in_odd[i], w_out_odd[i], rel_table)
        x = layer_norm(ALPHA * x + gate * mix, ln_g[layer, 0], ln_b[layer, 0])
        shift, scale, gate = ada_modulation(c, ada_w[layer, 1], ada_b[layer, 1])
        h = x * scale + shift
        if layer % 2 == 0:
            ff = swiglu(h, ffn_w1[i], ffn_w3[i], ffn_w2[i])
        else:
            ff = moe_swiglu(h, router[i], exp_w1[i], exp_w3[i], exp_w2[i])
        x = layer_norm(ALPHA * x + gate * ff, ln_g[layer, 1], ln_b[layer, 1])
    return x
```

```python
import functools
import math

import numpy as np
import jax
import jax.numpy as jnp
from jax import lax
from jax.experimental import pallas as pl
from jax.experimental.pallas import tpu as pltpu

F32 = jnp.float32
BF16 = jnp.bfloat16
I32 = jnp.int32

D_MODEL = 1024
HEAD_DIM = 64
BLK = 128
A_HEADS = 8
A_PATTERNS = ((128, 1), (512, 4), (2048, 16))
A_PAD = BLK * 16
B_HEADS = 8
B_KV_HEADS = 2
B_WINDOW = 128
C_HEADS = 16
IDX_HEADS = 8
IDX_DIM = 64
TOPK_MAX = 256
REL_BUCKETS = 32
REL_MAX_DIST = 2048
D_FF = 3584
N_EXPERTS = 8
TOP_K = 2
MOE_BLOCK = 512
DEPTH = 4
ALPHA = (2 * DEPTH) ** 0.25
LN_EPS = 1e-5

LANES = 128
SUBLANES = 8
D_TILES = D_MODEL // LANES
NEG = -1e30
INT_MIN = -(2 ** 31)
VMEM_LIMIT = 56 * 1024 * 1024

DSA_CK = 256


def _bucket_np(dist):
    n = np.maximum(dist, 0)
    max_exact = REL_BUCKETS // 2
    nf = np.maximum(n, 1).astype(np.float32)
    large = max_exact + (np.log(nf / np.float32(max_exact)) / np.float32(math.log(REL_MAX_DIST / max_exact))
                         * np.float32(REL_BUCKETS - max_exact)).astype(np.int32)
    large = np.minimum(large, REL_BUCKETS - 1)
    return np.where(n < max_exact, n, large).astype(np.int32)


def _far_distance():
    b = _bucket_np(np.arange(0, 2 * REL_MAX_DIST))
    return int(np.max(np.nonzero(b != REL_BUCKETS - 1)[0])) + 1


FAR_DIST = _far_distance()
NEAR_W = -(-(FAR_DIST + DSA_CK - 1 + BLK) // LANES) * LANES
STRIP_W = NEAR_W + BLK


def _cparams(sem):
    return pltpu.CompilerParams(dimension_semantics=sem, vmem_limit_bytes=VMEM_LIMIT)


def _layer_norm(z, g, b):
    mu = jnp.mean(z, axis=-1, keepdims=True)
    zc = z - mu
    var = jnp.mean(zc * zc, axis=-1, keepdims=True)
    return zc * lax.rsqrt(var + LN_EPS) * g + b


def _ada_kernel(c_ref, w_ref, b_ref, o_ref):
    j = pl.program_id(1)
    c = c_ref[...]
    sc = c * jax.nn.sigmoid(c)
    mod = jnp.dot(sc, w_ref[0], preferred_element_type=F32, precision=lax.Precision.HIGHEST)
    o_ref[0] = mod + b_ref[0] + jnp.where(j >= 1, 1.0, 0.0)


def ada_modulation_all(c, ada_w, ada_b):
    B, D = c.shape
    n = ada_w.shape[0] * ada_w.shape[1]
    rows = -(-B // SUBLANES) * SUBLANES
    cp = jnp.pad(c, ((0, rows - B), (0, 0)))
    w = ada_w.reshape(n, D, 3 * D)
    b = ada_b.reshape(n, 1, 3 * D)
    out = pl.pallas_call(
        _ada_kernel,
        grid=(n, 3),
        in_specs=[pl.BlockSpec((rows, D), lambda l, j: (0, 0)),
                  pl.BlockSpec((1, D, D), lambda l, j: (l, 0, j)),
                  pl.BlockSpec((1, 1, D), lambda l, j: (l, 0, j))],
        out_specs=pl.BlockSpec((1, rows, D), lambda l, j: (l, 0, j)),
        out_shape=jax.ShapeDtypeStruct((n, rows, 3 * D), F32),
        compiler_params=_cparams(("arbitrary", "arbitrary")),
        name="ada_modulation",
    )(cp, w, b)
    return out[:, :B]


def _inproj_kernel(x_ref, sc_ref, sh_ref, w_ref, *o_refs, splits):
    h = (x_ref[0] * sc_ref[0] + sh_ref[0]).astype(BF16)
    for o_ref, (start, width) in zip(o_refs, splits):
        o_ref[0] = jnp.dot(h, w_ref[:, start:start + width],
                           preferred_element_type=F32).astype(o_ref.dtype)


def in_projection(x, scale, shift, w, groups, tm=512):
    B, S, D = x.shape
    splits, start = [], 0
    for width, _ in groups:
        splits.append((start, width))
        start += width
    assert start == w.shape[1] and S % tm == 0
    return pl.pallas_call(
        functools.partial(_inproj_kernel, splits=tuple(splits)),
        grid=(B, S // tm),
        in_specs=[pl.BlockSpec((1, tm, D), lambda b, i: (b, i, 0)),
                  pl.BlockSpec((1, 1, D), lambda b, i: (b, 0, 0)),
                  pl.BlockSpec((1, 1, D), lambda b, i: (b, 0, 0)),
                  pl.BlockSpec(w.shape, lambda b, i: (0, 0))],
        out_specs=[pl.BlockSpec((1, tm, width), lambda b, i: (b, i, 0)) for width, _ in groups],
        out_shape=[jax.ShapeDtypeStruct((B, S, width), dt) for width, dt in groups],
        compiler_params=_cparams(("parallel", "parallel")),
        name="in_projection",
    )(x, scale, shift, w)


def _band_kernel(sink_ref, q_ref, kp_ref, ko_ref, vp_ref, vo_ref, bias_ref, o_ref, lse_ref, *,
                 hq, hk, use_sinks):
    b = pl.program_id(1)
    q = q_ref[0]
    kk = jnp.concatenate([kp_ref[0], ko_ref[0]], axis=0)
    vv = jnp.concatenate([vp_ref[0], vo_ref[0]], axis=0)
    col = lax.broadcasted_iota(I32, (BLK, 2 * BLK), 1)
    first_mask = jnp.where(jnp.logical_and(b == 0, col < BLK), NEG, 0.0)
    group = hq // hk
    for h in range(hq):
        g = h // group
        qh = q[:, h * HEAD_DIM:(h + 1) * HEAD_DIM]
        kh = kk[:, g * HEAD_DIM:(g + 1) * HEAD_DIM]
        vh = vv[:, g * HEAD_DIM:(g + 1) * HEAD_DIM]
        s = lax.dot_general(qh, kh, (((1,), (1,)), ((), ())), preferred_element_type=F32)
        s = s + bias_ref[h] + first_mask
        m = jnp.max(s, axis=-1, keepdims=True)
        if use_sinks:
            m = jnp.maximum(m, sink_ref[h])
        p = jnp.exp(s - m)
        l = jnp.sum(p, axis=-1, keepdims=True)
        if use_sinks:
            l = l + jnp.exp(sink_ref[h] - m)
        o = jnp.dot(p.astype(BF16), vh, preferred_element_type=F32)
        o_ref[0, :, h * HEAD_DIM:(h + 1) * HEAD_DIM] = (o / l).astype(o_ref.dtype)
        lse_ref[0, :, h:h + 1] = m + jnp.log(l)


def banded_attention(q, k, v, bias, sinks, out_dtype):
    N, L, qc = q.shape
    hq = qc // HEAD_DIM
    hk = k.shape[2] // HEAD_DIM
    use_sinks = sinks is not None
    if sinks is None:
        sinks = jnp.zeros((hq,), F32)
    cur = lambda n, b, s: (n, b, 0)
    prev = lambda n, b, s: (n, jnp.maximum(b - 1, 0), 0)
    return pl.pallas_call(
        functools.partial(_band_kernel, hq=hq, hk=hk, use_sinks=use_sinks),
        grid_spec=pltpu.PrefetchScalarGridSpec(
            num_scalar_prefetch=1,
            grid=(N, L // BLK),
            in_specs=[pl.BlockSpec((1, BLK, qc), cur),
                      pl.BlockSpec((1, BLK, hk * HEAD_DIM), prev),
                      pl.BlockSpec((1, BLK, hk * HEAD_DIM), cur),
                      pl.BlockSpec((1, BLK, hk * HEAD_DIM), prev),
                      pl.BlockSpec((1, BLK, hk * HEAD_DIM), cur),
                      pl.BlockSpec(bias.shape, lambda n, b, s: (0, 0, 0))],
            out_specs=[pl.BlockSpec((1, BLK, qc), cur),
                       pl.BlockSpec((1, BLK, hq), cur)]),
        out_shape=[jax.ShapeDtypeStruct((N, L, qc), out_dtype),
                   jax.ShapeDtypeStruct((N, L, hq), F32)],
        compiler_params=_cparams(("parallel", "arbitrary")),
        name="banded_attention",
    )(sinks, q, k, k, v, v, bias)


def band_bias(table, dilation, max_dist):
    qi = np.arange(BLK)[:, None]
    kj = np.arange(2 * BLK)[None, :]
    dist = qi + BLK - kj
    allowed = (dist >= 0) & (dist <= max_dist)
    bias = table[_bucket_np(dist * dilation)]
    bias = jnp.where(allowed[:, :, None], bias, NEG)
    return jnp.transpose(bias, (2, 0, 1)).astype(F32)


def _combine_kernel(*refs, n_pat, heads):
    o_refs, l_refs, out_ref = refs[:n_pat], refs[n_pat:2 * n_pat], refs[2 * n_pat]
    lses = [r[0] for r in l_refs]
    m = functools.reduce(jnp.maximum, lses)
    es = [jnp.exp(l - m) for l in lses]
    tot = functools.reduce(lambda a, b: a + b, es)
    ws = [e / tot for e in es]
    for h in range(heads):
        sl = slice(h * HEAD_DIM, (h + 1) * HEAD_DIM)
        acc = ws[0][:, h:h + 1] * o_refs[0][0, :, sl]
        for p in range(1, n_pat):
            acc = acc + ws[p][:, h:h + 1] * o_refs[p][0, :, sl]
        out_ref[0, :, sl] = acc.astype(out_ref.dtype)


def combine_patterns(outs, lses, tm=512):
    B, S, C = outs[0].shape
    heads = C // HEAD_DIM
    n_pat = len(outs)
    idx = lambda b, i: (b, i, 0)
    return pl.pallas_call(
        functools.partial(_combine_kernel, n_pat=n_pat, heads=heads),
        grid=(B, S // tm),
        in_specs=[pl.BlockSpec((1, tm, C), idx)] * n_pat + [pl.BlockSpec((1, tm, heads), idx)] * n_pat,
        out_specs=pl.BlockSpec((1, tm, C), idx),
        out_shape=jax.ShapeDtypeStruct((B, S, C), BF16),
        compiler_params=_cparams(("parallel", "parallel")),
        name="combine_patterns",
    )(*outs, *lses)


def _outproj_kernel(*refs, n_parts):
    o_refs, w_refs = refs[:n_parts], refs[n_parts:2 * n_parts]
    x_ref, gate_ref, g_ref, b_ref, out_ref = refs[2 * n_parts:]
    mix = jnp.dot(o_refs[0][0], w_refs[0][...], preferred_element_type=F32)
    for o_ref, w_ref in zip(o_refs[1:], w_refs[1:]):
        mix = mix + jnp.dot(o_ref[0], w_ref[...], preferred_element_type=F32)
    z = ALPHA * x_ref[0] + gate_ref[0] * mix
    out_ref[0] = _layer_norm(z, g_ref[...], b_ref[...])


def out_projection_ln(parts, weights, x, gate, ln_g, ln_b, tm=512):
    B, S, D = x.shape
    idx = lambda b, i: (b, i, 0)
    const2 = lambda b, i: (0, 0)
    return pl.pallas_call(
        functools.partial(_outproj_kernel, n_parts=len(parts)),
        grid=(B, S // tm),
        in_specs=([pl.BlockSpec((1, tm, p.shape[2]), idx) for p in parts]
                  + [pl.BlockSpec(w.shape, const2) for w in weights]
                  + [pl.BlockSpec((1, tm, D), idx),
                     pl.BlockSpec((1, 1, D), lambda b, i: (b, 0, 0)),
                     pl.BlockSpec((1, D), const2),
                     pl.BlockSpec((1, D), const2)]),
        out_specs=pl.BlockSpec((1, tm, D), idx),
        out_shape=jax.ShapeDtypeStruct((B, S, D), F32),
        compiler_params=_cparams(("parallel", "parallel")),
        name="out_projection_ln",
    )(*parts, *weights, x, gate, ln_g.reshape(1, D), ln_b.reshape(1, D))


def _dsa_kernel(cfar_ref, q_ref, qi_ref, wi_ref, kit_ref, kct_ref, va_ref, vb_ref, strip_ref, o_ref,
                sk_scr, qm_scr, qim_scr, wib_scr, m_scr, acc_scr, *, k_sel):
    i = pl.program_id(1)
    ck = DSA_CK
    sub = ck // LANES
    nck = (i * BLK + BLK + ck - 1) // ck
    nfar = jnp.maximum(i * BLK - FAR_DIST + 1, 0) // ck
    lane = lax.broadcasted_iota(I32, (BLK, LANES), 1)
    upper = lane >= HEAD_DIM

    for p in range(C_HEADS // 2):
        qp = q_ref[0, :, p * LANES:(p + 1) * LANES]
        qm_scr[2 * p] = jnp.where(upper, jnp.zeros_like(qp), qp)
        qm_scr[2 * p + 1] = jnp.where(upper, qp, jnp.zeros_like(qp))
    for p in range(IDX_HEADS // 2):
        qp = qi_ref[0, :, p * LANES:(p + 1) * LANES]
        qim_scr[2 * p] = jnp.where(upper, jnp.zeros_like(qp), qp)
        qim_scr[2 * p + 1] = jnp.where(upper, qp, jnp.zeros_like(qp))
    wscale = IDX_HEADS ** -0.5 * IDX_DIM ** -0.5
    for h in range(IDX_HEADS):
        wib_scr[h] = jnp.broadcast_to(wi_ref[0, :, h:h + 1] * wscale, (BLK, LANES))

    row_t = lax.broadcasted_iota(I32, (BLK, ck), 0) + i * BLK
    col_l = lax.broadcasted_iota(I32, (BLK, ck), 1)

    def score_chunk(j, carry):
        kt = kit_ref[0, j]
        sc = jnp.zeros((BLK, ck), F32)
        for h in range(IDX_HEADS):
            d = jnp.dot(qim_scr[h], kt, preferred_element_type=F32)
            w = wib_scr[h]
            sc = sc + jnp.maximum(d, 0.0) * jnp.concatenate([w] * sub, axis=1)
        bits = pltpu.bitcast(sc, I32)
        key = bits ^ ((bits >> 31) & 0x7FFFFFFF)
        sk_scr[j] = jnp.where(col_l + j * ck <= row_t, key, INT_MIN)
        return carry

    lax.fori_loop(0, nck, score_chunk, 0)

    def bit_pass(t, v_u):
        c_u = v_u | lax.shift_left(jnp.int32(1), 31 - t)
        c_s = c_u ^ INT_MIN
        c_wide = jnp.concatenate([c_s] * sub, axis=1)

        def count_chunk(j, acc):
            hit = jnp.where(sk_scr[j] >= c_wide, 1, 0)
            for u in range(sub):
                acc = acc + hit[:, u * LANES:(u + 1) * LANES]
            return acc

        acc = lax.fori_loop(0, nck, count_chunk, jnp.zeros((BLK, LANES), I32))
        cnt = jnp.sum(acc.astype(F32), axis=1, keepdims=True)
        return jnp.where(cnt >= float(k_sel), c_u, v_u)

    v_u = lax.fori_loop(0, 32, bit_pass, jnp.zeros((BLK, LANES), I32))
    thr = jnp.maximum(v_u ^ INT_MIN, INT_MIN + 1)
    thr_wide = jnp.concatenate([thr] * sub, axis=1)

    m_scr[...] = jnp.full(m_scr.shape, NEG, F32)
    acc_scr[...] = jnp.zeros(acc_scr.shape, F32)

    def attend(j, near):
        mb = jnp.where(sk_scr[j] >= thr_wide, 0.0, NEG)
        kt = kct_ref[0, j]
        va = va_ref[0, j]
        vb = vb_ref[0, j]
        if near:
            jb = (j * ck + NEAR_W) // LANES - (i + 1)
        for h in range(C_HEADS):
            s = jnp.dot(qm_scr[h], kt, preferred_element_type=F32) + mb
            if near:
                s = s + jnp.concatenate([strip_ref[h, jb + u] for u in range(sub)], axis=1)
                shift = 0.0
            else:
                shift = cfar_ref[h]
            m_old = m_scr[h]
            m_new = jnp.maximum(m_old, jnp.max(s, axis=1, keepdims=True) + shift)
            p = jnp.exp(s - (m_new - shift))
            pv = jnp.dot(p.astype(BF16), va if h % 2 == 0 else vb, preferred_element_type=F32)
            acc_scr[h] = jnp.exp(m_old - m_new) * acc_scr[h] + pv
            m_scr[h] = m_new

    def attend_far(j, carry):
        attend(j, False)
        return carry

    def attend_near(j, carry):
        attend(j, True)
        return carry

    lax.fori_loop(0, nfar, attend_far, 0)
    lax.fori_loop(nfar, nck, attend_near, 0)

    for p in range(C_HEADS // 2):
        a0 = acc_scr[2 * p]
        a1 = acc_scr[2 * p + 1]
        o = jnp.where(upper, a1 / pltpu.roll(a1, HEAD_DIM, 1), a0 / pltpu.roll(a0, HEAD_DIM, 1))
        o_ref[0, :, p * LANES:(p + 1) * LANES] = o.astype(o_ref.dtype)


def dsa_attention(q, qi, wi, ki, kc, vc, table):
    B, S, _ = q.shape
    ck = DSA_CK
    assert S % ck == 0
    nchunks = S // ck
    k_sel = min(TOPK_MAX, S // 4)

    def chunked_t(t):
        tt = jnp.transpose(t.reshape(B, nchunks, ck, HEAD_DIM), (0, 1, 3, 2))
        return jnp.concatenate([tt, tt], axis=2)

    ones = jnp.ones_like(vc)
    va = jnp.concatenate([vc, ones], axis=-1).reshape(B, nchunks, ck, LANES)
    vb = jnp.concatenate([ones, vc], axis=-1).reshape(B, nchunks, ck, LANES)
    dist = np.arange(BLK)[:, None] - BLK + NEAR_W - np.arange(STRIP_W)[None, :]
    strip = jnp.transpose(table[_bucket_np(dist)], (2, 0, 1)).astype(F32)
    strip = jnp.transpose(strip.reshape(C_HEADS, BLK, STRIP_W // LANES, LANES), (0, 2, 1, 3))
    cfar = table[REL_BUCKETS - 1].astype(F32)

    qblk = lambda b, i, s: (b, i, 0)
    per_b = lambda b, i, s: (b, 0, 0, 0)
    return pl.pallas_call(
        functools.partial(_dsa_kernel, k_sel=k_sel),
        grid_spec=pltpu.PrefetchScalarGridSpec(
            num_scalar_prefetch=1,
            grid=(B, S // BLK),
            in_specs=[pl.BlockSpec((1, BLK, C_HEADS * HEAD_DIM), qblk),
                      pl.BlockSpec((1, BLK, IDX_HEADS * IDX_DIM), qblk),
                      pl.BlockSpec((1, BLK, IDX_HEADS), qblk),
                      pl.BlockSpec((1, nchunks, LANES, ck), per_b),
                      pl.BlockSpec((1, nchunks, LANES, ck), per_b),
                      pl.BlockSpec((1, nchunks, ck, LANES), per_b),
                      pl.BlockSpec((1, nchunks, ck, LANES), per_b),
                      pl.BlockSpec(strip.shape, lambda b, i, s: (0, 0, 0, 0))],
            out_specs=pl.BlockSpec((1, BLK, C_HEADS * HEAD_DIM), qblk),
            scratch_shapes=[pltpu.VMEM((nchunks, BLK, ck), I32),
                            pltpu.VMEM((C_HEADS, BLK, LANES), BF16),
                            pltpu.VMEM((IDX_HEADS, BLK, LANES), BF16),
                            pltpu.VMEM((IDX_HEADS, BLK, LANES), F32),
                            pltpu.VMEM((C_HEADS, BLK, 1), F32),
                            pltpu.VMEM((C_HEADS, BLK, LANES), F32)]),
        out_shape=jax.ShapeDtypeStruct((B, S, C_HEADS * HEAD_DIM), BF16),
        compiler_params=_cparams(("parallel", "arbitrary")),
        name="dsa_attention",
    )(cfar, q, qi, wi, chunked_t(ki), chunked_t(kc), va, vb, strip)


def _ffn_step(x2d, w1_ref, w3_ref, w2_ref, acc):
    xb = x2d[...]
    h1 = jnp.dot(xb, w1_ref[0], preferred_element_type=F32)
    h3 = jnp.dot(xb, w3_ref[0], preferred_element_type=F32)
    a = (h1 * jax.nn.sigmoid(h1) * h3).astype(BF16)
    acc[...] += jnp.dot(a, w2_ref[0], preferred_element_type=F32)


def _ffn_dense_kernel(x_ref, sc_ref, sh_ref, gate_ref, g_ref, b_ref, w1_ref, w3_ref, w2_ref, out_ref,
                      x2d, acc, *, nf):
    f = pl.program_id(2)

    @pl.when(f == 0)
    def _():
        x2d[...] = (x_ref[0] * sc_ref[0] + sh_ref[0]).astype(BF16)
        acc[...] = jnp.zeros(acc.shape, F32)

    _ffn_step(x2d, w1_ref, w3_ref, w2_ref, acc)

    @pl.when(f == nf - 1)
    def _():
        z = ALPHA * x_ref[0] + gate_ref[0] * acc[...]
        out_ref[0] = _layer_norm(z, g_ref[...], b_ref[...])


def ffn_dense_ln(x, scale, shift, gate, ln_g, ln_b, w1, w3, w2, tm=512, tf=512):
    B, S, D = x.shape
    nf = D_FF // tf
    xi = lambda b, i, f: (b, i, 0)
    bi = lambda b, i, f: (b, 0, 0)
    c2 = lambda b, i, f: (0, 0)
    return pl.pallas_call(
        functools.partial(_ffn_dense_kernel, nf=nf),
        grid=(B, S // tm, nf),
        in_specs=[pl.BlockSpec((1, tm, D), xi),
                  pl.BlockSpec((1, 1, D), bi), pl.BlockSpec((1, 1, D), bi), pl.BlockSpec((1, 1, D), bi),
                  pl.BlockSpec((1, D), c2), pl.BlockSpec((1, D), c2),
                  pl.BlockSpec((1, D, tf), lambda b, i, f: (0, 0, f)),
                  pl.BlockSpec((1, D, tf), lambda b, i, f: (0, 0, f)),
                  pl.BlockSpec((1, tf, D), lambda b, i, f: (0, f, 0))],
        out_specs=pl.BlockSpec((1, tm, D), xi),
        out_shape=jax.ShapeDtypeStruct((B, S, D), F32),
        scratch_shapes=[pltpu.VMEM((tm, D), BF16), pltpu.VMEM((tm, D), F32)],
        compiler_params=_cparams(("parallel", "parallel", "arbitrary")),
        name="ffn_dense_ln",
    )(x, scale, shift, gate, ln_g.reshape(1, D), ln_b.reshape(1, D), w1, w3, w2)


def _ffn_expert_kernel(be_ref, x_ref, w1_ref, w3_ref, w2_ref, out_ref, x2d, acc, *, nf, tm):
    f = pl.program_id(1)

    @pl.when(f == 0)
    def _():
        for j in range(D_TILES):
            x2d[:, j * LANES:(j + 1) * LANES] = x_ref[pl.ds(j, tm, stride=D_TILES), :].astype(BF16)
        acc[...] = jnp.zeros(acc.shape, F32)

    _ffn_step(x2d, w1_ref, w3_ref, w2_ref, acc)

    @pl.when(f == nf - 1)
    def _():
        for j in range(D_TILES):
            out_ref[pl.ds(j, tm, stride=D_TILES), :] = acc[:, j * LANES:(j + 1) * LANES]


def ffn_experts(xg, block_expert, w1, w3, w2, tf=512):
    tm = MOE_BLOCK
    n_blocks = xg.shape[0] // (tm * D_TILES)
    nf = D_FF // tf
    return pl.pallas_call(
        functools.partial(_ffn_expert_kernel, nf=nf, tm=tm),
        grid_spec=pltpu.PrefetchScalarGridSpec(
            num_scalar_prefetch=1,
            grid=(n_blocks, nf),
            in_specs=[pl.BlockSpec((tm * D_TILES, LANES), lambda i, f, be: (i, 0)),
                      pl.BlockSpec((1, D_MODEL, tf), lambda i, f, be: (be[i], 0, f)),
                      pl.BlockSpec((1, D_MODEL, tf), lambda i, f, be: (be[i], 0, f)),
                      pl.BlockSpec((1, tf, D_MODEL), lambda i, f, be: (be[i], f, 0))],
            out_specs=pl.BlockSpec((tm * D_TILES, LANES), lambda i, f, be: (i, 0)),
            scratch_shapes=[pltpu.VMEM((tm, D_MODEL), BF16), pltpu.VMEM((tm, D_MODEL), F32)]),
        out_shape=jax.ShapeDtypeStruct(xg.shape, F32),
        compiler_params=_cparams(("parallel", "arbitrary")),
        name="ffn_experts",
    )(block_expert, xg, w1, w3, w2)


def _router_kernel(x_ref, sc_ref, sh_ref, wr_ref, h_ref, idx_ref, gate_ref, *, tm):
    h = x_ref[0] * sc_ref[0] + sh_ref[0]
    for j in range(D_TILES):
        h_ref[pl.ds(j, tm, stride=D_TILES), :] = h[:, j * LANES:(j + 1) * LANES]
    logits = jnp.dot(h, wr_ref[...], preferred_element_type=F32, precision=lax.Precision.HIGHEST)
    lane = lax.broadcasted_iota(I32, logits.shape, 1)
    lg = jnp.where(lane < N_EXPERTS, logits, -jnp.inf)
    m1 = jnp.max(lg, axis=1, keepdims=True)
    i1 = jnp.min(jnp.where(lg == m1, lane, LANES), axis=1, keepdims=True)
    lg2 = jnp.where(lane == i1, -jnp.inf, lg)
    m2 = jnp.max(lg2, axis=1, keepdims=True)
    i2 = jnp.min(jnp.where(lg2 == m2, lane, LANES), axis=1, keepdims=True)
    e = jnp.exp(m2 - m1)
    idx_ref[:, 0:1] = i1
    idx_ref[:, 1:2] = i2
    gate_ref[:, 0:1] = 1.0 / (1.0 + e)
    gate_ref[:, 1:2] = e / (1.0 + e)


def route_tokens(x, scale, shift, w_router, tm=512):
    B, S, D = x.shape
    T = B * S
    nb = S // tm
    wr = jnp.pad(w_router, ((0, 0), (0, LANES - N_EXPERTS)))
    return pl.pallas_call(
        functools.partial(_router_kernel, tm=tm),
        grid=(B, nb),
        in_specs=[pl.BlockSpec((1, tm, D), lambda b, i: (b, i, 0)),
                  pl.BlockSpec((1, 1, D), lambda b, i: (b, 0, 0)),
                  pl.BlockSpec((1, 1, D), lambda b, i: (b, 0, 0)),
                  pl.BlockSpec((D, LANES), lambda b, i: (0, 0))],
        out_specs=[pl.BlockSpec((tm * D_TILES, LANES), lambda b, i: (b * nb + i, 0)),
                   pl.BlockSpec((tm, TOP_K), lambda b, i: (b * nb + i, 0)),
                   pl.BlockSpec((tm, TOP_K), lambda b, i: (b * nb + i, 0))],
        out_shape=[jax.ShapeDtypeStruct((T * D_TILES, LANES), F32),
                   jax.ShapeDtypeStruct((T, TOP_K), I32),
                   jax.ShapeDtypeStruct((T, TOP_K), F32)],
        compiler_params=_cparams(("parallel", "parallel")),
        name="route_tokens",
    )(x, scale, shift, wr)


def _gather_kernel(idx_ref, src_ref, dst_ref, sem, *, rows):
    base = pl.program_id(0) * rows

    def row_copy(r):
        src = pl.multiple_of(idx_ref[0, 0, r] * D_TILES, D_TILES)
        dst = pl.multiple_of((base + r) * D_TILES, D_TILES)
        return pltpu.make_async_copy(src_ref.at[pl.ds(src, D_TILES)], dst_ref.at[pl.ds(dst, D_TILES)], sem)

    def start(r, carry):
        row_copy(r).start()
        return carry

    def wait(r, carry):
        row_copy(r).wait()
        return carry

    lax.fori_loop(0, rows, start, 0)
    lax.fori_loop(0, rows, wait, 0)


def gather_rows(src, idx, rows=512):
    m = idx.shape[0]
    assert m % rows == 0
    idx3 = idx.reshape(m // rows, 1, rows)
    return pl.pallas_call(
        functools.partial(_gather_kernel, rows=rows),
        grid=(m // rows,),
        in_specs=[pl.BlockSpec((1, 1, rows), lambda i: (i, 0, 0), memory_space=pltpu.SMEM),
                  pl.BlockSpec(memory_space=pl.ANY)],
        out_specs=pl.BlockSpec(memory_space=pl.ANY),
        out_shape=jax.ShapeDtypeStruct((m * D_TILES, LANES), src.dtype),
        scratch_shapes=[pltpu.SemaphoreType.DMA(())],
        compiler_params=_cparams(("arbitrary",)),
        name="gather_rows",
    )(idx3, src)


def _moe_combine_kernel(y_ref, gt_ref, x_ref, gate_ref, g_ref, b_ref, out_ref, *, tm):
    g0 = gt_ref[:, 0:1]
    g1 = gt_ref[:, 1:2]
    stride = TOP_K * D_TILES
    pieces = []
    for j in range(D_TILES):
        sl = slice(j * LANES, (j + 1) * LANES)
        ff = g0 * y_ref[pl.ds(j, tm, stride=stride), :] + g1 * y_ref[pl.ds(D_TILES + j, tm, stride=stride), :]
        pieces.append(ALPHA * x_ref[0, :, sl] + gate_ref[0, :, sl] * ff)
    z = jnp.concatenate(pieces, axis=1)
    out_ref[0] = _layer_norm(z, g_ref[...], b_ref[...])


def moe_combine_ln(yg, gates, x, gate, ln_g, ln_b, tm=512):
    B, S, D = x.shape
    nb = S // tm
    return pl.pallas_call(
        functools.partial(_moe_combine_kernel, tm=tm),
        grid=(B, nb),
        in_specs=[pl.BlockSpec((tm * TOP_K * D_TILES, LANES), lambda b, i: (b * nb + i, 0)),
                  pl.BlockSpec((tm, TOP_K), lambda b, i: (b * nb + i, 0)),
                  pl.BlockSpec((1, tm, D), lambda b, i: (b, i, 0)),
                  pl.BlockSpec((1, 1, D), lambda b, i: (b, 0, 0)),
                  pl.BlockSpec((1, D), lambda b, i: (0, 0)),
                  pl.BlockSpec((1, D), lambda b, i: (0, 0))],
        out_specs=pl.BlockSpec((1, tm, D), lambda b, i: (b, i, 0)),
        out_shape=jax.ShapeDtypeStruct((B, S, D), F32),
        compiler_params=_cparams(("parallel", "parallel")),
        name="moe_combine_ln",
    )(yg, gates, x, gate, ln_g.reshape(1, D), ln_b.reshape(1, D))


def _scale_cols(w, start, width, factor):
    return w.at[:, start:start + width].multiply(factor)


def even_mixer_layer(x, mod, w_in, w_out, rel_table, sinks, ln_g, ln_b):
    B, S, D = x.shape
    assert S % A_PAD == 0
    shift, scale, gate = mod
    ah, bh, bk = A_HEADS * HEAD_DIM, B_HEADS * HEAD_DIM, B_KV_HEADS * HEAD_DIM
    w = _scale_cols(w_in, 0, ah, HEAD_DIM ** -0.5)
    w = _scale_cols(w, 3 * ah, bh, HEAD_DIM ** -0.5).astype(BF16)
    groups = [(ah, BF16)] * 3 + [(bh, BF16), (bk, BF16), (bk, BF16)]
    qa, ka, va, qb, kb, vb = in_projection(x, scale, shift, w, groups)

    outs, lses = [], []
    for window, d in A_PATTERNS:
        ld = S // d

        def regroup(t):
            return jnp.transpose(t.reshape(B, ld, d, ah), (0, 2, 1, 3)).reshape(B * d, ld, ah)

        bias = band_bias(rel_table[:, :A_HEADS], d, window // d)
        o, lse = banded_attention(regroup(qa), regroup(ka), regroup(va), bias, None, F32)
        outs.append(jnp.transpose(o.reshape(B, d, ld, ah), (0, 2, 1, 3)).reshape(B, S, ah))
        lses.append(jnp.transpose(lse.reshape(B, d, ld, A_HEADS), (0, 2, 1, 3)).reshape(B, S, A_HEADS))
    oa = combine_patterns(outs, lses)

    bias_b = band_bias(rel_table[:, A_HEADS:A_HEADS + B_HEADS], 1, B_WINDOW - 1)
    ob, _ = banded_attention(qb, kb, vb, bias_b, sinks.astype(F32), BF16)

    wo = w_out.astype(BF16)
    return out_projection_ln([oa, ob], [wo[:ah], wo[ah:]], x, gate, ln_g, ln_b)


def dsa_mixer_layer(x, mod, w_in, w_out, rel_table, ln_g, ln_b):
    shift, scale, gate = mod
    qw = C_HEADS * HEAD_DIM
    iw = IDX_HEADS * IDX_DIM
    q_w, kc_w, vc_w, qi_w, ki_w, wi_w = jnp.split(
        w_in, [qw, qw + HEAD_DIM, qw + 2 * HEAD_DIM, qw + 2 * HEAD_DIM + iw, qw + 2 * HEAD_DIM + iw + IDX_DIM],
        axis=1)
    pad = jnp.zeros((D_MODEL, LANES - IDX_DIM - IDX_HEADS), w_in.dtype)
    w = jnp.concatenate([q_w * HEAD_DIM ** -0.5, kc_w, vc_w, qi_w, ki_w, wi_w, pad], axis=1).astype(BF16)
    groups = [(qw, BF16), (2 * HEAD_DIM, BF16), (iw, BF16), (LANES, F32)]
    q, kv, qi, kw = in_projection(x, scale, shift, w, groups)
    kc, vc = kv[..., :HEAD_DIM], kv[..., HEAD_DIM:]
    ki = kw[..., :IDX_DIM].astype(BF16)
    wi = kw[..., IDX_DIM:IDX_DIM + IDX_HEADS]
    o = dsa_attention(q, qi, wi, ki, kc, vc, rel_table[:, :C_HEADS])
    return out_projection_ln([o], [w_out.astype(BF16)], x, gate, ln_g, ln_b)


def moe_layer(x, mod, w_router, w1, w3, w2, ln_g, ln_b):
    B, S, D = x.shape
    shift, scale, gate = mod
    T = B * S
    A = T * TOP_K
    h_tiles, top_idx, gates = route_tokens(x, scale, shift, w_router)

    e_flat = top_idx.reshape(-1)
    onehot = (e_flat[:, None] == jnp.arange(N_EXPERTS)[None, :]).astype(I32)
    rank = jnp.take_along_axis(jnp.cumsum(onehot, axis=0) - onehot, e_flat[:, None], axis=1)[:, 0]
    counts = jnp.sum(onehot, axis=0)
    padded = (counts + MOE_BLOCK - 1) // MOE_BLOCK * MOE_BLOCK
    pends = jnp.cumsum(padded)
    dest = (pends - padded)[e_flat] + rank
    n_blocks = -(-A // MOE_BLOCK) + N_EXPERTS
    P = n_blocks * MOE_BLOCK
    slot_tok = jnp.zeros((P,), I32).at[dest].set(jnp.arange(A, dtype=I32) // TOP_K)
    block_expert = jnp.minimum(
        jnp.searchsorted(pends, jnp.arange(n_blocks) * MOE_BLOCK, side='right'), N_EXPERTS - 1).astype(I32)

    xg = gather_rows(h_tiles, slot_tok)
    yb = ffn_experts(xg, block_expert, w1.astype(BF16), w3.astype(BF16), w2.astype(BF16))
    yg = gather_rows(yb, dest.astype(I32))
    return moe_combine_ln(yg, gates, x, gate, ln_g, ln_b)


def kernel(x, c, rel_table, w_in_even, w_out_even, sinks, w_in_odd, w_out_odd, ffn_w1, ffn_w3, ffn_w2,
           router, exp_w1, exp_w3, exp_w2, ada_w, ada_b, ln_g, ln_b):
    D = D_MODEL
    mods = ada_modulation_all(c, ada_w, ada_b)

    def mod(layer, sub):
        m = mods[2 * layer + sub]
        return m[:, None, :D], m[:, None, D:2 * D], m[:, None, 2 * D:]

    for layer in range(DEPTH):
        i = layer // 2
        if layer % 2 == 0:
            x = even_mixer_layer(x, mod(layer, 0), w_in_even[i], w_out_even[i], rel_table, sinks[i],
                                 ln_g[layer, 0], ln_b[layer, 0])
            shift, scale, gate = mod(layer, 1)
            x = ffn_dense_ln(x, scale, shift, gate, ln_g[layer, 1], ln_b[layer, 1],
                             ffn_w1[i][None].astype(BF16), ffn_w3[i][None].astype(BF16),
                             ffn_w2[i][None].astype(BF16))
        else:
            x = dsa_mixer_layer(x, mod(layer, 0), w_in_odd[i], w_out_odd[i], rel_table,
                                ln_g[layer, 0], ln_b[layer, 0])
            x = moe_layer(x, mod(layer, 1), router[i], exp_w1[i], exp_w3[i], exp_w2[i],
                          ln_g[layer, 1], ln_b[layer, 1])
    return x
```

```python
import functools
import math

import numpy as np
import jax
import jax.numpy as jnp
from jax import lax
from jax.experimental import pallas as pl
from jax.experimental.pallas import tpu as pltpu

F32 = jnp.float32
BF16 = jnp.bfloat16
I32 = jnp.int32

D_MODEL = 1024
HEAD_DIM = 64
BLK = 128
A_HEADS = 8
A_PATTERNS = ((128, 1), (512, 4), (2048, 16))
A_PAD = BLK * 16
B_HEADS = 8
B_KV_HEADS = 2
B_WINDOW = 128
C_HEADS = 16
IDX_HEADS = 8
IDX_DIM = 64
TOPK_MAX = 256
REL_BUCKETS = 32
REL_MAX_DIST = 2048
D_FF = 3584
N_EXPERTS = 8
TOP_K = 2
MOE_BLOCK = 512
DEPTH = 4
ALPHA = (2 * DEPTH) ** 0.25
LN_EPS = 1e-5

LANES = 128
SUBLANES = 8
D_TILES = D_MODEL // LANES
NEG = -1e30
INT_MIN = -(2 ** 31)
VMEM_LIMIT = 56 * 1024 * 1024

DSA_CK = 256


def _bucket_np(dist):
    n = np.maximum(dist, 0)
    max_exact = REL_BUCKETS // 2
    nf = np.maximum(n, 1).astype(np.float32)
    large = max_exact + (np.log(nf / np.float32(max_exact)) / np.float32(math.log(REL_MAX_DIST / max_exact))
                         * np.float32(REL_BUCKETS - max_exact)).astype(np.int32)
    large = np.minimum(large, REL_BUCKETS - 1)
    return np.where(n < max_exact, n, large).astype(np.int32)


def _far_distance():
    b = _bucket_np(np.arange(0, 2 * REL_MAX_DIST))
    return int(np.max(np.nonzero(b != REL_BUCKETS - 1)[0])) + 1


FAR_DIST = _far_distance()
NEAR_W = -(-(FAR_DIST + DSA_CK - 1 + BLK) // LANES) * LANES
STRIP_W = NEAR_W + BLK


def _cparams(sem):
    return pltpu.CompilerParams(dimension_semantics=sem, vmem_limit_bytes=VMEM_LIMIT)


def _layer_norm(z, g, b):
    mu = jnp.mean(z, axis=-1, keepdims=True)
    zc = z - mu
    var = jnp.mean(zc * zc, axis=-1, keepdims=True)
    return zc * lax.rsqrt(var + LN_EPS) * g + b


def _ada_kernel(c_ref, w_ref, b_ref, o_ref):
    j = pl.program_id(1)
    c = c_ref[...]
    sc = c * jax.nn.sigmoid(c)
    mod = jnp.dot(sc, w_ref[0], preferred_element_type=F32, precision=lax.Precision.HIGHEST)
    o_ref[0] = mod + b_ref[0] + jnp.where(j >= 1, 1.0, 0.0)


def ada_modulation_all(c, ada_w, ada_b):
    B, D = c.shape
    n = ada_w.shape[0] * ada_w.shape[1]
    rows = -(-B // SUBLANES) * SUBLANES
    cp = jnp.pad(c, ((0, rows - B), (0, 0)))
    w = ada_w.reshape(n, D, 3 * D)
    b = ada_b.reshape(n, 1, 3 * D)
    out = pl.pallas_call(
        _ada_kernel,
        grid=(n, 3),
        in_specs=[pl.BlockSpec((rows, D), lambda l, j: (0, 0)),
                  pl.BlockSpec((1, D, D), lambda l, j: (l, 0, j)),
                  pl.BlockSpec((1, 1, D), lambda l, j: (l, 0, j))],
        out_specs=pl.BlockSpec((1, rows, D), lambda l, j: (l, 0, j)),
        out_shape=jax.ShapeDtypeStruct((n, rows, 3 * D), F32),
        compiler_params=_cparams(("arbitrary", "arbitrary")),
        name="ada_modulation",
    )(cp, w, b)
    return out[:, :B]


def _inproj_kernel(x_ref, sc_ref, sh_ref, w_ref, *o_refs, splits):
    h = (x_ref[0] * sc_ref[0] + sh_ref[0]).astype(BF16)
    for o_ref, (start, width) in zip(o_refs, splits):
        o_ref[0] = jnp.dot(h, w_ref[:, start:start + width],
                           preferred_element_type=F32).astype(o_ref.dtype)


def in_projection(x, scale, shift, w, groups, tm=512):
    B, S, D = x.shape
    splits, start = [], 0
    for width, _ in groups:
        splits.append((start, width))
        start += width
    assert start == w.shape[1] and S % tm == 0
    return pl.pallas_call(
        functools.partial(_inproj_kernel, splits=tuple(splits)),
        grid=(B, S // tm),
        in_specs=[pl.BlockSpec((1, tm, D), lambda b, i: (b, i, 0)),
                  pl.BlockSpec((1, 1, D), lambda b, i: (b, 0, 0)),
                  pl.BlockSpec((1, 1, D), lambda b, i: (b, 0, 0)),
                  pl.BlockSpec(w.shape, lambda b, i: (0, 0))],
        out_specs=[pl.BlockSpec((1, tm, width), lambda b, i: (b, i, 0)) for width, _ in groups],
        out_shape=[jax.ShapeDtypeStruct((B, S, width), dt) for width, dt in groups],
        compiler_params=_cparams(("parallel", "parallel")),
        name="in_projection",
    )(x, scale, shift, w)


def _band_kernel(sink_ref, q_ref, kp_ref, ko_ref, vp_ref, vo_ref, bias_ref, o_ref, lse_ref, *,
                 hq, hk, use_sinks):
    b = pl.program_id(1)
    q = q_ref[0]
    kk = jnp.concatenate([kp_ref[0], ko_ref[0]], axis=0)
    vv = jnp.concatenate([vp_ref[0], vo_ref[0]], axis=0)
    col = lax.broadcasted_iota(I32, (BLK, 2 * BLK), 1)
    first_mask = jnp.where(jnp.logical_and(b == 0, col < BLK), NEG, 0.0)
    group = hq // hk
    for h in range(hq):
        g = h // group
        qh = q[:, h * HEAD_DIM:(h + 1) * HEAD_DIM]
        kh = kk[:, g * HEAD_DIM:(g + 1) * HEAD_DIM]
        vh = vv[:, g * HEAD_DIM:(g + 1) * HEAD_DIM]
        s = lax.dot_general(qh, kh, (((1,), (1,)), ((), ())), preferred_element_type=F32)
        s = s + bias_ref[h] + first_mask
        m = jnp.max(s, axis=-1, keepdims=True)
        if use_sinks:
            m = jnp.maximum(m, sink_ref[h])
        p = jnp.exp(s - m)
        l = jnp.sum(p, axis=-1, keepdims=True)
        if use_sinks:
            l = l + jnp.exp(sink_ref[h] - m)
        o = jnp.dot(p.astype(BF16), vh, preferred_element_type=F32)
        o_ref[0, :, h * HEAD_DIM:(h + 1) * HEAD_DIM] = (o / l).astype(o_ref.dtype)
        lse_ref[0, :, h:h + 1] = m + jnp.log(l)


def banded_attention(q, k, v, bias, sinks, out_dtype):
    N, L, qc = q.shape
    hq = qc // HEAD_DIM
    hk = k.shape[2] // HEAD_DIM
    use_sinks = sinks is not None
    if sinks is None:
        sinks = jnp.zeros((hq,), F32)
    cur = lambda n, b, s: (n, b, 0)
    prev = lambda n, b, s: (n, jnp.maximum(b - 1, 0), 0)
    return pl.pallas_call(
        functools.partial(_band_kernel, hq=hq, hk=hk, use_sinks=use_sinks),
        grid_spec=pltpu.PrefetchScalarGridSpec(
            num_scalar_prefetch=1,
            grid=(N, L // BLK),
            in_specs=[pl.BlockSpec((1, BLK, qc), cur),
                      pl.BlockSpec((1, BLK, hk * HEAD_DIM), prev),
                      pl.BlockSpec((1, BLK, hk * HEAD_DIM), cur),
                      pl.BlockSpec((1, BLK, hk * HEAD_DIM), prev),
                      pl.BlockSpec((1, BLK, hk * HEAD_DIM), cur),
                      pl.BlockSpec(bias.shape, lambda n, b, s: (0, 0, 0))],
            out_specs=[pl.BlockSpec((1, BLK, qc), cur),
                       pl.BlockSpec((1, BLK, hq), cur)]),
        out_shape=[jax.ShapeDtypeStruct((N, L, qc), out_dtype),
                   jax.ShapeDtypeStruct((N, L, hq), F32)],
        compiler_params=_cparams(("parallel", "arbitrary")),
        name="banded_attention",
    )(sinks, q, k, k, v, v, bias)


def band_bias(table, dilation, max_dist):
    qi = np.arange(BLK)[:, None]
    kj = np.arange(2 * BLK)[None, :]
    dist = qi + BLK - kj
    allowed = (dist >= 0) & (dist <= max_dist)
    bias = table[_bucket_np(dist * dilation)]
    bias = jnp.where(allowed[:, :, None], bias, NEG)
    return jnp.transpose(bias, (2, 0, 1)).astype(F32)


def _combine_kernel(*refs, n_pat, heads):
    o_refs, l_refs, out_ref = refs[:n_pat], refs[n_pat:2 * n_pat], refs[2 * n_pat]
    lses = [r[0] for r in l_refs]
    m = functools.reduce(jnp.maximum, lses)
    es = [jnp.exp(l - m) for l in lses]
    tot = functools.reduce(lambda a, b: a + b, es)
    ws = [e / tot for e in es]
    for h in range(heads):
        sl = slice(h * HEAD_DIM, (h + 1) * HEAD_DIM)
        acc = ws[0][:, h:h + 1] * o_refs[0][0, :, sl]
        for p in range(1, n_pat):
            acc = acc + ws[p][:, h:h + 1] * o_refs[p][0, :, sl]
        out_ref[0, :, sl] = acc.astype(out_ref.dtype)


def combine_patterns(outs, lses, tm=512):
    B, S, C = outs[0].shape
    heads = C // HEAD_DIM
    n_pat = len(outs)
    idx = lambda b, i: (b, i, 0)
    return pl.pallas_call(
        functools.partial(_combine_kernel, n_pat=n_pat, heads=heads),
        grid=(B, S // tm),
        in_specs=[pl.BlockSpec((1, tm, C), idx)] * n_pat + [pl.BlockSpec((1, tm, heads), idx)] * n_pat,
        out_specs=pl.BlockSpec((1, tm, C), idx),
        out_shape=jax.ShapeDtypeStruct((B, S, C), BF16),
        compiler_params=_cparams(("parallel", "parallel")),
        name="combine_patterns",
    )(*outs, *lses)


def _outproj_kernel(*refs, n_parts):
    o_refs, w_refs = refs[:n_parts], refs[n_parts:2 * n_parts]
    x_ref, gate_ref, g_ref, b_ref, out_ref = refs[2 * n_parts:]
    mix = jnp.dot(o_refs[0][0], w_refs[0][...], preferred_element_type=F32)
    for o_ref, w_ref in zip(o_refs[1:], w_refs[1:]):
        mix = mix + jnp.dot(o_ref[0], w_ref[...], preferred_element_type=F32)
    z = ALPHA * x_ref[0] + gate_ref[0] * mix
    out_ref[0] = _layer_norm(z, g_ref[...], b_ref[...])


def out_projection_ln(parts, weights, x, gate, ln_g, ln_b, tm=512):
    B, S, D = x.shape
    idx = lambda b, i: (b, i, 0)
    const2 = lambda b, i: (0, 0)
    return pl.pallas_call(
        functools.partial(_outproj_kernel, n_parts=len(parts)),
        grid=(B, S // tm),
        in_specs=([pl.BlockSpec((1, tm, p.shape[2]), idx) for p in parts]
                  + [pl.BlockSpec(w.shape, const2) for w in weights]
                  + [pl.BlockSpec((1, tm, D), idx),
                     pl.BlockSpec((1, 1, D), lambda b, i: (b, 0, 0)),
                     pl.BlockSpec((1, D), const2),
                     pl.BlockSpec((1, D), const2)]),
        out_specs=pl.BlockSpec((1, tm, D), idx),
        out_shape=jax.ShapeDtypeStruct((B, S, D), F32),
        compiler_params=_cparams(("parallel", "parallel")),
        name="out_projection_ln",
    )(*parts, *weights, x, gate, ln_g.reshape(1, D), ln_b.reshape(1, D))


def _dsa_kernel(cfar_ref, q_ref, qi_ref, wi_ref, kit_ref, kct_ref, va_ref, vb_ref, strip_ref, o_ref,
                sk_scr, qm_scr, qim_scr, wib_scr, m_scr, acc_scr, *, k_sel):
    i = pl.program_id(1)
    ck = DSA_CK
    sub = ck // LANES
    nck = (i * BLK + BLK + ck - 1) // ck
    nfar = jnp.maximum(i * BLK - FAR_DIST + 1, 0) // ck
    lane = lax.broadcasted_iota(I32, (BLK, LANES), 1)
    upper = lane >= HEAD_DIM

    for p in range(C_HEADS // 2):
        qp = q_ref[0, :, p * LANES:(p + 1) * LANES]
        qm_scr[2 * p] = jnp.where(upper, jnp.zeros_like(qp), qp)
        qm_scr[2 * p + 1] = jnp.where(upper, qp, jnp.zeros_like(qp))
    for p in range(IDX_HEADS // 2):
        qp = qi_ref[0, :, p * LANES:(p + 1) * LANES]
        qim_scr[2 * p] = jnp.where(upper, jnp.zeros_like(qp), qp)
        qim_scr[2 * p + 1] = jnp.where(upper, qp, jnp.zeros_like(qp))
    wscale = IDX_HEADS ** -0.5 * IDX_DIM ** -0.5
    for h in range(IDX_HEADS):
        wib_scr[h] = jnp.broadcast_to(wi_ref[0, :, h:h + 1] * wscale, (BLK, LANES))

    row_t = lax.broadcasted_iota(I32, (BLK, ck), 0) + i * BLK
    col_l = lax.broadcasted_iota(I32, (BLK, ck), 1)

    def score_chunk(j, carry):
        kt = kit_ref[0, j]
        sc = jnp.zeros((BLK, ck), F32)
        for h in range(IDX_HEADS):
            d = jnp.dot(qim_scr[h], kt, preferred_element_type=F32)
            w = wib_scr[h]
            sc = sc + jnp.maximum(d, 0.0) * jnp.concatenate([w] * sub, axis=1)
        bits = pltpu.bitcast(sc, I32)
        key = bits ^ ((bits >> 31) & 0x7FFFFFFF)
        key = jnp.where(col_l + j * ck <= row_t, key, INT_MIN)
        for u in range(sub):
            sk_scr[j * sub + u] = key[:, u * LANES:(u + 1) * LANES]
        return carry

    lax.fori_loop(0, nck, score_chunk, 0)

    def bit_pass(t, v_u):
        c_u = v_u | lax.shift_left(jnp.int32(1), 31 - t)
        c_s = c_u ^ INT_MIN

        def count_tile(g, acc):
            return acc + jnp.where(sk_scr[g] >= c_s, 1, 0)

        acc = lax.fori_loop(0, nck * sub, count_tile, jnp.zeros((BLK, LANES), I32))
        cnt = jnp.sum(acc.astype(F32), axis=1, keepdims=True)
        return jnp.where(cnt >= float(k_sel), c_u, v_u)

    v_u = lax.fori_loop(0, 32, bit_pass, jnp.zeros((BLK, LANES), I32))
    thr = jnp.maximum(v_u ^ INT_MIN, INT_MIN + 1)
    thr_wide = jnp.concatenate([thr] * sub, axis=1)

    m_scr[...] = jnp.full(m_scr.shape, NEG, F32)
    acc_scr[...] = jnp.zeros(acc_scr.shape, F32)

    def attend(j, near):
        skc = jnp.concatenate([sk_scr[j * sub + u] for u in range(sub)], axis=1)
        mb = jnp.where(skc >= thr_wide, 0.0, NEG)
        kt = kct_ref[0, j]
        va = va_ref[0, j]
        vb = vb_ref[0, j]
        if near:
            jb = (j * ck + NEAR_W) // LANES - (i + 1)
        for h in range(C_HEADS):
            s = jnp.dot(qm_scr[h], kt, preferred_element_type=F32) + mb
            if near:
                s = s + jnp.concatenate([strip_ref[h, jb + u] for u in range(sub)], axis=1)
                shift = 0.0
            else:
                shift = cfar_ref[h]
            m_old = m_scr[h]
            m_new = jnp.maximum(m_old, jnp.max(s, axis=1, keepdims=True) + shift)
            p = jnp.exp(s - (m_new - shift))
            pv = jnp.dot(p.astype(BF16), va if h % 2 == 0 else vb, preferred_element_type=F32)
            acc_scr[h] = jnp.exp(m_old - m_new) * acc_scr[h] + pv
            m_scr[h] = m_new

    def attend_far(j, carry):
        attend(j, False)
        return carry

    def attend_near(j, carry):
        attend(j, True)
        return carry

    lax.fori_loop(0, nfar, attend_far, 0)
    lax.fori_loop(nfar, nck, attend_near, 0)

    for p in range(C_HEADS // 2):
        a0 = acc_scr[2 * p]
        a1 = acc_scr[2 * p + 1]
        o = jnp.where(upper, a1 / pltpu.roll(a1, HEAD_DIM, 1), a0 / pltpu.roll(a0, HEAD_DIM, 1))
        o_ref[0, :, p * LANES:(p + 1) * LANES] = o.astype(o_ref.dtype)


def dsa_attention(q, qi, wi, ki, kc, vc, table):
    B, S, _ = q.shape
    ck = DSA_CK
    assert S % ck == 0
    nchunks = S // ck
    k_sel = min(TOPK_MAX, S // 4)

    def chunked_t(t):
        tt = jnp.transpose(t.reshape(B, nchunks, ck, HEAD_DIM), (0, 1, 3, 2))
        return jnp.concatenate([tt, tt], axis=2)

    ones = jnp.ones_like(vc)
    va = jnp.concatenate([vc, ones], axis=-1).reshape(B, nchunks, ck, LANES)
    vb = jnp.concatenate([ones, vc], axis=-1).reshape(B, nchunks, ck, LANES)
    dist = np.arange(BLK)[:, None] - BLK + NEAR_W - np.arange(STRIP_W)[None, :]
    strip = jnp.transpose(table[_bucket_np(dist)], (2, 0, 1)).astype(F32)
    strip = jnp.transpose(strip.reshape(C_HEADS, BLK, STRIP_W // LANES, LANES), (0, 2, 1, 3))
    cfar = table[REL_BUCKETS - 1].astype(F32)

    qblk = lambda b, i, s: (b, i, 0)
    per_b = lambda b, i, s: (b, 0, 0, 0)
    return pl.pallas_call(
        functools.partial(_dsa_kernel, k_sel=k_sel),
        grid_spec=pltpu.PrefetchScalarGridSpec(
            num_scalar_prefetch=1,
            grid=(B, S // BLK),
            in_specs=[pl.BlockSpec((1, BLK, C_HEADS * HEAD_DIM), qblk),
                      pl.BlockSpec((1, BLK, IDX_HEADS * IDX_DIM), qblk),
                      pl.BlockSpec((1, BLK, IDX_HEADS), qblk),
                      pl.BlockSpec((1, nchunks, LANES, ck), per_b),
                      pl.BlockSpec((1, nchunks, LANES, ck), per_b),
                      pl.BlockSpec((1, nchunks, ck, LANES), per_b),
                      pl.BlockSpec((1, nchunks, ck, LANES), per_b),
                      pl.BlockSpec(strip.shape, lambda b, i, s: (0, 0, 0, 0))],
            out_specs=pl.BlockSpec((1, BLK, C_HEADS * HEAD_DIM), qblk),
            scratch_shapes=[pltpu.VMEM((S // LANES, BLK, LANES), I32),
                            pltpu.VMEM((C_HEADS, BLK, LANES), BF16),
                            pltpu.VMEM((IDX_HEADS, BLK, LANES), BF16),
                            pltpu.VMEM((IDX_HEADS, BLK, LANES), F32),
                            pltpu.VMEM((C_HEADS, BLK, 1), F32),
                            pltpu.VMEM((C_HEADS, BLK, LANES), F32)]),
        out_shape=jax.ShapeDtypeStruct((B, S, C_HEADS * HEAD_DIM), BF16),
        compiler_params=_cparams(("parallel", "arbitrary")),
        name="dsa_attention",
    )(cfar, q, qi, wi, chunked_t(ki), chunked_t(kc), va, vb, strip)


def _ffn_step(x2d, w1_ref, w3_ref, w2_ref, acc):
    xb = x2d[...]
    h1 = jnp.dot(xb, w1_ref[0], preferred_element_type=F32)
    h3 = jnp.dot(xb, w3_ref[0], preferred_element_type=F32)
    a = (h1 * jax.nn.sigmoid(h1) * h3).astype(BF16)
    acc[...] += jnp.dot(a, w2_ref[0], preferred_element_type=F32)


def _ffn_dense_kernel(x_ref, sc_ref, sh_ref, gate_ref, g_ref, b_ref, w1_ref, w3_ref, w2_ref, out_ref,
                      x2d, acc, *, nf):
    f = pl.program_id(2)

    @pl.when(f == 0)
    def _():
        x2d[...] = (x_ref[0] * sc_ref[0] + sh_ref[0]).astype(BF16)
        acc[...] = jnp.zeros(acc.shape, F32)

    _ffn_step(x2d, w1_ref, w3_ref, w2_ref, acc)

    @pl.when(f == nf - 1)
    def _():
        z = ALPHA * x_ref[0] + gate_ref[0] * acc[...]
        out_ref[0] = _layer_norm(z, g_ref[...], b_ref[...])


def ffn_dense_ln(x, scale, shift, gate, ln_g, ln_b, w1, w3, w2, tm=512, tf=512):
    B, S, D = x.shape
    nf = D_FF // tf
    xi = lambda b, i, f: (b, i, 0)
    bi = lambda b, i, f: (b, 0, 0)
    c2 = lambda b, i, f: (0, 0)
    return pl.pallas_call(
        functools.partial(_ffn_dense_kernel, nf=nf),
        grid=(B, S // tm, nf),
        in_specs=[pl.BlockSpec((1, tm, D), xi),
                  pl.BlockSpec((1, 1, D), bi), pl.BlockSpec((1, 1, D), bi), pl.BlockSpec((1, 1, D), bi),
                  pl.BlockSpec((1, D), c2), pl.BlockSpec((1, D), c2),
                  pl.BlockSpec((1, D, tf), lambda b, i, f: (0, 0, f)),
                  pl.BlockSpec((1, D, tf), lambda b, i, f: (0, 0, f)),
                  pl.BlockSpec((1, tf, D), lambda b, i, f: (0, f, 0))],
        out_specs=pl.BlockSpec((1, tm, D), xi),
        out_shape=jax.ShapeDtypeStruct((B, S, D), F32),
        scratch_shapes=[pltpu.VMEM((tm, D), BF16), pltpu.VMEM((tm, D), F32)],
        compiler_params=_cparams(("parallel", "parallel", "arbitrary")),
        name="ffn_dense_ln",
    )(x, scale, shift, gate, ln_g.reshape(1, D), ln_b.reshape(1, D), w1, w3, w2)


def _ffn_expert_kernel(be_ref, tok_ref, tok_next_ref, dst_ref, h_hbm, w1_ref, w3_ref, w2_ref, y_hbm,
                       xbuf, ybuf, x2d, acc, gsem, ssem, *, nf, tm):
    i = pl.program_id(0)
    f = pl.program_id(1)
    n_blocks = pl.num_programs(0)
    slot = i % 2
    rows = tm * D_TILES

    def start_gather(idx_ref, s):
        def body(r, carry):
            src = pl.multiple_of(idx_ref[0, 0, r] * D_TILES, D_TILES)
            dst = pl.multiple_of(r * D_TILES, D_TILES)
            pltpu.make_async_copy(h_hbm.at[pl.ds(src, D_TILES)], xbuf.at[s, pl.ds(dst, D_TILES)],
                                  gsem.at[s]).start()
            return carry
        lax.fori_loop(0, tm, body, 0, unroll=8)

    def wait_gather(s):
        pltpu.make_async_copy(h_hbm.at[pl.ds(0, rows)], xbuf.at[s], gsem.at[s]).wait()

    def start_scatter():
        def body(r, carry):
            src = pl.multiple_of(r * D_TILES, D_TILES)
            dst = pl.multiple_of(dst_ref[0, 0, r] * D_TILES, D_TILES)
            pltpu.make_async_copy(ybuf.at[pl.ds(src, D_TILES)], y_hbm.at[pl.ds(dst, D_TILES)], ssem).start()
            return carry
        lax.fori_loop(0, tm, body, 0, unroll=8)

    def wait_scatter():
        pltpu.make_async_copy(ybuf, y_hbm.at[pl.ds(0, rows)], ssem).wait()

    @pl.when(f == 0)
    def _():
        @pl.when(i == 0)
        def _():
            start_gather(tok_ref, 0)

        @pl.when(i + 1 < n_blocks)
        def _():
            start_gather(tok_next_ref, 1 - slot)

        wait_gather(slot)
        for j in range(D_TILES):
            x2d[:, j * LANES:(j + 1) * LANES] = xbuf[slot, pl.ds(j, tm, stride=D_TILES), :].astype(BF16)
        acc[...] = jnp.zeros(acc.shape, F32)

    _ffn_step(x2d, w1_ref, w3_ref, w2_ref, acc)

    @pl.when(f == nf - 1)
    def _():
        @pl.when(i > 0)
        def _():
            wait_scatter()

        for j in range(D_TILES):
            ybuf[pl.ds(j, tm, stride=D_TILES), :] = acc[:, j * LANES:(j + 1) * LANES]
        start_scatter()

        @pl.when(i == n_blocks - 1)
        def _():
            wait_scatter()


def ffn_experts(h_tiles, slot_tok, slot_dst, block_expert, w1, w3, w2, tf=512):
    tm = MOE_BLOCK
    P = slot_tok.shape[0]
    n_blocks = P // tm
    nf = D_FF // tf
    tok3 = slot_tok.reshape(n_blocks, 1, tm)
    dst3 = slot_dst.reshape(n_blocks, 1, tm)
    cur = lambda i, f, be: (i, 0, 0)
    nxt = lambda i, f, be: (jnp.minimum(i + 1, n_blocks - 1), 0, 0)
    return pl.pallas_call(
        functools.partial(_ffn_expert_kernel, nf=nf, tm=tm),
        grid_spec=pltpu.PrefetchScalarGridSpec(
            num_scalar_prefetch=1,
            grid=(n_blocks, nf),
            in_specs=[pl.BlockSpec((1, 1, tm), cur, memory_space=pltpu.SMEM),
                      pl.BlockSpec((1, 1, tm), nxt, memory_space=pltpu.SMEM),
                      pl.BlockSpec((1, 1, tm), cur, memory_space=pltpu.SMEM),
                      pl.BlockSpec(memory_space=pl.ANY),
                      pl.BlockSpec((1, D_MODEL, tf), lambda i, f, be: (be[i], 0, f)),
                      pl.BlockSpec((1, D_MODEL, tf), lambda i, f, be: (be[i], 0, f)),
                      pl.BlockSpec((1, tf, D_MODEL), lambda i, f, be: (be[i], f, 0))],
            out_specs=pl.BlockSpec(memory_space=pl.ANY),
            scratch_shapes=[pltpu.VMEM((2, tm * D_TILES, LANES), F32),
                            pltpu.VMEM((tm * D_TILES, LANES), F32),
                            pltpu.VMEM((tm, D_MODEL), BF16),
                            pltpu.VMEM((tm, D_MODEL), F32),
                            pltpu.SemaphoreType.DMA((2,)),
                            pltpu.SemaphoreType.DMA(())]),
        out_shape=jax.ShapeDtypeStruct((P * D_TILES, LANES), F32),
        compiler_params=_cparams(("arbitrary", "arbitrary")),
        name="ffn_experts",
    )(block_expert, tok3, tok3, dst3, h_tiles, w1, w3, w2)


def _router_kernel(x_ref, sc_ref, sh_ref, wr_ref, h_ref, idx_ref, gate_ref, *, tm):
    h = x_ref[0] * sc_ref[0] + sh_ref[0]
    for j in range(D_TILES):
        h_ref[pl.ds(j, tm, stride=D_TILES), :] = h[:, j * LANES:(j + 1) * LANES]
    logits = jnp.dot(h, wr_ref[...], preferred_element_type=F32, precision=lax.Precision.HIGHEST)
    lane = lax.broadcasted_iota(I32, logits.shape, 1)
    lg = jnp.where(lane < N_EXPERTS, logits, -jnp.inf)
    m1 = jnp.max(lg, axis=1, keepdims=True)
    i1 = jnp.min(jnp.where(lg == m1, lane, LANES), axis=1, keepdims=True)
    lg2 = jnp.where(lane == i1, -jnp.inf, lg)
    m2 = jnp.max(lg2, axis=1, keepdims=True)
    i2 = jnp.min(jnp.where(lg2 == m2, lane, LANES), axis=1, keepdims=True)
    e = jnp.exp(m2 - m1)
    idx_ref[:, 0:1] = i1
    idx_ref[:, 1:2] = i2
    gate_ref[:, 0:1] = 1.0 / (1.0 + e)
    gate_ref[:, 1:2] = e / (1.0 + e)


def route_tokens(x, scale, shift, w_router, tm=512):
    B, S, D = x.shape
    T = B * S
    nb = S // tm
    wr = jnp.pad(w_router, ((0, 0), (0, LANES - N_EXPERTS)))
    return pl.pallas_call(
        functools.partial(_router_kernel, tm=tm),
        grid=(B, nb),
        in_specs=[pl.BlockSpec((1, tm, D), lambda b, i: (b, i, 0)),
                  pl.BlockSpec((1, 1, D), lambda b, i: (b, 0, 0)),
                  pl.BlockSpec((1, 1, D), lambda b, i: (b, 0, 0)),
                  pl.BlockSpec((D, LANES), lambda b, i: (0, 0))],
        out_specs=[pl.BlockSpec((tm * D_TILES, LANES), lambda b, i: (b * nb + i, 0)),
                   pl.BlockSpec((tm, TOP_K), lambda b, i: (b * nb + i, 0)),
                   pl.BlockSpec((tm, TOP_K), lambda b, i: (b * nb + i, 0))],
        out_shape=[jax.ShapeDtypeStruct((T * D_TILES, LANES), F32),
                   jax.ShapeDtypeStruct((T, TOP_K), I32),
                   jax.ShapeDtypeStruct((T, TOP_K), F32)],
        compiler_params=_cparams(("parallel", "parallel")),
        name="route_tokens",
    )(x, scale, shift, wr)


def _moe_combine_kernel(y_ref, gt_ref, x_ref, gate_ref, g_ref, b_ref, out_ref, *, tm):
    g0 = gt_ref[:, 0:1]
    g1 = gt_ref[:, 1:2]
    stride = TOP_K * D_TILES
    pieces = []
    for j in range(D_TILES):
        sl = slice(j * LANES, (j + 1) * LANES)
        ff = g0 * y_ref[pl.ds(j, tm, stride=stride), :] + g1 * y_ref[pl.ds(D_TILES + j, tm, stride=stride), :]
        pieces.append(ALPHA * x_ref[0, :, sl] + gate_ref[0, :, sl] * ff)
    z = jnp.concatenate(pieces, axis=1)
    out_ref[0] = _layer_norm(z, g_ref[...], b_ref[...])


def moe_combine_ln(yg, gates, x, gate, ln_g, ln_b, tm=512):
    B, S, D = x.shape
    nb = S // tm
    return pl.pallas_call(
        functools.partial(_moe_combine_kernel, tm=tm),
        grid=(B, nb),
        in_specs=[pl.BlockSpec((tm * TOP_K * D_TILES, LANES), lambda b, i: (b * nb + i, 0)),
                  pl.BlockSpec((tm, TOP_K), lambda b, i: (b * nb + i, 0)),
                  pl.BlockSpec((1, tm, D), lambda b, i: (b, i, 0)),
                  pl.BlockSpec((1, 1, D), lambda b, i: (b, 0, 0)),
                  pl.BlockSpec((1, D), lambda b, i: (0, 0)),
                  pl.BlockSpec((1, D), lambda b, i: (0, 0))],
        out_specs=pl.BlockSpec((1, tm, D), lambda b, i: (b, i, 0)),
        out_shape=jax.ShapeDtypeStruct((B, S, D), F32),
        compiler_params=_cparams(("parallel", "parallel")),
        name="moe_combine_ln",
    )(yg, gates, x, gate, ln_g.reshape(1, D), ln_b.reshape(1, D))


def _scale_cols(w, start, width, factor):
    return w.at[:, start:start + width].multiply(factor)


def even_mixer_layer(x, mod, w_in, w_out, rel_table, sinks, ln_g, ln_b):
    B, S, D = x.shape
    assert S % A_PAD == 0
    shift, scale, gate = mod
    ah, bh, bk = A_HEADS * HEAD_DIM, B_HEADS * HEAD_DIM, B_KV_HEADS * HEAD_DIM
    w = _scale_cols(w_in, 0, ah, HEAD_DIM ** -0.5)
    w = _scale_cols(w, 3 * ah, bh, HEAD_DIM ** -0.5).astype(BF16)
    groups = [(ah, BF16)] * 3 + [(bh, BF16), (bk, BF16), (bk, BF16)]
    qa, ka, va, qb, kb, vb = in_projection(x, scale, shift, w, groups)

    outs, lses = [], []
    for window, d in A_PATTERNS:
        ld = S // d

        def regroup(t):
            return jnp.transpose(t.reshape(B, ld, d, ah), (0, 2, 1, 3)).reshape(B * d, ld, ah)

        bias = band_bias(rel_table[:, :A_HEADS], d, window // d)
        o, lse = banded_attention(regroup(qa), regroup(ka), regroup(va), bias, None, F32)
        outs.append(jnp.transpose(o.reshape(B, d, ld, ah), (0, 2, 1, 3)).reshape(B, S, ah))
        lses.append(jnp.transpose(lse.reshape(B, d, ld, A_HEADS), (0, 2, 1, 3)).reshape(B, S, A_HEADS))
    oa = combine_patterns(outs, lses)

    bias_b = band_bias(rel_table[:, A_HEADS:A_HEADS + B_HEADS], 1, B_WINDOW - 1)
    ob, _ = banded_attention(qb, kb, vb, bias_b, sinks.astype(F32), BF16)

    wo = w_out.astype(BF16)
    return out_projection_ln([oa, ob], [wo[:ah], wo[ah:]], x, gate, ln_g, ln_b)


def dsa_mixer_layer(x, mod, w_in, w_out, rel_table, ln_g, ln_b):
    shift, scale, gate = mod
    qw = C_HEADS * HEAD_DIM
    iw = IDX_HEADS * IDX_DIM
    q_w, kc_w, vc_w, qi_w, ki_w, wi_w = jnp.split(
        w_in, [qw, qw + HEAD_DIM, qw + 2 * HEAD_DIM, qw + 2 * HEAD_DIM + iw, qw + 2 * HEAD_DIM + iw + IDX_DIM],
        axis=1)
    pad = jnp.zeros((D_MODEL, LANES - IDX_DIM - IDX_HEADS), w_in.dtype)
    w = jnp.concatenate([q_w * HEAD_DIM ** -0.5, kc_w, vc_w, qi_w, ki_w, wi_w, pad], axis=1).astype(BF16)
    groups = [(qw, BF16), (2 * HEAD_DIM, BF16), (iw, BF16), (LANES, F32)]
    q, kv, qi, kw = in_projection(x, scale, shift, w, groups)
    kc, vc = kv[..., :HEAD_DIM], kv[..., HEAD_DIM:]
    ki = kw[..., :IDX_DIM].astype(BF16)
    wi = kw[..., IDX_DIM:IDX_DIM + IDX_HEADS]
    o = dsa_attention(q, qi, wi, ki, kc, vc, rel_table[:, :C_HEADS])
    return out_projection_ln([o], [w_out.astype(BF16)], x, gate, ln_g, ln_b)


def moe_layer(x, mod, w_router, w1, w3, w2, ln_g, ln_b):
    B, S, D = x.shape
    shift, scale, gate = mod
    T = B * S
    A = T * TOP_K
    h_tiles, top_idx, gates = route_tokens(x, scale, shift, w_router)

    e_flat = top_idx.reshape(-1)
    onehot = (e_flat[:, None] == jnp.arange(N_EXPERTS)[None, :]).astype(I32)
    rank = jnp.take_along_axis(jnp.cumsum(onehot, axis=0) - onehot, e_flat[:, None], axis=1)[:, 0]
    counts = jnp.sum(onehot, axis=0)
    padded = (counts + MOE_BLOCK - 1) // MOE_BLOCK * MOE_BLOCK
    pends = jnp.cumsum(padded)
    dest = (pends - padded)[e_flat] + rank
    n_blocks = -(-A // MOE_BLOCK) + N_EXPERTS
    P = n_blocks * MOE_BLOCK
    slot_src = jnp.full((P,), -1, I32).at[dest].set(jnp.arange(A, dtype=I32))
    is_pad = slot_src < 0
    slot_tok = jnp.where(is_pad, 0, slot_src // TOP_K)
    slot_dst = jnp.where(is_pad, A - 1 + jnp.cumsum(is_pad.astype(I32)), slot_src)
    block_expert = jnp.minimum(
        jnp.searchsorted(pends, jnp.arange(n_blocks) * MOE_BLOCK, side='right'), N_EXPERTS - 1).astype(I32)

    yg = ffn_experts(h_tiles, slot_tok, slot_dst, block_expert,
                     w1.astype(BF16), w3.astype(BF16), w2.astype(BF16))
    return moe_combine_ln(yg, gates, x, gate, ln_g, ln_b)


def kernel(x, c, rel_table, w_in_even, w_out_even, sinks, w_in_odd, w_out_odd, ffn_w1, ffn_w3, ffn_w2,
           router, exp_w1, exp_w3, exp_w2, ada_w, ada_b, ln_g, ln_b):
    D = D_MODEL
    mods = ada_modulation_all(c, ada_w, ada_b)

    def mod(layer, sub):
        m = mods[2 * layer + sub]
        return m[:, None, :D], m[:, None, D:2 * D], m[:, None, 2 * D:]

    for layer in range(DEPTH):
        i = layer // 2
        if layer % 2 == 0:
            x = even_mixer_layer(x, mod(layer, 0), w_in_even[i], w_out_even[i], rel_table, sinks[i],
                                 ln_g[layer, 0], ln_b[layer, 0])
            shift, scale, gate = mod(layer, 1)
            x = ffn_dense_ln(x, scale, shift, gate, ln_g[layer, 1], ln_b[layer, 1],
                             ffn_w1[i][None].astype(BF16), ffn_w3[i][None].astype(BF16),
                             ffn_w2[i][None].astype(BF16))
        else:
            x = dsa_mixer_layer(x, mod(layer, 0), w_in_odd[i], w_out_odd[i], rel_table,
                                ln_g[layer, 0], ln_b[layer, 0])
            x = moe_layer(x, mod(layer, 1), router[i], exp_w1[i], exp_w3[i], exp_w2[i],
                          ln_g[layer, 1], ln_b[layer, 1])
    return x
```

```python
import functools
import math

import numpy as np
import jax
import jax.numpy as jnp
from jax import lax
from jax.experimental import pallas as pl
from jax.experimental.pallas import tpu as pltpu

F32 = jnp.float32
BF16 = jnp.bfloat16
I32 = jnp.int32

D_MODEL = 1024
HEAD_DIM = 64
BLK = 128
A_HEADS = 8
A_PATTERNS = ((128, 1), (512, 4), (2048, 16))
A_PAD = BLK * 16
B_HEADS = 8
B_KV_HEADS = 2
B_WINDOW = 128
C_HEADS = 16
IDX_HEADS = 8
IDX_DIM = 64
TOPK_MAX = 256
REL_BUCKETS = 32
REL_MAX_DIST = 2048
D_FF = 3584
N_EXPERTS = 8
TOP_K = 2
MOE_BLOCK = 512
DEPTH = 4
ALPHA = (2 * DEPTH) ** 0.25
LN_EPS = 1e-5

LANES = 128
SUBLANES = 8
D_TILES = D_MODEL // LANES
NEG = -1e30
INT_MIN = -(2 ** 31)
VMEM_LIMIT = 56 * 1024 * 1024

DSA_CK = 256
DSA_GK = 512


def _bucket_np(dist):
    n = np.maximum(dist, 0)
    max_exact = REL_BUCKETS // 2
    nf = np.maximum(n, 1).astype(np.float32)
    large = max_exact + (np.log(nf / np.float32(max_exact)) / np.float32(math.log(REL_MAX_DIST / max_exact))
                         * np.float32(REL_BUCKETS - max_exact)).astype(np.int32)
    large = np.minimum(large, REL_BUCKETS - 1)
    return np.where(n < max_exact, n, large).astype(np.int32)


def _far_distance():
    b = _bucket_np(np.arange(0, 2 * REL_MAX_DIST))
    return int(np.max(np.nonzero(b != REL_BUCKETS - 1)[0])) + 1


FAR_DIST = _far_distance()
N_NEAR = -(-(FAR_DIST + BLK - 1) // BLK)


def _cparams(sem):
    return pltpu.CompilerParams(dimension_semantics=sem, vmem_limit_bytes=VMEM_LIMIT)


def _layer_norm(z, g, b):
    mu = jnp.mean(z, axis=-1, keepdims=True)
    zc = z - mu
    var = jnp.mean(zc * zc, axis=-1, keepdims=True)
    return zc * lax.rsqrt(var + LN_EPS) * g + b


def _ada_kernel(c_ref, w_ref, b_ref, o_ref):
    j = pl.program_id(1)
    c = c_ref[...]
    sc = c * jax.nn.sigmoid(c)
    mod = jnp.dot(sc, w_ref[0], preferred_element_type=F32, precision=lax.Precision.HIGHEST)
    o_ref[0] = mod + b_ref[0] + jnp.where(j >= 1, 1.0, 0.0)


def ada_modulation_all(c, ada_w, ada_b):
    B, D = c.shape
    n = ada_w.shape[0] * ada_w.shape[1]
    rows = -(-B // SUBLANES) * SUBLANES
    cp = jnp.pad(c, ((0, rows - B), (0, 0)))
    w = ada_w.reshape(n, D, 3 * D)
    b = ada_b.reshape(n, 1, 3 * D)
    out = pl.pallas_call(
        _ada_kernel,
        grid=(n, 3),
        in_specs=[pl.BlockSpec((rows, D), lambda l, j: (0, 0)),
                  pl.BlockSpec((1, D, D), lambda l, j: (l, 0, j)),
                  pl.BlockSpec((1, 1, D), lambda l, j: (l, 0, j))],
        out_specs=pl.BlockSpec((1, rows, D), lambda l, j: (l, 0, j)),
        out_shape=jax.ShapeDtypeStruct((n, rows, 3 * D), F32),
        compiler_params=_cparams(("arbitrary", "arbitrary")),
        name="ada_modulation",
    )(cp, w, b)
    return out[:, :B]


def _inproj_kernel(x_ref, sc_ref, sh_ref, w_ref, *o_refs, splits):
    h = (x_ref[0] * sc_ref[0] + sh_ref[0]).astype(BF16)
    for o_ref, (start, width) in zip(o_refs, splits):
        o_ref[0] = jnp.dot(h, w_ref[:, start:start + width],
                           preferred_element_type=F32).astype(o_ref.dtype)


def in_projection(x, scale, shift, w, groups, tm=512):
    B, S, D = x.shape
    splits, start = [], 0
    for width, _ in groups:
        splits.append((start, width))
        start += width
    assert start == w.shape[1] and S % tm == 0
    return pl.pallas_call(
        functools.partial(_inproj_kernel, splits=tuple(splits)),
        grid=(B, S // tm),
        in_specs=[pl.BlockSpec((1, tm, D), lambda b, i: (b, i, 0)),
                  pl.BlockSpec((1, 1, D), lambda b, i: (b, 0, 0)),
                  pl.BlockSpec((1, 1, D), lambda b, i: (b, 0, 0)),
                  pl.BlockSpec(w.shape, lambda b, i: (0, 0))],
        out_specs=[pl.BlockSpec((1, tm, width), lambda b, i: (b, i, 0)) for width, _ in groups],
        out_shape=[jax.ShapeDtypeStruct((B, S, width), dt) for width, dt in groups],
        compiler_params=_cparams(("parallel", "parallel")),
        name="in_projection",
    )(x, scale, shift, w)


def _band_kernel(sink_ref, q_ref, kp_ref, ko_ref, vp_ref, vo_ref, bias_ref, o_ref, lse_ref, *,
                 hq, hk, use_sinks):
    b = pl.program_id(1)
    q = q_ref[0]
    kk = jnp.concatenate([kp_ref[0], ko_ref[0]], axis=0)
    vv = jnp.concatenate([vp_ref[0], vo_ref[0]], axis=0)
    col = lax.broadcasted_iota(I32, (BLK, 2 * BLK), 1)
    first_mask = jnp.where(jnp.logical_and(b == 0, col < BLK), NEG, 0.0)
    group = hq // hk
    for h in range(hq):
        g = h // group
        qh = q[:, h * HEAD_DIM:(h + 1) * HEAD_DIM]
        kh = kk[:, g * HEAD_DIM:(g + 1) * HEAD_DIM]
        vh = vv[:, g * HEAD_DIM:(g + 1) * HEAD_DIM]
        s = lax.dot_general(qh, kh, (((1,), (1,)), ((), ())), preferred_element_type=F32)
        s = s + bias_ref[h] + first_mask
        m = jnp.max(s, axis=-1, keepdims=True)
        if use_sinks:
            m = jnp.maximum(m, sink_ref[h])
        p = jnp.exp(s - m)
        l = jnp.sum(p, axis=-1, keepdims=True)
        if use_sinks:
            l = l + jnp.exp(sink_ref[h] - m)
        o = jnp.dot(p.astype(BF16), vh, preferred_element_type=F32)
        o_ref[0, :, h * HEAD_DIM:(h + 1) * HEAD_DIM] = (o / l).astype(o_ref.dtype)
        lse_ref[0, :, h:h + 1] = m + jnp.log(l)


def banded_attention(q, k, v, bias, sinks, out_dtype):
    N, L, qc = q.shape
    hq = qc // HEAD_DIM
    hk = k.shape[2] // HEAD_DIM
    use_sinks = sinks is not None
    if sinks is None:
        sinks = jnp.zeros((hq,), F32)
    cur = lambda n, b, s: (n, b, 0)
    prev = lambda n, b, s: (n, jnp.maximum(b - 1, 0), 0)
    return pl.pallas_call(
        functools.partial(_band_kernel, hq=hq, hk=hk, use_sinks=use_sinks),
        grid_spec=pltpu.PrefetchScalarGridSpec(
            num_scalar_prefetch=1,
            grid=(N, L // BLK),
            in_specs=[pl.BlockSpec((1, BLK, qc), cur),
                      pl.BlockSpec((1, BLK, hk * HEAD_DIM), prev),
                      pl.BlockSpec((1, BLK, hk * HEAD_DIM), cur),
                      pl.BlockSpec((1, BLK, hk * HEAD_DIM), prev),
                      pl.BlockSpec((1, BLK, hk * HEAD_DIM), cur),
                      pl.BlockSpec(bias.shape, lambda n, b, s: (0, 0, 0))],
            out_specs=[pl.BlockSpec((1, BLK, qc), cur),
                       pl.BlockSpec((1, BLK, hq), cur)]),
        out_shape=[jax.ShapeDtypeStruct((N, L, qc), out_dtype),
                   jax.ShapeDtypeStruct((N, L, hq), F32)],
        compiler_params=_cparams(("parallel", "arbitrary")),
        name="banded_attention",
    )(sinks, q, k, k, v, v, bias)


def band_bias(table, dilation, max_dist):
    qi = np.arange(BLK)[:, None]
    kj = np.arange(2 * BLK)[None, :]
    dist = qi + BLK - kj
    allowed = (dist >= 0) & (dist <= max_dist)
    bias = table[_bucket_np(dist * dilation)]
    bias = jnp.where(allowed[:, :, None], bias, NEG)
    return jnp.transpose(bias, (2, 0, 1)).astype(F32)


def _combine_kernel(*refs, n_pat, heads):
    o_refs, l_refs, out_ref = refs[:n_pat], refs[n_pat:2 * n_pat], refs[2 * n_pat]
    lses = [r[0] for r in l_refs]
    m = functools.reduce(jnp.maximum, lses)
    es = [jnp.exp(l - m) for l in lses]
    tot = functools.reduce(lambda a, b: a + b, es)
    ws = [e / tot for e in es]
    for h in range(heads):
        sl = slice(h * HEAD_DIM, (h + 1) * HEAD_DIM)
        acc = ws[0][:, h:h + 1] * o_refs[0][0, :, sl]
        for p in range(1, n_pat):
            acc = acc + ws[p][:, h:h + 1] * o_refs[p][0, :, sl]
        out_ref[0, :, sl] = acc.astype(out_ref.dtype)


def combine_patterns(outs, lses, tm=512):
    B, S, C = outs[0].shape
    heads = C // HEAD_DIM
    n_pat = len(outs)
    idx = lambda b, i: (b, i, 0)
    return pl.pallas_call(
        functools.partial(_combine_kernel, n_pat=n_pat, heads=heads),
        grid=(B, S // tm),
        in_specs=[pl.BlockSpec((1, tm, C), idx)] * n_pat + [pl.BlockSpec((1, tm, heads), idx)] * n_pat,
        out_specs=pl.BlockSpec((1, tm, C), idx),
        out_shape=jax.ShapeDtypeStruct((B, S, C), BF16),
        compiler_params=_cparams(("parallel", "parallel")),
        name="combine_patterns",
    )(*outs, *lses)


def _outproj_kernel(*refs, n_parts):
    o_refs, w_refs = refs[:n_parts], refs[n_parts:2 * n_parts]
    x_ref, gate_ref, g_ref, b_ref, out_ref = refs[2 * n_parts:]
    mix = jnp.dot(o_refs[0][0], w_refs[0][...], preferred_element_type=F32)
    for o_ref, w_ref in zip(o_refs[1:], w_refs[1:]):
        mix = mix + jnp.dot(o_ref[0], w_ref[...], preferred_element_type=F32)
    z = ALPHA * x_ref[0] + gate_ref[0] * mix
    out_ref[0] = _layer_norm(z, g_ref[...], b_ref[...])


def out_projection_ln(parts, weights, x, gate, ln_g, ln_b, tm=512):
    B, S, D = x.shape
    idx = lambda b, i: (b, i, 0)
    const2 = lambda b, i: (0, 0)
    return pl.pallas_call(
        functools.partial(_outproj_kernel, n_parts=len(parts)),
        grid=(B, S // tm),
        in_specs=([pl.BlockSpec((1, tm, p.shape[2]), idx) for p in parts]
                  + [pl.BlockSpec(w.shape, const2) for w in weights]
                  + [pl.BlockSpec((1, tm, D), idx),
                     pl.BlockSpec((1, 1, D), lambda b, i: (b, 0, 0)),
                     pl.BlockSpec((1, D), const2),
                     pl.BlockSpec((1, D), const2)]),
        out_specs=pl.BlockSpec((1, tm, D), idx),
        out_shape=jax.ShapeDtypeStruct((B, S, D), F32),
        compiler_params=_cparams(("parallel", "parallel")),
        name="out_projection_ln",
    )(*parts, *weights, x, gate, ln_g.reshape(1, D), ln_b.reshape(1, D))


def _dsa_kernel(q_ref, qi_ref, wi_ref, kit_ref, kct_ref, va_ref, vb_ref, strip_ref, o_ref,
                sk_scr, qm_scr, qim_scr, wib_scr, mb_scr, s_scr, m_scr, acc_scr, *, k_sel):
    i = pl.program_id(1)
    ck = DSA_CK
    sub = ck // LANES
    gt = DSA_GK // LANES
    gc = DSA_GK // ck
    ngr = (i * BLK + BLK + DSA_GK - 1) // DSA_GK
    nck = ngr * gc
    lane = lax.broadcasted_iota(I32, (BLK, LANES), 1)
    upper = lane >= HEAD_DIM

    for p in range(C_HEADS // 2):
        qp = q_ref[0, :, p * LANES:(p + 1) * LANES]
        qm_scr[2 * p] = jnp.where(upper, jnp.zeros_like(qp), qp)
        qm_scr[2 * p + 1] = jnp.where(upper, qp, jnp.zeros_like(qp))
    for p in range(IDX_HEADS // 2):
        qp = qi_ref[0, :, p * LANES:(p + 1) * LANES]
        qim_scr[2 * p] = jnp.where(upper, jnp.zeros_like(qp), qp)
        qim_scr[2 * p + 1] = jnp.where(upper, qp, jnp.zeros_like(qp))
    wscale = IDX_HEADS ** -0.5 * IDX_DIM ** -0.5
    for h in range(IDX_HEADS):
        wib_scr[h] = jnp.broadcast_to(wi_ref[0, :, h:h + 1] * wscale, (BLK, LANES))

    row_t = lax.broadcasted_iota(I32, (BLK, ck), 0) + i * BLK
    col_l = lax.broadcasted_iota(I32, (BLK, ck), 1)

    def score_chunk(j, carry):
        kt = kit_ref[0, j]
        sc = jnp.zeros((BLK, ck), F32)
        for h in range(IDX_HEADS):
            d = jnp.dot(qim_scr[h], kt, preferred_element_type=F32)
            w = wib_scr[h]
            sc = sc + jnp.maximum(d, 0.0) * jnp.concatenate([w] * sub, axis=1)
        bits = pltpu.bitcast(sc, I32)
        key = bits ^ ((bits >> 31) & 0x7FFFFFFF)
        key = jnp.where(col_l + j * ck <= row_t, key, INT_MIN)
        for u in range(sub):
            sk_scr[j * sub + u] = key[:, u * LANES:(u + 1) * LANES]
        return carry

    lax.fori_loop(0, nck, score_chunk, 0)

    def bit_pass(t, v_u):
        c_u = v_u | lax.shift_left(jnp.int32(1), 31 - t)
        c_s = c_u ^ INT_MIN

        def count_group(g, acc):
            for u in range(gt):
                acc = acc + jnp.where(sk_scr[g * gt + u] >= c_s, 1, 0)
            return acc

        acc = lax.fori_loop(0, ngr, count_group, jnp.zeros((BLK, LANES), I32))
        cnt = jnp.sum(acc.astype(F32), axis=1, keepdims=True)
        return jnp.where(cnt >= float(k_sel), c_u, v_u)

    v_u = lax.fori_loop(0, 32, bit_pass, jnp.zeros((BLK, LANES), I32))
    thr = jnp.maximum(v_u ^ INT_MIN, INT_MIN + 1)

    m_scr[...] = jnp.full(m_scr.shape, NEG, F32)
    acc_scr[...] = jnp.zeros(acc_scr.shape, F32)

    def attend_group(g, carry):
        for u in range(gt):
            mb_scr[u] = jnp.where(sk_scr[g * gt + u] >= thr, 0.0, NEG)
        tiles = [jnp.clip(i - (g * gt + u), 0, N_NEAR) for u in range(gt)]

        def logits(h):
            mx = None
            for c in range(gc):
                s = jnp.dot(qm_scr[h], kct_ref[0, g * gc + c], preferred_element_type=F32)
                for u in range(sub):
                    t = c * sub + u
                    piece = s[:, u * LANES:(u + 1) * LANES] + mb_scr[t] + strip_ref[h, tiles[t]]
                    s_scr[h % 2, t] = piece
                    mx = piece if mx is None else jnp.maximum(mx, piece)
            m_old = m_scr[h]
            m_new = jnp.maximum(m_old, jnp.max(mx, axis=1, keepdims=True))
            m_scr[h] = m_new
            return m_old, m_new

        def accumulate(h, m_old, m_new):
            p = jnp.concatenate([jnp.exp(s_scr[h % 2, t] - m_new) for t in range(gt)], axis=1)
            pv = jnp.dot(p.astype(BF16), (va_ref if h % 2 == 0 else vb_ref)[0, g],
                         preferred_element_type=F32)
            acc_scr[h] = jnp.exp(m_old - m_new) * acc_scr[h] + pv

        stats = logits(0)
        for h in range(C_HEADS):
            nxt = logits(h + 1) if h + 1 < C_HEADS else None
            accumulate(h, *stats)
            stats = nxt
        return carry

    lax.fori_loop(0, ngr, attend_group, 0)

    for p in range(C_HEADS // 2):
        a0 = acc_scr[2 * p]
        a1 = acc_scr[2 * p + 1]
        o = jnp.where(upper, a1 / pltpu.roll(a1, HEAD_DIM, 1), a0 / pltpu.roll(a0, HEAD_DIM, 1))
        o_ref[0, :, p * LANES:(p + 1) * LANES] = o.astype(o_ref.dtype)


def dsa_attention(q, qi, wi, ki, kc, vc, table):
    B, S, _ = q.shape
    ck = DSA_CK
    assert S % DSA_GK == 0
    nchunks = S // ck
    k_sel = min(TOPK_MAX, S // 4)

    def chunked_t(t):
        tt = jnp.transpose(t.reshape(B, nchunks, ck, HEAD_DIM), (0, 1, 3, 2))
        return jnp.concatenate([tt, tt], axis=2)

    ngroups = S // DSA_GK
    ones = jnp.ones_like(vc)
    va = jnp.concatenate([vc, ones], axis=-1).reshape(B, ngroups, DSA_GK, LANES)
    vb = jnp.concatenate([ones, vc], axis=-1).reshape(B, ngroups, DSA_GK, LANES)
    j = np.arange(2 * BLK)
    offs = np.where(j <= BLK, -j, 2 * BLK - j)
    v = table[_bucket_np(np.arange(N_NEAR)[:, None] * BLK + offs[None, :])]
    v = jnp.transpose(v, (2, 0, 1)).astype(F32)
    near = jnp.tile(v, (1, 1, BLK))[..., :BLK * (2 * BLK - 1)]
    near = near.reshape(C_HEADS, N_NEAR, BLK, 2 * BLK - 1)[..., :BLK]
    far = jnp.broadcast_to(table[REL_BUCKETS - 1].astype(F32)[:, None, None, None],
                           (C_HEADS, 1, BLK, LANES))
    strip = jnp.concatenate([near, far], axis=1)

    once = pl.Buffered(1)
    qblk = lambda b, i: (b, i, 0)
    per_b = lambda b, i: (b, 0, 0, 0)
    return pl.pallas_call(
        functools.partial(_dsa_kernel, k_sel=k_sel),
        grid=(B, S // BLK),
        in_specs=[pl.BlockSpec((1, BLK, C_HEADS * HEAD_DIM), qblk),
                  pl.BlockSpec((1, BLK, IDX_HEADS * IDX_DIM), qblk),
                  pl.BlockSpec((1, BLK, IDX_HEADS), qblk),
                  pl.BlockSpec((1, nchunks, LANES, ck), per_b, pipeline_mode=once),
                  pl.BlockSpec((1, nchunks, LANES, ck), per_b, pipeline_mode=once),
                  pl.BlockSpec((1, ngroups, DSA_GK, LANES), per_b, pipeline_mode=once),
                  pl.BlockSpec((1, ngroups, DSA_GK, LANES), per_b, pipeline_mode=once),
                  pl.BlockSpec(strip.shape, lambda b, i: (0, 0, 0, 0), pipeline_mode=once)],
        out_specs=pl.BlockSpec((1, BLK, C_HEADS * HEAD_DIM), qblk),
        scratch_shapes=[pltpu.VMEM((S // LANES, BLK, LANES), I32),
                        pltpu.VMEM((C_HEADS, BLK, LANES), BF16),
                        pltpu.VMEM((IDX_HEADS, BLK, LANES), BF16),
                        pltpu.VMEM((IDX_HEADS, BLK, LANES), F32),
                        pltpu.VMEM((DSA_GK // LANES, BLK, LANES), F32),
                        pltpu.VMEM((2, DSA_GK // LANES, BLK, LANES), F32),
                        pltpu.VMEM((C_HEADS, BLK, LANES), F32),
                        pltpu.VMEM((C_HEADS, BLK, LANES), F32)],
        out_shape=jax.ShapeDtypeStruct((B, S, C_HEADS * HEAD_DIM), BF16),
        compiler_params=_cparams(("parallel", "arbitrary")),
        name="dsa_attention",
    )(q, qi, wi, chunked_t(ki), chunked_t(kc), va, vb, strip)


def _ffn_step(x2d, w1_ref, w3_ref, w2_ref, acc):
    xb = x2d[...]
    h1 = jnp.dot(xb, w1_ref[0], preferred_element_type=F32)
    h3 = jnp.dot(xb, w3_ref[0], preferred_element_type=F32)
    a = (h1 * jax.nn.sigmoid(h1) * h3).astype(BF16)
    acc[...] += jnp.dot(a, w2_ref[0], preferred_element_type=F32)


def _ffn_dense_kernel(x_ref, sc_ref, sh_ref, gate_ref, g_ref, b_ref, w1_ref, w3_ref, w2_ref, out_ref,
                      x2d, acc, *, nf):
    f = pl.program_id(2)

    @pl.when(f == 0)
    def _():
        x2d[...] = (x_ref[0] * sc_ref[0] + sh_ref[0]).astype(BF16)
        acc[...] = jnp.zeros(acc.shape, F32)

    _ffn_step(x2d, w1_ref, w3_ref, w2_ref, acc)

    @pl.when(f == nf - 1)
    def _():
        z = ALPHA * x_ref[0] + gate_ref[0] * acc[...]
        out_ref[0] = _layer_norm(z, g_ref[...], b_ref[...])


def ffn_dense_ln(x, scale, shift, gate, ln_g, ln_b, w1, w3, w2, tm=512, tf=512):
    B, S, D = x.shape
    nf = D_FF // tf
    xi = lambda b, i, f: (b, i, 0)
    bi = lambda b, i, f: (b, 0, 0)
    c2 = lambda b, i, f: (0, 0)
    return pl.pallas_call(
        functools.partial(_ffn_dense_kernel, nf=nf),
        grid=(B, S // tm, nf),
        in_specs=[pl.BlockSpec((1, tm, D), xi),
                  pl.BlockSpec((1, 1, D), bi), pl.BlockSpec((1, 1, D), bi), pl.BlockSpec((1, 1, D), bi),
                  pl.BlockSpec((1, D), c2), pl.BlockSpec((1, D), c2),
                  pl.BlockSpec((1, D, tf), lambda b, i, f: (0, 0, f)),
                  pl.BlockSpec((1, D, tf), lambda b, i, f: (0, 0, f)),
                  pl.BlockSpec((1, tf, D), lambda b, i, f: (0, f, 0))],
        out_specs=pl.BlockSpec((1, tm, D), xi),
        out_shape=jax.ShapeDtypeStruct((B, S, D), F32),
        scratch_shapes=[pltpu.VMEM((tm, D), BF16), pltpu.VMEM((tm, D), F32)],
        compiler_params=_cparams(("parallel", "parallel", "arbitrary")),
        name="ffn_dense_ln",
    )(x, scale, shift, gate, ln_g.reshape(1, D), ln_b.reshape(1, D), w1, w3, w2)


def _ffn_expert_kernel(be_ref, tok_ref, tok_next_ref, dst_ref, h_hbm, w1_ref, w3_ref, w2_ref, y_hbm,
                       xbuf, ybuf, x2d, acc, gsem, ssem, *, nf, tm):
    i = pl.program_id(0)
    f = pl.program_id(1)
    n_blocks = pl.num_programs(0)
    slot = i % 2
    rows = tm * D_TILES

    def start_gather(idx_ref, s):
        def body(r, carry):
            src = pl.multiple_of(idx_ref[0, 0, r] * D_TILES, D_TILES)
            dst = pl.multiple_of(r * D_TILES, D_TILES)
            pltpu.make_async_copy(h_hbm.at[pl.ds(src, D_TILES)], xbuf.at[s, pl.ds(dst, D_TILES)],
                                  gsem.at[s]).start()
            return carry
        lax.fori_loop(0, tm, body, 0, unroll=8)

    def wait_gather(s):
        pltpu.make_async_copy(h_hbm.at[pl.ds(0, rows)], xbuf.at[s], gsem.at[s]).wait()

    def start_scatter():
        def body(r, carry):
            src = pl.multiple_of(r * D_TILES, D_TILES)
            dst = pl.multiple_of(dst_ref[0, 0, r] * D_TILES, D_TILES)
            pltpu.make_async_copy(ybuf.at[pl.ds(src, D_TILES)], y_hbm.at[pl.ds(dst, D_TILES)], ssem).start()
            return carry
        lax.fori_loop(0, tm, body, 0, unroll=8)

    def wait_scatter():
        pltpu.make_async_copy(ybuf, y_hbm.at[pl.ds(0, rows)], ssem).wait()

    @pl.when(f == 0)
    def _():
        @pl.when(i == 0)
        def _():
            start_gather(tok_ref, 0)

        @pl.when(i + 1 < n_blocks)
        def _():
            start_gather(tok_next_ref, 1 - slot)

        wait_gather(slot)
        for j in range(D_TILES):
            x2d[:, j * LANES:(j + 1) * LANES] = xbuf[slot, pl.ds(j, tm, stride=D_TILES), :].astype(BF16)
        acc[...] = jnp.zeros(acc.shape, F32)

    _ffn_step(x2d, w1_ref, w3_ref, w2_ref, acc)

    @pl.when(f == nf - 1)
    def _():
        @pl.when(i > 0)
        def _():
            wait_scatter()

        for j in range(D_TILES):
            ybuf[pl.ds(j, tm, stride=D_TILES), :] = acc[:, j * LANES:(j + 1) * LANES]
        start_scatter()

        @pl.when(i == n_blocks - 1)
        def _():
            wait_scatter()


def ffn_experts(h_tiles, slot_tok, slot_dst, block_expert, w1, w3, w2, tf=512):
    tm = MOE_BLOCK
    P = slot_tok.shape[0]
    n_blocks = P // tm
    nf = D_FF // tf
    tok3 = slot_tok.reshape(n_blocks, 1, tm)
    dst3 = slot_dst.reshape(n_blocks, 1, tm)
    cur = lambda i, f, be: (i, 0, 0)
    nxt = lambda i, f, be: (jnp.minimum(i + 1, n_blocks - 1), 0, 0)
    return pl.pallas_call(
        functools.partial(_ffn_expert_kernel, nf=nf, tm=tm),
        grid_spec=pltpu.PrefetchScalarGridSpec(
            num_scalar_prefetch=1,
            grid=(n_blocks, nf),
            in_specs=[pl.BlockSpec((1, 1, tm), cur, memory_space=pltpu.SMEM),
                      pl.BlockSpec((1, 1, tm), nxt, memory_space=pltpu.SMEM),
                      pl.BlockSpec((1, 1, tm), cur, memory_space=pltpu.SMEM),
                      pl.BlockSpec(memory_space=pl.ANY),
                      pl.BlockSpec((1, D_MODEL, tf), lambda i, f, be: (be[i], 0, f)),
                      pl.BlockSpec((1, D_MODEL, tf), lambda i, f, be: (be[i], 0, f)),
                      pl.BlockSpec((1, tf, D_MODEL), lambda i, f, be: (be[i], f, 0))],
            out_specs=pl.BlockSpec(memory_space=pl.ANY),
            scratch_shapes=[pltpu.VMEM((2, tm * D_TILES, LANES), F32),
                            pltpu.VMEM((tm * D_TILES, LANES), F32),
                            pltpu.VMEM((tm, D_MODEL), BF16),
                            pltpu.VMEM((tm, D_MODEL), F32),
                            pltpu.SemaphoreType.DMA((2,)),
                            pltpu.SemaphoreType.DMA(())]),
        out_shape=jax.ShapeDtypeStruct((P * D_TILES, LANES), F32),
        compiler_params=_cparams(("arbitrary", "arbitrary")),
        name="ffn_experts",
    )(block_expert, tok3, tok3, dst3, h_tiles, w1, w3, w2)


def _router_kernel(x_ref, sc_ref, sh_ref, wr_ref, h_ref, idx_ref, gate_ref, *, tm):
    h = x_ref[0] * sc_ref[0] + sh_ref[0]
    for j in range(D_TILES):
        h_ref[pl.ds(j, tm, stride=D_TILES), :] = h[:, j * LANES:(j + 1) * LANES]
    logits = jnp.dot(h, wr_ref[...], preferred_element_type=F32, precision=lax.Precision.HIGHEST)
    lane = lax.broadcasted_iota(I32, logits.shape, 1)
    lg = jnp.where(lane < N_EXPERTS, logits, -jnp.inf)
    m1 = jnp.max(lg, axis=1, keepdims=True)
    i1 = jnp.min(jnp.where(lg == m1, lane, LANES), axis=1, keepdims=True)
    lg2 = jnp.where(lane == i1, -jnp.inf, lg)
    m2 = jnp.max(lg2, axis=1, keepdims=True)
    i2 = jnp.min(jnp.where(lg2 == m2, lane, LANES), axis=1, keepdims=True)
    e = jnp.exp(m2 - m1)
    idx_ref[:, 0:1] = i1
    idx_ref[:, 1:2] = i2
    gate_ref[:, 0:1] = 1.0 / (1.0 + e)
    gate_ref[:, 1:2] = e / (1.0 + e)


def route_tokens(x, scale, shift, w_router, tm=512):
    B, S, D = x.shape
    T = B * S
    nb = S // tm
    wr = jnp.pad(w_router, ((0, 0), (0, LANES - N_EXPERTS)))
    return pl.pallas_call(
        functools.partial(_router_kernel, tm=tm),
        grid=(B, nb),
        in_specs=[pl.BlockSpec((1, tm, D), lambda b, i: (b, i, 0)),
                  pl.BlockSpec((1, 1, D), lambda b, i: (b, 0, 0)),
                  pl.BlockSpec((1, 1, D), lambda b, i: (b, 0, 0)),
                  pl.BlockSpec((D, LANES), lambda b, i: (0, 0))],
        out_specs=[pl.BlockSpec((tm * D_TILES, LANES), lambda b, i: (b * nb + i, 0)),
                   pl.BlockSpec((tm, TOP_K), lambda b, i: (b * nb + i, 0)),
                   pl.BlockSpec((tm, TOP_K), lambda b, i: (b * nb + i, 0))],
        out_shape=[jax.ShapeDtypeStruct((T * D_TILES, LANES), F32),
                   jax.ShapeDtypeStruct((T, TOP_K), I32),
                   jax.ShapeDtypeStruct((T, TOP_K), F32)],
        compiler_params=_cparams(("parallel", "parallel")),
        name="route_tokens",
    )(x, scale, shift, wr)


def _moe_combine_kernel(y_ref, gt_ref, x_ref, gate_ref, g_ref, b_ref, out_ref, *, tm):
    g0 = gt_ref[:, 0:1]
    g1 = gt_ref[:, 1:2]
    stride = TOP_K * D_TILES
    pieces = []
    for j in range(D_TILES):
        sl = slice(j * LANES, (j + 1) * LANES)
        ff = g0 * y_ref[pl.ds(j, tm, stride=stride), :] + g1 * y_ref[pl.ds(D_TILES + j, tm, stride=stride), :]
        pieces.append(ALPHA * x_ref[0, :, sl] + gate_ref[0, :, sl] * ff)
    z = jnp.concatenate(pieces, axis=1)
    out_ref[0] = _layer_norm(z, g_ref[...], b_ref[...])


def moe_combine_ln(yg, gates, x, gate, ln_g, ln_b, tm=512):
    B, S, D = x.shape
    nb = S // tm
    return pl.pallas_call(
        functools.partial(_moe_combine_kernel, tm=tm),
        grid=(B, nb),
        in_specs=[pl.BlockSpec((tm * TOP_K * D_TILES, LANES), lambda b, i: (b * nb + i, 0)),
                  pl.BlockSpec((tm, TOP_K), lambda b, i: (b * nb + i, 0)),
                  pl.BlockSpec((1, tm, D), lambda b, i: (b, i, 0)),
                  pl.BlockSpec((1, 1, D), lambda b, i: (b, 0, 0)),
                  pl.BlockSpec((1, D), lambda b, i: (0, 0)),
                  pl.BlockSpec((1, D), lambda b, i: (0, 0))],
        out_specs=pl.BlockSpec((1, tm, D), lambda b, i: (b, i, 0)),
        out_shape=jax.ShapeDtypeStruct((B, S, D), F32),
        compiler_params=_cparams(("parallel", "parallel")),
        name="moe_combine_ln",
    )(yg, gates, x, gate, ln_g.reshape(1, D), ln_b.reshape(1, D))


def _scale_cols(w, start, width, factor):
    return w.at[:, start:start + width].multiply(factor)


def even_mixer_layer(x, mod, w_in, w_out, rel_table, sinks, ln_g, ln_b):
    B, S, D = x.shape
    assert S % A_PAD == 0
    shift, scale, gate = mod
    ah, bh, bk = A_HEADS * HEAD_DIM, B_HEADS * HEAD_DIM, B_KV_HEADS * HEAD_DIM
    w = _scale_cols(w_in, 0, ah, HEAD_DIM ** -0.5)
    w = _scale_cols(w, 3 * ah, bh, HEAD_DIM ** -0.5).astype(BF16)
    groups = [(ah, BF16)] * 3 + [(bh, BF16), (bk, BF16), (bk, BF16)]
    qa, ka, va, qb, kb, vb = in_projection(x, scale, shift, w, groups)

    outs, lses = [], []
    for window, d in A_PATTERNS:
        ld = S // d

        def regroup(t):
            return jnp.transpose(t.reshape(B, ld, d, ah), (0, 2, 1, 3)).reshape(B * d, ld, ah)

        bias = band_bias(rel_table[:, :A_HEADS], d, window // d)
        o, lse = banded_attention(regroup(qa), regroup(ka), regroup(va), bias, None, F32)
        outs.append(jnp.transpose(o.reshape(B, d, ld, ah), (0, 2, 1, 3)).reshape(B, S, ah))
        lses.append(jnp.transpose(lse.reshape(B, d, ld, A_HEADS), (0, 2, 1, 3)).reshape(B, S, A_HEADS))
    oa = combine_patterns(outs, lses)

    bias_b = band_bias(rel_table[:, A_HEADS:A_HEADS + B_HEADS], 1, B_WINDOW - 1)
    ob, _ = banded_attention(qb, kb, vb, bias_b, sinks.astype(F32), BF16)

    wo = w_out.astype(BF16)
    return out_projection_ln([oa, ob], [wo[:ah], wo[ah:]], x, gate, ln_g, ln_b)


def dsa_mixer_layer(x, mod, w_in, w_out, rel_table, ln_g, ln_b):
    shift, scale, gate = mod
    qw = C_HEADS * HEAD_DIM
    iw = IDX_HEADS * IDX_DIM
    q_w, kc_w, vc_w, qi_w, ki_w, wi_w = jnp.split(
        w_in, [qw, qw + HEAD_DIM, qw + 2 * HEAD_DIM, qw + 2 * HEAD_DIM + iw, qw + 2 * HEAD_DIM + iw + IDX_DIM],
        axis=1)
    pad = jnp.zeros((D_MODEL, LANES - IDX_DIM - IDX_HEADS), w_in.dtype)
    w = jnp.concatenate([q_w * HEAD_DIM ** -0.5, kc_w, vc_w, qi_w, ki_w, wi_w, pad], axis=1).astype(BF16)
    groups = [(qw, BF16), (2 * HEAD_DIM, BF16), (iw, BF16), (LANES, F32)]
    q, kv, qi, kw = in_projection(x, scale, shift, w, groups)
    kc, vc = kv[..., :HEAD_DIM], kv[..., HEAD_DIM:]
    ki = kw[..., :IDX_DIM].astype(BF16)
    wi = kw[..., IDX_DIM:IDX_DIM + IDX_HEADS]
    o = dsa_attention(q, qi, wi, ki, kc, vc, rel_table[:, :C_HEADS])
    return out_projection_ln([o], [w_out.astype(BF16)], x, gate, ln_g, ln_b)


def moe_layer(x, mod, w_router, w1, w3, w2, ln_g, ln_b):
    B, S, D = x.shape
    shift, scale, gate = mod
    T = B * S
    A = T * TOP_K
    h_tiles, top_idx, gates = route_tokens(x, scale, shift, w_router)

    e_flat = top_idx.reshape(-1)
    onehot = (e_flat[:, None] == jnp.arange(N_EXPERTS)[None, :]).astype(I32)
    rank = jnp.take_along_axis(jnp.cumsum(onehot, axis=0) - onehot, e_flat[:, None], axis=1)[:, 0]
    counts = jnp.sum(onehot, axis=0)
    padded = (counts + MOE_BLOCK - 1) // MOE_BLOCK * MOE_BLOCK
    pends = jnp.cumsum(padded)
    dest = (pends - padded)[e_flat] + rank
    n_blocks = -(-A // MOE_BLOCK) + N_EXPERTS
    P = n_blocks * MOE_BLOCK
    slot_src = jnp.full((P,), -1, I32).at[dest].set(jnp.arange(A, dtype=I32))
    is_pad = slot_src < 0
    slot_tok = jnp.where(is_pad, 0, slot_src // TOP_K)
    slot_dst = jnp.where(is_pad, A - 1 + jnp.cumsum(is_pad.astype(I32)), slot_src)
    block_expert = jnp.minimum(
        jnp.searchsorted(pends, jnp.arange(n_blocks) * MOE_BLOCK, side='right'), N_EXPERTS - 1).astype(I32)

    yg = ffn_experts(h_tiles, slot_tok, slot_dst, block_expert,
                     w1.astype(BF16), w3.astype(BF16), w2.astype(BF16))
    return moe_combine_ln(yg, gates, x, gate, ln_g, ln_b)


def kernel(x, c, rel_table, w_in_even, w_out_even, sinks, w_in_odd, w_out_odd, ffn_w1, ffn_w3, ffn_w2,
           router, exp_w1, exp_w3, exp_w2, ada_w, ada_b, ln_g, ln_b):
    D = D_MODEL
    mods = ada_modulation_all(c, ada_w, ada_b)

    def mod(layer, sub):
        m = mods[2 * layer + sub]
        return m[:, None, :D], m[:, None, D:2 * D], m[:, None, 2 * D:]

    for layer in range(DEPTH):
        i = layer // 2
        if layer % 2 == 0:
            x = even_mixer_layer(x, mod(layer, 0), w_in_even[i], w_out_even[i], rel_table, sinks[i],
                                 ln_g[layer, 0], ln_b[layer, 0])
            shift, scale, gate = mod(layer, 1)
            x = ffn_dense_ln(x, scale, shift, gate, ln_g[layer, 1], ln_b[layer, 1],
                             ffn_w1[i][None].astype(BF16), ffn_w3[i][None].astype(BF16),
                             ffn_w2[i][None].astype(BF16))
        else:
            x = dsa_mixer_layer(x, mod(layer, 0), w_in_odd[i], w_out_odd[i], rel_table,
                                ln_g[layer, 0], ln_b[layer, 0])
            x = moe_layer(x, mod(layer, 1), router[i], exp_w1[i], exp_w3[i], exp_w2[i],
                          ln_g[layer, 1], ln_b[layer, 1])
    return x
```

```python
import functools
import math

import numpy as np
import jax
import jax.numpy as jnp
from jax import lax
from jax.experimental import pallas as pl
from jax.experimental.pallas import tpu as pltpu

F32 = jnp.float32
BF16 = jnp.bfloat16
I32 = jnp.int32

D_MODEL = 1024
HEAD_DIM = 64
BLK = 128
A_HEADS = 8
A_PATTERNS = ((128, 1), (512, 4), (2048, 16))
A_PAD = BLK * 16
B_HEADS = 8
B_KV_HEADS = 2
B_WINDOW = 128
C_HEADS = 16
IDX_HEADS = 8
IDX_DIM = 64
TOPK_MAX = 256
REL_BUCKETS = 32
REL_MAX_DIST = 2048
D_FF = 3584
N_EXPERTS = 8
TOP_K = 2
MOE_BLOCK = 512
DEPTH = 4
ALPHA = (2 * DEPTH) ** 0.25
LN_EPS = 1e-5

LANES = 128
SUBLANES = 8
D_TILES = D_MODEL // LANES
NEG = -(2.0 ** 100)
INT_MIN = -(2 ** 31)
VMEM_LIMIT = 56 * 1024 * 1024

DSA_CK = 256
DSA_GK = 512
DSA_SKEW = 5


def _bucket_np(dist):
    n = np.maximum(dist, 0)
    max_exact = REL_BUCKETS // 2
    nf = np.maximum(n, 1).astype(np.float32)
    large = max_exact + (np.log(nf / np.float32(max_exact)) / np.float32(math.log(REL_MAX_DIST / max_exact))
                         * np.float32(REL_BUCKETS - max_exact)).astype(np.int32)
    large = np.minimum(large, REL_BUCKETS - 1)
    return np.where(n < max_exact, n, large).astype(np.int32)


def _far_distance():
    b = _bucket_np(np.arange(0, 2 * REL_MAX_DIST))
    return int(np.max(np.nonzero(b != REL_BUCKETS - 1)[0])) + 1


FAR_DIST = _far_distance()
N_NEAR = -(-(FAR_DIST + BLK - 1) // BLK)


def _cparams(sem):
    return pltpu.CompilerParams(dimension_semantics=sem, vmem_limit_bytes=VMEM_LIMIT)


def _layer_norm(z, g, b):
    mu = jnp.mean(z, axis=-1, keepdims=True)
    zc = z - mu
    var = jnp.mean(zc * zc, axis=-1, keepdims=True)
    return zc * lax.rsqrt(var + LN_EPS) * g + b


def _ada_kernel(c_ref, w_ref, b_ref, o_ref):
    j = pl.program_id(1)
    c = c_ref[...]
    sc = c * jax.nn.sigmoid(c)
    mod = jnp.dot(sc, w_ref[0], preferred_element_type=F32, precision=lax.Precision.HIGHEST)
    o_ref[0] = mod + b_ref[0] + jnp.where(j >= 1, 1.0, 0.0)


def ada_modulation_all(c, ada_w, ada_b):
    B, D = c.shape
    n = ada_w.shape[0] * ada_w.shape[1]
    rows = -(-B // SUBLANES) * SUBLANES
    cp = jnp.pad(c, ((0, rows - B), (0, 0)))
    w = ada_w.reshape(n, D, 3 * D)
    b = ada_b.reshape(n, 1, 3 * D)
    out = pl.pallas_call(
        _ada_kernel,
        grid=(n, 3),
        in_specs=[pl.BlockSpec((rows, D), lambda l, j: (0, 0)),
                  pl.BlockSpec((1, D, D), lambda l, j: (l, 0, j)),
                  pl.BlockSpec((1, 1, D), lambda l, j: (l, 0, j))],
        out_specs=pl.BlockSpec((1, rows, D), lambda l, j: (l, 0, j)),
        out_shape=jax.ShapeDtypeStruct((n, rows, 3 * D), F32),
        compiler_params=_cparams(("arbitrary", "arbitrary")),
        name="ada_modulation",
    )(cp, w, b)
    return out[:, :B]


def _inproj_kernel(x_ref, sc_ref, sh_ref, w_ref, *o_refs, splits):
    h = (x_ref[0] * sc_ref[0] + sh_ref[0]).astype(BF16)
    for o_ref, (start, width) in zip(o_refs, splits):
        o_ref[0] = jnp.dot(h, w_ref[:, start:start + width],
                           preferred_element_type=F32).astype(o_ref.dtype)


def in_projection(x, scale, shift, w, groups, tm=512):
    B, S, D = x.shape
    splits, start = [], 0
    for width, _ in groups:
        splits.append((start, width))
        start += width
    assert start == w.shape[1] and S % tm == 0
    return pl.pallas_call(
        functools.partial(_inproj_kernel, splits=tuple(splits)),
        grid=(B, S // tm),
        in_specs=[pl.BlockSpec((1, tm, D), lambda b, i: (b, i, 0)),
                  pl.BlockSpec((1, 1, D), lambda b, i: (b, 0, 0)),
                  pl.BlockSpec((1, 1, D), lambda b, i: (b, 0, 0)),
                  pl.BlockSpec(w.shape, lambda b, i: (0, 0))],
        out_specs=[pl.BlockSpec((1, tm, width), lambda b, i: (b, i, 0)) for width, _ in groups],
        out_shape=[jax.ShapeDtypeStruct((B, S, width), dt) for width, dt in groups],
        compiler_params=_cparams(("parallel", "parallel")),
        name="in_projection",
    )(x, scale, shift, w)


def _band_kernel(sink_ref, q_ref, kp_ref, ko_ref, vp_ref, vo_ref, bias_ref, o_ref, lse_ref, *,
                 hq, hk, use_sinks):
    b = pl.program_id(1)
    q = q_ref[0]
    kk = jnp.concatenate([kp_ref[0], ko_ref[0]], axis=0)
    vv = jnp.concatenate([vp_ref[0], vo_ref[0]], axis=0)
    col = lax.broadcasted_iota(I32, (BLK, 2 * BLK), 1)
    first_mask = jnp.where(jnp.logical_and(b == 0, col < BLK), NEG, 0.0)
    group = hq // hk
    for h in range(hq):
        g = h // group
        qh = q[:, h * HEAD_DIM:(h + 1) * HEAD_DIM]
        kh = kk[:, g * HEAD_DIM:(g + 1) * HEAD_DIM]
        vh = vv[:, g * HEAD_DIM:(g + 1) * HEAD_DIM]
        s = lax.dot_general(qh, kh, (((1,), (1,)), ((), ())), preferred_element_type=F32)
        s = s + bias_ref[h] + first_mask
        m = jnp.max(s, axis=-1, keepdims=True)
        if use_sinks:
            m = jnp.maximum(m, sink_ref[h])
        p = jnp.exp(s - m)
        l = jnp.sum(p, axis=-1, keepdims=True)
        if use_sinks:
            l = l + jnp.exp(sink_ref[h] - m)
        o = jnp.dot(p.astype(BF16), vh, preferred_element_type=F32)
        o_ref[0, :, h * HEAD_DIM:(h + 1) * HEAD_DIM] = (o / l).astype(o_ref.dtype)
        lse_ref[0, :, h:h + 1] = m + jnp.log(l)


def banded_attention(q, k, v, bias, sinks, out_dtype):
    N, L, qc = q.shape
    hq = qc // HEAD_DIM
    hk = k.shape[2] // HEAD_DIM
    use_sinks = sinks is not None
    if sinks is None:
        sinks = jnp.zeros((hq,), F32)
    cur = lambda n, b, s: (n, b, 0)
    prev = lambda n, b, s: (n, jnp.maximum(b - 1, 0), 0)
    return pl.pallas_call(
        functools.partial(_band_kernel, hq=hq, hk=hk, use_sinks=use_sinks),
        grid_spec=pltpu.PrefetchScalarGridSpec(
            num_scalar_prefetch=1,
            grid=(N, L // BLK),
            in_specs=[pl.BlockSpec((1, BLK, qc), cur),
                      pl.BlockSpec((1, BLK, hk * HEAD_DIM), prev),
                      pl.BlockSpec((1, BLK, hk * HEAD_DIM), cur),
                      pl.BlockSpec((1, BLK, hk * HEAD_DIM), prev),
                      pl.BlockSpec((1, BLK, hk * HEAD_DIM), cur),
                      pl.BlockSpec(bias.shape, lambda n, b, s: (0, 0, 0))],
            out_specs=[pl.BlockSpec((1, BLK, qc), cur),
                       pl.BlockSpec((1, BLK, hq), cur)]),
        out_shape=[jax.ShapeDtypeStruct((N, L, qc), out_dtype),
                   jax.ShapeDtypeStruct((N, L, hq), F32)],
        compiler_params=_cparams(("parallel", "arbitrary")),
        name="banded_attention",
    )(sinks, q, k, k, v, v, bias)


def band_bias(table, dilation, max_dist):
    qi = np.arange(BLK)[:, None]
    kj = np.arange(2 * BLK)[None, :]
    dist = qi + BLK - kj
    allowed = (dist >= 0) & (dist <= max_dist)
    bias = table[_bucket_np(dist * dilation)]
    bias = jnp.where(allowed[:, :, None], bias, NEG)
    return jnp.transpose(bias, (2, 0, 1)).astype(F32)


def _combine_kernel(*refs, n_pat, heads):
    o_refs, l_refs, out_ref = refs[:n_pat], refs[n_pat:2 * n_pat], refs[2 * n_pat]
    lses = [r[0] for r in l_refs]
    m = functools.reduce(jnp.maximum, lses)
    es = [jnp.exp(l - m) for l in lses]
    tot = functools.reduce(lambda a, b: a + b, es)
    ws = [e / tot for e in es]
    for h in range(heads):
        sl = slice(h * HEAD_DIM, (h + 1) * HEAD_DIM)
        acc = ws[0][:, h:h + 1] * o_refs[0][0, :, sl]
        for p in range(1, n_pat):
            acc = acc + ws[p][:, h:h + 1] * o_refs[p][0, :, sl]
        out_ref[0, :, sl] = acc.astype(out_ref.dtype)


def combine_patterns(outs, lses, tm=512):
    B, S, C = outs[0].shape
    heads = C // HEAD_DIM
    n_pat = len(outs)
    idx = lambda b, i: (b, i, 0)
    return pl.pallas_call(
        functools.partial(_combine_kernel, n_pat=n_pat, heads=heads),
        grid=(B, S // tm),
        in_specs=[pl.BlockSpec((1, tm, C), idx)] * n_pat + [pl.BlockSpec((1, tm, heads), idx)] * n_pat,
        out_specs=pl.BlockSpec((1, tm, C), idx),
        out_shape=jax.ShapeDtypeStruct((B, S, C), BF16),
        compiler_params=_cparams(("parallel", "parallel")),
        name="combine_patterns",
    )(*outs, *lses)


def _outproj_kernel(*refs, n_parts):
    o_refs, w_refs = refs[:n_parts], refs[n_parts:2 * n_parts]
    x_ref, gate_ref, g_ref, b_ref, out_ref = refs[2 * n_parts:]
    mix = jnp.dot(o_refs[0][0], w_refs[0][...], preferred_element_type=F32)
    for o_ref, w_ref in zip(o_refs[1:], w_refs[1:]):
        mix = mix + jnp.dot(o_ref[0], w_ref[...], preferred_element_type=F32)
    z = ALPHA * x_ref[0] + gate_ref[0] * mix
    out_ref[0] = _layer_norm(z, g_ref[...], b_ref[...])


def out_projection_ln(parts, weights, x, gate, ln_g, ln_b, tm=512):
    B, S, D = x.shape
    idx = lambda b, i: (b, i, 0)
    const2 = lambda b, i: (0, 0)
    return pl.pallas_call(
        functools.partial(_outproj_kernel, n_parts=len(parts)),
        grid=(B, S // tm),
        in_specs=([pl.BlockSpec((1, tm, p.shape[2]), idx) for p in parts]
                  + [pl.BlockSpec(w.shape, const2) for w in weights]
                  + [pl.BlockSpec((1, tm, D), idx),
                     pl.BlockSpec((1, 1, D), lambda b, i: (b, 0, 0)),
                     pl.BlockSpec((1, D), const2),
                     pl.BlockSpec((1, D), const2)]),
        out_specs=pl.BlockSpec((1, tm, D), idx),
        out_shape=jax.ShapeDtypeStruct((B, S, D), F32),
        compiler_params=_cparams(("parallel", "parallel")),
        name="out_projection_ln",
    )(*parts, *weights, x, gate, ln_g.reshape(1, D), ln_b.reshape(1, D))


def _dsa_kernel(q_ref, qi_ref, wi_ref, kit_ref, kct_ref, va_ref, vb_ref, strip_ref, o_ref,
                sk_scr, qm_scr, qim_scr, wib_scr, mb_scr, s_scr, m_scr, acc_scr, *, k_sel):
    i = pl.program_id(1)
    ck = DSA_CK
    sub = ck // LANES
    gt = DSA_GK // LANES
    gc = DSA_GK // ck
    nbuf = DSA_SKEW + 1
    ngr = (i * BLK + BLK + DSA_GK - 1) // DSA_GK
    nck = ngr * gc
    lane = lax.broadcasted_iota(I32, (BLK, LANES), 1)
    upper = lane >= HEAD_DIM

    for p in range(C_HEADS // 2):
        qp = q_ref[0, :, p * LANES:(p + 1) * LANES]
        qm_scr[2 * p] = jnp.where(upper, jnp.zeros_like(qp), qp)
        qm_scr[2 * p + 1] = jnp.where(upper, qp, jnp.zeros_like(qp))
    for p in range(IDX_HEADS // 2):
        qp = qi_ref[0, :, p * LANES:(p + 1) * LANES]
        qim_scr[2 * p] = jnp.where(upper, jnp.zeros_like(qp), qp)
        qim_scr[2 * p + 1] = jnp.where(upper, qp, jnp.zeros_like(qp))
    wscale = IDX_HEADS ** -0.5 * IDX_DIM ** -0.5
    for h in range(IDX_HEADS):
        wib_scr[h] = jnp.broadcast_to(wi_ref[0, :, h:h + 1] * wscale, (BLK, LANES))

    row_t = lax.broadcasted_iota(I32, (BLK, ck), 0) + i * BLK
    col_l = lax.broadcasted_iota(I32, (BLK, ck), 1)

    def score_chunk(j, carry):
        kt = kit_ref[0, j]
        sc = jnp.zeros((BLK, ck), F32)
        for h in range(IDX_HEADS):
            d = jnp.dot(qim_scr[h], kt, preferred_element_type=F32)
            w = wib_scr[h]
            sc = sc + jnp.maximum(d, 0.0) * jnp.concatenate([w] * sub, axis=1)
        bits = pltpu.bitcast(sc, I32)
        key = bits ^ ((bits >> 31) & 0x7FFFFFFF)
        key = jnp.where(col_l + j * ck <= row_t, key, INT_MIN)
        for u in range(sub):
            sk_scr[j * sub + u] = key[:, u * LANES:(u + 1) * LANES]
        return carry

    lax.fori_loop(0, nck, score_chunk, 0)

    def bit_pass(t, v_u):
        c_u = v_u | lax.shift_left(jnp.int32(1), 31 - t)
        c_s = c_u ^ INT_MIN

        def count_group(g, acc):
            for u in range(gt):
                acc = acc + jnp.where(sk_scr[g * gt + u] >= c_s, 1, 0)
            return acc

        acc = lax.fori_loop(0, ngr, count_group, jnp.zeros((BLK, LANES), I32))
        cnt = jnp.sum(acc.astype(F32), axis=1, keepdims=True)
        return jnp.where(cnt >= float(k_sel), c_u, v_u)

    v_u = lax.fori_loop(0, 32, bit_pass, jnp.zeros((BLK, LANES), I32))
    thr = jnp.maximum(v_u ^ INT_MIN, INT_MIN + 1)

    m_scr[...] = jnp.full(m_scr.shape, NEG, F32)
    acc_scr[...] = jnp.zeros(acc_scr.shape, F32)

    def attend_group(g, carry):
        for u in range(gt):
            mb_scr[u] = jnp.where(sk_scr[g * gt + u] >= thr, 0.0, NEG).astype(BF16)
        tiles = [jnp.clip(i - (g * gt + u), 0, N_NEAR) for u in range(gt)]

        def logits(h):
            mx = None
            for c in range(gc):
                s = jnp.dot(qm_scr[h], kct_ref[0, g * gc + c], preferred_element_type=F32).astype(BF16)
                for u in range(sub):
                    t = c * sub + u
                    piece = s[:, u * LANES:(u + 1) * LANES] + mb_scr[t] + strip_ref[h, tiles[t]]
                    s_scr[h % nbuf, t] = piece
                    mx = piece if mx is None else jnp.maximum(mx, piece)
            m_old = m_scr[h]
            m_new = jnp.maximum(m_old, jnp.max(mx.astype(F32), axis=1, keepdims=True))
            m_scr[h] = m_new
            return m_old, m_new

        def accumulate(h, m_old, m_new):
            m16 = m_new.astype(BF16)
            p = jnp.concatenate([jnp.exp(s_scr[h % nbuf, t] - m16) for t in range(gt)], axis=1)
            pv = jnp.dot(p, (va_ref if h % 2 == 0 else vb_ref)[0, g], preferred_element_type=F32)
            acc_scr[h] = jnp.exp(m_old - m_new) * acc_scr[h] + pv

        stats = [logits(h) for h in range(DSA_SKEW)]
        for h in range(C_HEADS):
            if h + DSA_SKEW < C_HEADS:
                stats.append(logits(h + DSA_SKEW))
            accumulate(h, *stats[h])
        return carry

    lax.fori_loop(0, ngr, attend_group, 0)

    for p in range(C_HEADS // 2):
        a0 = acc_scr[2 * p]
        a1 = acc_scr[2 * p + 1]
        o = jnp.where(upper, a1 / pltpu.roll(a1, HEAD_DIM, 1), a0 / pltpu.roll(a0, HEAD_DIM, 1))
        o_ref[0, :, p * LANES:(p + 1) * LANES] = o.astype(o_ref.dtype)


def dsa_attention(q, qi, wi, ki, kc, vc, table):
    B, S, _ = q.shape
    ck = DSA_CK
    assert S % DSA_GK == 0
    nchunks = S // ck
    k_sel = min(TOPK_MAX, S // 4)

    def chunked_t(t):
        tt = jnp.transpose(t.reshape(B, nchunks, ck, HEAD_DIM), (0, 1, 3, 2))
        return jnp.concatenate([tt, tt], axis=2)

    ngroups = S // DSA_GK
    ones = jnp.ones_like(vc)
    va = jnp.concatenate([vc, ones], axis=-1).reshape(B, ngroups, DSA_GK, LANES)
    vb = jnp.concatenate([ones, vc], axis=-1).reshape(B, ngroups, DSA_GK, LANES)
    j = np.arange(2 * BLK)
    offs = np.where(j <= BLK, -j, 2 * BLK - j)
    v = table[_bucket_np(np.arange(N_NEAR)[:, None] * BLK + offs[None, :])]
    v = jnp.transpose(v, (2, 0, 1)).astype(F32)
    near = jnp.tile(v, (1, 1, BLK))[..., :BLK * (2 * BLK - 1)]
    near = near.reshape(C_HEADS, N_NEAR, BLK, 2 * BLK - 1)[..., :BLK]
    far = jnp.broadcast_to(table[REL_BUCKETS - 1].astype(F32)[:, None, None, None],
                           (C_HEADS, 1, BLK, LANES))
    strip = jnp.concatenate([near, far], axis=1).astype(BF16)

    once = pl.Buffered(1)
    qblk = lambda b, i: (b, i, 0)
    per_b = lambda b, i: (b, 0, 0, 0)
    return pl.pallas_call(
        functools.partial(_dsa_kernel, k_sel=k_sel),
        grid=(B, S // BLK),
        in_specs=[pl.BlockSpec((1, BLK, C_HEADS * HEAD_DIM), qblk),
                  pl.BlockSpec((1, BLK, IDX_HEADS * IDX_DIM), qblk),
                  pl.BlockSpec((1, BLK, IDX_HEADS), qblk),
                  pl.BlockSpec((1, nchunks, LANES, ck), per_b, pipeline_mode=once),
                  pl.BlockSpec((1, nchunks, LANES, ck), per_b, pipeline_mode=once),
                  pl.BlockSpec((1, ngroups, DSA_GK, LANES), per_b, pipeline_mode=once),
                  pl.BlockSpec((1, ngroups, DSA_GK, LANES), per_b, pipeline_mode=once),
                  pl.BlockSpec(strip.shape, lambda b, i: (0, 0, 0, 0), pipeline_mode=once)],
        out_specs=pl.BlockSpec((1, BLK, C_HEADS * HEAD_DIM), qblk),
        scratch_shapes=[pltpu.VMEM((S // LANES, BLK, LANES), I32),
                        pltpu.VMEM((C_HEADS, BLK, LANES), BF16),
                        pltpu.VMEM((IDX_HEADS, BLK, LANES), BF16),
                        pltpu.VMEM((IDX_HEADS, BLK, LANES), F32),
                        pltpu.VMEM((DSA_GK // LANES, BLK, LANES), BF16),
                        pltpu.VMEM((DSA_SKEW + 1, DSA_GK // LANES, BLK, LANES), BF16),
                        pltpu.VMEM((C_HEADS, BLK, LANES), F32),
                        pltpu.VMEM((C_HEADS, BLK, LANES), F32)],
        out_shape=jax.ShapeDtypeStruct((B, S, C_HEADS * HEAD_DIM), BF16),
        compiler_params=_cparams(("parallel", "arbitrary")),
        name="dsa_attention",
    )(q, qi, wi, chunked_t(ki), chunked_t(kc), va, vb, strip)


def _ffn_step(x2d, w1_ref, w3_ref, w2_ref, acc):
    xb = x2d[...]
    h1 = jnp.dot(xb, w1_ref[0], preferred_element_type=F32)
    h3 = jnp.dot(xb, w3_ref[0], preferred_element_type=F32)
    a = (h1 * jax.nn.sigmoid(h1) * h3).astype(BF16)
    acc[...] += jnp.dot(a, w2_ref[0], preferred_element_type=F32)


def _ffn_dense_kernel(x_ref, sc_ref, sh_ref, gate_ref, g_ref, b_ref, w1_ref, w3_ref, w2_ref, out_ref,
                      x2d, acc, *, nf):
    f = pl.program_id(2)

    @pl.when(f == 0)
    def _():
        x2d[...] = (x_ref[0] * sc_ref[0] + sh_ref[0]).astype(BF16)
        acc[...] = jnp.zeros(acc.shape, F32)

    _ffn_step(x2d, w1_ref, w3_ref, w2_ref, acc)

    @pl.when(f == nf - 1)
    def _():
        z = ALPHA * x_ref[0] + gate_ref[0] * acc[...]
        out_ref[0] = _layer_norm(z, g_ref[...], b_ref[...])


def ffn_dense_ln(x, scale, shift, gate, ln_g, ln_b, w1, w3, w2, tm=512, tf=512):
    B, S, D = x.shape
    nf = D_FF // tf
    xi = lambda b, i, f: (b, i, 0)
    bi = lambda b, i, f: (b, 0, 0)
    c2 = lambda b, i, f: (0, 0)
    return pl.pallas_call(
        functools.partial(_ffn_dense_kernel, nf=nf),
        grid=(B, S // tm, nf),
        in_specs=[pl.BlockSpec((1, tm, D), xi),
                  pl.BlockSpec((1, 1, D), bi), pl.BlockSpec((1, 1, D), bi), pl.BlockSpec((1, 1, D), bi),
                  pl.BlockSpec((1, D), c2), pl.BlockSpec((1, D), c2),
                  pl.BlockSpec((1, D, tf), lambda b, i, f: (0, 0, f)),
                  pl.BlockSpec((1, D, tf), lambda b, i, f: (0, 0, f)),
                  pl.BlockSpec((1, tf, D), lambda b, i, f: (0, f, 0))],
        out_specs=pl.BlockSpec((1, tm, D), xi),
        out_shape=jax.ShapeDtypeStruct((B, S, D), F32),
        scratch_shapes=[pltpu.VMEM((tm, D), BF16), pltpu.VMEM((tm, D), F32)],
        compiler_params=_cparams(("parallel", "parallel", "arbitrary")),
        name="ffn_dense_ln",
    )(x, scale, shift, gate, ln_g.reshape(1, D), ln_b.reshape(1, D), w1, w3, w2)


def _ffn_expert_kernel(be_ref, tok_ref, tok_next_ref, dst_ref, h_hbm, w1_ref, w3_ref, w2_ref, y_hbm,
                       xbuf, ybuf, x2d, acc, gsem, ssem, *, nf, tm):
    i = pl.program_id(0)
    f = pl.program_id(1)
    n_blocks = pl.num_programs(0)
    slot = i % 2
    rows = tm * D_TILES

    def start_gather(idx_ref, s):
        def body(r, carry):
            src = pl.multiple_of(idx_ref[0, 0, r] * D_TILES, D_TILES)
            dst = pl.multiple_of(r * D_TILES, D_TILES)
            pltpu.make_async_copy(h_hbm.at[pl.ds(src, D_TILES)], xbuf.at[s, pl.ds(dst, D_TILES)],
                                  gsem.at[s]).start()
            return carry
        lax.fori_loop(0, tm, body, 0, unroll=8)

    def wait_gather(s):
        pltpu.make_async_copy(h_hbm.at[pl.ds(0, rows)], xbuf.at[s], gsem.at[s]).wait()

    def start_scatter():
        def body(r, carry):
            src = pl.multiple_of(r * D_TILES, D_TILES)
            dst = pl.multiple_of(dst_ref[0, 0, r] * D_TILES, D_TILES)
            pltpu.make_async_copy(ybuf.at[pl.ds(src, D_TILES)], y_hbm.at[pl.ds(dst, D_TILES)], ssem).start()
            return carry
        lax.fori_loop(0, tm, body, 0, unroll=8)

    def wait_scatter():
        pltpu.make_async_copy(ybuf, y_hbm.at[pl.ds(0, rows)], ssem).wait()

    @pl.when(f == 0)
    def _():
        @pl.when(i == 0)
        def _():
            start_gather(tok_ref, 0)

        @pl.when(i + 1 < n_blocks)
        def _():
            start_gather(tok_next_ref, 1 - slot)

        wait_gather(slot)
        for j in range(D_TILES):
            x2d[:, j * LANES:(j + 1) * LANES] = xbuf[slot, pl.ds(j, tm, stride=D_TILES), :].astype(BF16)
        acc[...] = jnp.zeros(acc.shape, F32)

    _ffn_step(x2d, w1_ref, w3_ref, w2_ref, acc)

    @pl.when(f == nf - 1)
    def _():
        @pl.when(i > 0)
        def _():
            wait_scatter()

        for j in range(D_TILES):
            ybuf[pl.ds(j, tm, stride=D_TILES), :] = acc[:, j * LANES:(j + 1) * LANES]
        start_scatter()

        @pl.when(i == n_blocks - 1)
        def _():
            wait_scatter()


def ffn_experts(h_tiles, slot_tok, slot_dst, block_expert, w1, w3, w2, tf=512):
    tm = MOE_BLOCK
    P = slot_tok.shape[0]
    n_blocks = P // tm
    nf = D_FF // tf
    tok3 = slot_tok.reshape(n_blocks, 1, tm)
    dst3 = slot_dst.reshape(n_blocks, 1, tm)
    cur = lambda i, f, be: (i, 0, 0)
    nxt = lambda i, f, be: (jnp.minimum(i + 1, n_blocks - 1), 0, 0)
    return pl.pallas_call(
        functools.partial(_ffn_expert_kernel, nf=nf, tm=tm),
        grid_spec=pltpu.PrefetchScalarGridSpec(
            num_scalar_prefetch=1,
            grid=(n_blocks, nf),
            in_specs=[pl.BlockSpec((1, 1, tm), cur, memory_space=pltpu.SMEM),
                      pl.BlockSpec((1, 1, tm), nxt, memory_space=pltpu.SMEM),
                      pl.BlockSpec((1, 1, tm), cur, memory_space=pltpu.SMEM),
                      pl.BlockSpec(memory_space=pl.ANY),
                      pl.BlockSpec((1, D_MODEL, tf), lambda i, f, be: (be[i], 0, f)),
                      pl.BlockSpec((1, D_MODEL, tf), lambda i, f, be: (be[i], 0, f)),
                      pl.BlockSpec((1, tf, D_MODEL), lambda i, f, be: (be[i], f, 0))],
            out_specs=pl.BlockSpec(memory_space=pl.ANY),
            scratch_shapes=[pltpu.VMEM((2, tm * D_TILES, LANES), F32),
                            pltpu.VMEM((tm * D_TILES, LANES), F32),
                            pltpu.VMEM((tm, D_MODEL), BF16),
                            pltpu.VMEM((tm, D_MODEL), F32),
                            pltpu.SemaphoreType.DMA((2,)),
                            pltpu.SemaphoreType.DMA(())]),
        out_shape=jax.ShapeDtypeStruct((P * D_TILES, LANES), F32),
        compiler_params=_cparams(("arbitrary", "arbitrary")),
        name="ffn_experts",
    )(block_expert, tok3, tok3, dst3, h_tiles, w1, w3, w2)


def _router_kernel(x_ref, sc_ref, sh_ref, wr_ref, h_ref, idx_ref, gate_ref, *, tm):
    h = x_ref[0] * sc_ref[0] + sh_ref[0]
    for j in range(D_TILES):
        h_ref[pl.ds(j, tm, stride=D_TILES), :] = h[:, j * LANES:(j + 1) * LANES]
    logits = jnp.dot(h, wr_ref[...], preferred_element_type=F32, precision=lax.Precision.HIGHEST)
    lane = lax.broadcasted_iota(I32, logits.shape, 1)
    lg = jnp.where(lane < N_EXPERTS, logits, -jnp.inf)
    m1 = jnp.max(lg, axis=1, keepdims=True)
    i1 = jnp.min(jnp.where(lg == m1, lane, LANES), axis=1, keepdims=True)
    lg2 = jnp.where(lane == i1, -jnp.inf, lg)
    m2 = jnp.max(lg2, axis=1, keepdims=True)
    i2 = jnp.min(jnp.where(lg2 == m2, lane, LANES), axis=1, keepdims=True)
    e = jnp.exp(m2 - m1)
    idx_ref[:, 0:1] = i1
    idx_ref[:, 1:2] = i2
    gate_ref[:, 0:1] = 1.0 / (1.0 + e)
    gate_ref[:, 1:2] = e / (1.0 + e)


def route_tokens(x, scale, shift, w_router, tm=512):
    B, S, D = x.shape
    T = B * S
    nb = S // tm
    wr = jnp.pad(w_router, ((0, 0), (0, LANES - N_EXPERTS)))
    return pl.pallas_call(
        functools.partial(_router_kernel, tm=tm),
        grid=(B, nb),
        in_specs=[pl.BlockSpec((1, tm, D), lambda b, i: (b, i, 0)),
                  pl.BlockSpec((1, 1, D), lambda b, i: (b, 0, 0)),
                  pl.BlockSpec((1, 1, D), lambda b, i: (b, 0, 0)),
                  pl.BlockSpec((D, LANES), lambda b, i: (0, 0))],
        out_specs=[pl.BlockSpec((tm * D_TILES, LANES), lambda b, i: (b * nb + i, 0)),
                   pl.BlockSpec((tm, TOP_K), lambda b, i: (b * nb + i, 0)),
                   pl.BlockSpec((tm, TOP_K), lambda b, i: (b * nb + i, 0))],
        out_shape=[jax.ShapeDtypeStruct((T * D_TILES, LANES), F32),
                   jax.ShapeDtypeStruct((T, TOP_K), I32),
                   jax.ShapeDtypeStruct((T, TOP_K), F32)],
        compiler_params=_cparams(("parallel", "parallel")),
        name="route_tokens",
    )(x, scale, shift, wr)


def _moe_combine_kernel(y_ref, gt_ref, x_ref, gate_ref, g_ref, b_ref, out_ref, *, tm):
    g0 = gt_ref[:, 0:1]
    g1 = gt_ref[:, 1:2]
    stride = TOP_K * D_TILES
    pieces = []
    for j in range(D_TILES):
        sl = slice(j * LANES, (j + 1) * LANES)
        ff = g0 * y_ref[pl.ds(j, tm, stride=stride), :] + g1 * y_ref[pl.ds(D_TILES + j, tm, stride=stride), :]
        pieces.append(ALPHA * x_ref[0, :, sl] + gate_ref[0, :, sl] * ff)
    z = jnp.concatenate(pieces, axis=1)
    out_ref[0] = _layer_norm(z, g_ref[...], b_ref[...])


def moe_combine_ln(yg, gates, x, gate, ln_g, ln_b, tm=512):
    B, S, D = x.shape
    nb = S // tm
    return pl.pallas_call(
        functools.partial(_moe_combine_kernel, tm=tm),
        grid=(B, nb),
        in_specs=[pl.BlockSpec((tm * TOP_K * D_TILES, LANES), lambda b, i: (b * nb + i, 0)),
                  pl.BlockSpec((tm, TOP_K), lambda b, i: (b * nb + i, 0)),
                  pl.BlockSpec((1, tm, D), lambda b, i: (b, i, 0)),
                  pl.BlockSpec((1, 1, D), lambda b, i: (b, 0, 0)),
                  pl.BlockSpec((1, D), lambda b, i: (0, 0)),
                  pl.BlockSpec((1, D), lambda b, i: (0, 0))],
        out_specs=pl.BlockSpec((1, tm, D), lambda b, i: (b, i, 0)),
        out_shape=jax.ShapeDtypeStruct((B, S, D), F32),
        compiler_params=_cparams(("parallel", "parallel")),
        name="moe_combine_ln",
    )(yg, gates, x, gate, ln_g.reshape(1, D), ln_b.reshape(1, D))


def _scale_cols(w, start, width, factor):
    return w.at[:, start:start + width].multiply(factor)


def even_mixer_layer(x, mod, w_in, w_out, rel_table, sinks, ln_g, ln_b):
    B, S, D = x.shape
    assert S % A_PAD == 0
    shift, scale, gate = mod
    ah, bh, bk = A_HEADS * HEAD_DIM, B_HEADS * HEAD_DIM, B_KV_HEADS * HEAD_DIM
    w = _scale_cols(w_in, 0, ah, HEAD_DIM ** -0.5)
    w = _scale_cols(w, 3 * ah, bh, HEAD_DIM ** -0.5).astype(BF16)
    groups = [(ah, BF16)] * 3 + [(bh, BF16), (bk, BF16), (bk, BF16)]
    qa, ka, va, qb, kb, vb = in_projection(x, scale, shift, w, groups)

    outs, lses = [], []
    for window, d in A_PATTERNS:
        ld = S // d

        def regroup(t):
            return jnp.transpose(t.reshape(B, ld, d, ah), (0, 2, 1, 3)).reshape(B * d, ld, ah)

        bias = band_bias(rel_table[:, :A_HEADS], d, window // d)
        o, lse = banded_attention(regroup(qa), regroup(ka), regroup(va), bias, None, F32)
        outs.append(jnp.transpose(o.reshape(B, d, ld, ah), (0, 2, 1, 3)).reshape(B, S, ah))
        lses.append(jnp.transpose(lse.reshape(B, d, ld, A_HEADS), (0, 2, 1, 3)).reshape(B, S, A_HEADS))
    oa = combine_patterns(outs, lses)

    bias_b = band_bias(rel_table[:, A_HEADS:A_HEADS + B_HEADS], 1, B_WINDOW - 1)
    ob, _ = banded_attention(qb, kb, vb, bias_b, sinks.astype(F32), BF16)

    wo = w_out.astype(BF16)
    return out_projection_ln([oa, ob], [wo[:ah], wo[ah:]], x, gate, ln_g, ln_b)


def dsa_mixer_layer(x, mod, w_in, w_out, rel_table, ln_g, ln_b):
    shift, scale, gate = mod
    qw = C_HEADS * HEAD_DIM
    iw = IDX_HEADS * IDX_DIM
    q_w, kc_w, vc_w, qi_w, ki_w, wi_w = jnp.split(
        w_in, [qw, qw + HEAD_DIM, qw + 2 * HEAD_DIM, qw + 2 * HEAD_DIM + iw, qw + 2 * HEAD_DIM + iw + IDX_DIM],
        axis=1)
    pad = jnp.zeros((D_MODEL, LANES - IDX_DIM - IDX_HEADS), w_in.dtype)
    w = jnp.concatenate([q_w * HEAD_DIM ** -0.5, kc_w, vc_w, qi_w, ki_w, wi_w, pad], axis=1).astype(BF16)
    groups = [(qw, BF16), (2 * HEAD_DIM, BF16), (iw, BF16), (LANES, F32)]
    q, kv, qi, kw = in_projection(x, scale, shift, w, groups)
    kc, vc = kv[..., :HEAD_DIM], kv[..., HEAD_DIM:]
    ki = kw[..., :IDX_DIM].astype(BF16)
    wi = kw[..., IDX_DIM:IDX_DIM + IDX_HEADS]
    o = dsa_attention(q, qi, wi, ki, kc, vc, rel_table[:, :C_HEADS])
    return out_projection_ln([o], [w_out.astype(BF16)], x, gate, ln_g, ln_b)


def moe_layer(x, mod, w_router, w1, w3, w2, ln_g, ln_b):
    B, S, D = x.shape
    shift, scale, gate = mod
    T = B * S
    A = T * TOP_K
    h_tiles, top_idx, gates = route_tokens(x, scale, shift, w_router)

    e_flat = top_idx.reshape(-1)
    onehot = (e_flat[:, None] == jnp.arange(N_EXPERTS)[None, :]).astype(I32)
    rank = jnp.take_along_axis(jnp.cumsum(onehot, axis=0) - onehot, e_flat[:, None], axis=1)[:, 0]
    counts = jnp.sum(onehot, axis=0)
    padded = (counts + MOE_BLOCK - 1) // MOE_BLOCK * MOE_BLOCK
    pends = jnp.cumsum(padded)
    dest = (pends - padded)[e_flat] + rank
    n_blocks = -(-A // MOE_BLOCK) + N_EXPERTS
    P = n_blocks * MOE_BLOCK
    slot_src = jnp.full((P,), -1, I32).at[dest].set(jnp.arange(A, dtype=I32))
    is_pad = slot_src < 0
    slot_tok = jnp.where(is_pad, 0, slot_src // TOP_K)
    slot_dst = jnp.where(is_pad, A - 1 + jnp.cumsum(is_pad.astype(I32)), slot_src)
    block_expert = jnp.minimum(
        jnp.searchsorted(pends, jnp.arange(n_blocks) * MOE_BLOCK, side='right'), N_EXPERTS - 1).astype(I32)

    yg = ffn_experts(h_tiles, slot_tok, slot_dst, block_expert,
                     w1.astype(BF16), w3.astype(BF16), w2.astype(BF16))
    return moe_combine_ln(yg, gates, x, gate, ln_g, ln_b)


def kernel(x, c, rel_table, w_in_even, w_out_even, sinks, w_in_odd, w_out_odd, ffn_w1, ffn_w3, ffn_w2,
           router, exp_w1, exp_w3, exp_w2, ada_w, ada_b, ln_g, ln_b):
    D = D_MODEL
    mods = ada_modulation_all(c, ada_w, ada_b)

    def mod(layer, sub):
        m = mods[2 * layer + sub]
        return m[:, None, :D], m[:, None, D:2 * D], m[:, None, 2 * D:]

    for layer in range(DEPTH):
        i = layer // 2
        if layer % 2 == 0:
            x = even_mixer_layer(x, mod(layer, 0), w_in_even[i], w_out_even[i], rel_table, sinks[i],
                                 ln_g[layer, 0], ln_b[layer, 0])
            shift, scale, gate = mod(layer, 1)
            x = ffn_dense_ln(x, scale, shift, gate, ln_g[layer, 1], ln_b[layer, 1],
                             ffn_w1[i][None].astype(BF16), ffn_w3[i][None].astype(BF16),
                             ffn_w2[i][None].astype(BF16))
        else:
            x = dsa_mixer_layer(x, mod(layer, 0), w_in_odd[i], w_out_odd[i], rel_table,
                                ln_g[layer, 0], ln_b[layer, 0])
            x = moe_layer(x, mod(layer, 1), router[i], exp_w1[i], exp_w3[i], exp_w2[i],
                          ln_g[layer, 1], ln_b[layer, 1])
    return x
```

```python
import functools
import math

import numpy as np
import jax
import jax.numpy as jnp
from jax import lax
from jax.experimental import pallas as pl
from jax.experimental.pallas import tpu as pltpu

F32 = jnp.float32
BF16 = jnp.bfloat16
I32 = jnp.int32
I16 = jnp.int16

D_MODEL = 1024
HEAD_DIM = 64
BLK = 128
A_HEADS = 8
A_PATTERNS = ((128, 1), (512, 4), (2048, 16))
A_PAD = BLK * 16
B_HEADS = 8
B_KV_HEADS = 2
B_WINDOW = 128
C_HEADS = 16
IDX_HEADS = 8
IDX_DIM = 64
TOPK_MAX = 256
REL_BUCKETS = 32
REL_MAX_DIST = 2048
D_FF = 3584
N_EXPERTS = 8
TOP_K = 2
MOE_BLOCK = 512
DEPTH = 4
ALPHA = (2 * DEPTH) ** 0.25
LN_EPS = 1e-5

LANES = 128
SUBLANES = 8
D_TILES = D_MODEL // LANES
NEG = -(2.0 ** 100)
INT_MIN = -(2 ** 31)
HALF = 2 ** 15
VMEM_LIMIT = 56 * 1024 * 1024

FFN_TF = 1792
DSA_CK = 256
DSA_GK = 512
DSA_SKEW = 5


def _bucket_np(dist):
    n = np.maximum(dist, 0)
    max_exact = REL_BUCKETS // 2
    nf = np.maximum(n, 1).astype(np.float32)
    large = max_exact + (np.log(nf / np.float32(max_exact)) / np.float32(math.log(REL_MAX_DIST / max_exact))
                         * np.float32(REL_BUCKETS - max_exact)).astype(np.int32)
    large = np.minimum(large, REL_BUCKETS - 1)
    return np.where(n < max_exact, n, large).astype(np.int32)


def _far_distance():
    b = _bucket_np(np.arange(0, 2 * REL_MAX_DIST))
    return int(np.max(np.nonzero(b != REL_BUCKETS - 1)[0])) + 1


FAR_DIST = _far_distance()
N_NEAR = -(-(FAR_DIST + BLK - 1) // BLK)


def _cparams(sem):
    return pltpu.CompilerParams(dimension_semantics=sem, vmem_limit_bytes=VMEM_LIMIT)


def _layer_norm(z, g, b):
    mu = jnp.mean(z, axis=-1, keepdims=True)
    zc = z - mu
    var = jnp.mean(zc * zc, axis=-1, keepdims=True)
    return zc * lax.rsqrt(var + LN_EPS) * g + b


def _ada_kernel(c_ref, w_ref, b_ref, o_ref):
    j = pl.program_id(1)
    c = c_ref[...]
    sc = c * jax.nn.sigmoid(c)
    mod = jnp.dot(sc, w_ref[0], preferred_element_type=F32, precision=lax.Precision.HIGHEST)
    o_ref[0] = mod + b_ref[0] + jnp.where(j >= 1, 1.0, 0.0)


def ada_modulation_all(c, ada_w, ada_b):
    B, D = c.shape
    n = ada_w.shape[0] * ada_w.shape[1]
    rows = -(-B // SUBLANES) * SUBLANES
    cp = jnp.pad(c, ((0, rows - B), (0, 0)))
    w = ada_w.reshape(n, D, 3 * D)
    b = ada_b.reshape(n, 1, 3 * D)
    out = pl.pallas_call(
        _ada_kernel,
        grid=(n, 3),
        in_specs=[pl.BlockSpec((rows, D), lambda l, j: (0, 0)),
                  pl.BlockSpec((1, D, D), lambda l, j: (l, 0, j)),
                  pl.BlockSpec((1, 1, D), lambda l, j: (l, 0, j))],
        out_specs=pl.BlockSpec((1, rows, D), lambda l, j: (l, 0, j)),
        out_shape=jax.ShapeDtypeStruct((n, rows, 3 * D), F32),
        compiler_params=_cparams(("arbitrary", "arbitrary")),
        name="ada_modulation",
    )(cp, w, b)
    return out[:, :B]


def _inproj_kernel(x_ref, sc_ref, sh_ref, w_ref, *o_refs, splits):
    h = (x_ref[0] * sc_ref[0] + sh_ref[0]).astype(BF16)
    for o_ref, (start, width) in zip(o_refs, splits):
        o_ref[0] = jnp.dot(h, w_ref[:, start:start + width],
                           preferred_element_type=F32).astype(o_ref.dtype)


def in_projection(x, scale, shift, w, groups, tm=512):
    B, S, D = x.shape
    splits, start = [], 0
    for width, _ in groups:
        splits.append((start, width))
        start += width
    assert start == w.shape[1] and S % tm == 0
    return pl.pallas_call(
        functools.partial(_inproj_kernel, splits=tuple(splits)),
        grid=(B, S // tm),
        in_specs=[pl.BlockSpec((1, tm, D), lambda b, i: (b, i, 0)),
                  pl.BlockSpec((1, 1, D), lambda b, i: (b, 0, 0)),
                  pl.BlockSpec((1, 1, D), lambda b, i: (b, 0, 0)),
                  pl.BlockSpec(w.shape, lambda b, i: (0, 0))],
        out_specs=[pl.BlockSpec((1, tm, width), lambda b, i: (b, i, 0)) for width, _ in groups],
        out_shape=[jax.ShapeDtypeStruct((B, S, width), dt) for width, dt in groups],
        compiler_params=_cparams(("parallel", "parallel")),
        name="in_projection",
    )(x, scale, shift, w)


def _band_kernel(sink_ref, q_ref, kp_ref, ko_ref, vp_ref, vo_ref, bias_ref, o_ref, lse_ref, *,
                 hq, hk, use_sinks):
    b = pl.program_id(1)
    q = q_ref[0]
    kk = jnp.concatenate([kp_ref[0], ko_ref[0]], axis=0)
    vv = jnp.concatenate([vp_ref[0], vo_ref[0]], axis=0)
    col = lax.broadcasted_iota(I32, (BLK, 2 * BLK), 1)
    first_mask = jnp.where(jnp.logical_and(b == 0, col < BLK), NEG, 0.0)
    group = hq // hk
    for h in range(hq):
        g = h // group
        qh = q[:, h * HEAD_DIM:(h + 1) * HEAD_DIM]
        kh = kk[:, g * HEAD_DIM:(g + 1) * HEAD_DIM]
        vh = vv[:, g * HEAD_DIM:(g + 1) * HEAD_DIM]
        s = lax.dot_general(qh, kh, (((1,), (1,)), ((), ())), preferred_element_type=F32)
        s = s + bias_ref[h] + first_mask
        m = jnp.max(s, axis=-1, keepdims=True)
        if use_sinks:
            m = jnp.maximum(m, sink_ref[h])
        p = jnp.exp(s - m)
        l = jnp.sum(p, axis=-1, keepdims=True)
        if use_sinks:
            l = l + jnp.exp(sink_ref[h] - m)
        o = jnp.dot(p.astype(BF16), vh, preferred_element_type=F32)
        o_ref[0, :, h * HEAD_DIM:(h + 1) * HEAD_DIM] = (o / l).astype(o_ref.dtype)
        lse_ref[0, :, h:h + 1] = m + jnp.log(l)


def banded_attention(q, k, v, bias, sinks, out_dtype):
    N, L, qc = q.shape
    hq = qc // HEAD_DIM
    hk = k.shape[2] // HEAD_DIM
    use_sinks = sinks is not None
    if sinks is None:
        sinks = jnp.zeros((hq,), F32)
    cur = lambda n, b, s: (n, b, 0)
    prev = lambda n, b, s: (n, jnp.maximum(b - 1, 0), 0)
    return pl.pallas_call(
        functools.partial(_band_kernel, hq=hq, hk=hk, use_sinks=use_sinks),
        grid_spec=pltpu.PrefetchScalarGridSpec(
            num_scalar_prefetch=1,
            grid=(N, L // BLK),
            in_specs=[pl.BlockSpec((1, BLK, qc), cur),
                      pl.BlockSpec((1, BLK, hk * HEAD_DIM), prev),
                      pl.BlockSpec((1, BLK, hk * HEAD_DIM), cur),
                      pl.BlockSpec((1, BLK, hk * HEAD_DIM), prev),
                      pl.BlockSpec((1, BLK, hk * HEAD_DIM), cur),
                      pl.BlockSpec(bias.shape, lambda n, b, s: (0, 0, 0))],
            out_specs=[pl.BlockSpec((1, BLK, qc), cur),
                       pl.BlockSpec((1, BLK, hq), cur)]),
        out_shape=[jax.ShapeDtypeStruct((N, L, qc), out_dtype),
                   jax.ShapeDtypeStruct((N, L, hq), F32)],
        compiler_params=_cparams(("parallel", "arbitrary")),
        name="banded_attention",
    )(sinks, q, k, k, v, v, bias)


def band_bias(table, dilation, max_dist):
    qi = np.arange(BLK)[:, None]
    kj = np.arange(2 * BLK)[None, :]
    dist = qi + BLK - kj
    allowed = (dist >= 0) & (dist <= max_dist)
    bias = table[_bucket_np(dist * dilation)]
    bias = jnp.where(allowed[:, :, None], bias, NEG)
    return jnp.transpose(bias, (2, 0, 1)).astype(F32)


def _combine_kernel(*refs, n_pat, heads):
    o_refs, l_refs, out_ref = refs[:n_pat], refs[n_pat:2 * n_pat], refs[2 * n_pat]
    lses = [r[0] for r in l_refs]
    m = functools.reduce(jnp.maximum, lses)
    es = [jnp.exp(l - m) for l in lses]
    tot = functools.reduce(lambda a, b: a + b, es)
    ws = [e / tot for e in es]
    for h in range(heads):
        sl = slice(h * HEAD_DIM, (h + 1) * HEAD_DIM)
        acc = ws[0][:, h:h + 1] * o_refs[0][0, :, sl]
        for p in range(1, n_pat):
            acc = acc + ws[p][:, h:h + 1] * o_refs[p][0, :, sl]
        out_ref[0, :, sl] = acc.astype(out_ref.dtype)


def combine_patterns(outs, lses, tm=512):
    B, S, C = outs[0].shape
    heads = C // HEAD_DIM
    n_pat = len(outs)
    idx = lambda b, i: (b, i, 0)
    return pl.pallas_call(
        functools.partial(_combine_kernel, n_pat=n_pat, heads=heads),
        grid=(B, S // tm),
        in_specs=[pl.BlockSpec((1, tm, C), idx)] * n_pat + [pl.BlockSpec((1, tm, heads), idx)] * n_pat,
        out_specs=pl.BlockSpec((1, tm, C), idx),
        out_shape=jax.ShapeDtypeStruct((B, S, C), BF16),
        compiler_params=_cparams(("parallel", "parallel")),
        name="combine_patterns",
    )(*outs, *lses)


def _outproj_kernel(*refs, n_parts):
    o_refs, w_refs = refs[:n_parts], refs[n_parts:2 * n_parts]
    x_ref, gate_ref, g_ref, b_ref, out_ref = refs[2 * n_parts:]
    mix = jnp.dot(o_refs[0][0], w_refs[0][...], preferred_element_type=F32)
    for o_ref, w_ref in zip(o_refs[1:], w_refs[1:]):
        mix = mix + jnp.dot(o_ref[0], w_ref[...], preferred_element_type=F32)
    z = ALPHA * x_ref[0] + gate_ref[0] * mix
    out_ref[0] = _layer_norm(z, g_ref[...], b_ref[...])


def out_projection_ln(parts, weights, x, gate, ln_g, ln_b, tm=512):
    B, S, D = x.shape
    idx = lambda b, i: (b, i, 0)
    const2 = lambda b, i: (0, 0)
    return pl.pallas_call(
        functools.partial(_outproj_kernel, n_parts=len(parts)),
        grid=(B, S // tm),
        in_specs=([pl.BlockSpec((1, tm, p.shape[2]), idx) for p in parts]
                  + [pl.BlockSpec(w.shape, const2) for w in weights]
                  + [pl.BlockSpec((1, tm, D), idx),
                     pl.BlockSpec((1, 1, D), lambda b, i: (b, 0, 0)),
                     pl.BlockSpec((1, D), const2),
                     pl.BlockSpec((1, D), const2)]),
        out_specs=pl.BlockSpec((1, tm, D), idx),
        out_shape=jax.ShapeDtypeStruct((B, S, D), F32),
        compiler_params=_cparams(("parallel", "parallel")),
        name="out_projection_ln",
    )(*parts, *weights, x, gate, ln_g.reshape(1, D), ln_b.reshape(1, D))


def _dsa_kernel(q_ref, qi_ref, wi_ref, kit_ref, kct_ref, va_ref, vb_ref, strip_ref, o_ref,
                sk_scr, hi_scr, lo_scr, qm_scr, qim_scr, wib_scr, mb_scr, s_scr, m_scr, acc_scr, *, k_sel):
    i = pl.program_id(1)
    ck = DSA_CK
    sub = ck // LANES
    gt = DSA_GK // LANES
    gc = DSA_GK // ck
    nbuf = DSA_SKEW + 1
    ngr = (i * BLK + BLK + DSA_GK - 1) // DSA_GK
    nck = ngr * gc
    lane = lax.broadcasted_iota(I32, (BLK, LANES), 1)
    upper = lane >= HEAD_DIM

    for p in range(C_HEADS // 2):
        qp = q_ref[0, :, p * LANES:(p + 1) * LANES]
        qm_scr[2 * p] = jnp.where(upper, jnp.zeros_like(qp), qp)
        qm_scr[2 * p + 1] = jnp.where(upper, qp, jnp.zeros_like(qp))
    for p in range(IDX_HEADS // 2):
        qp = qi_ref[0, :, p * LANES:(p + 1) * LANES]
        qim_scr[2 * p] = jnp.where(upper, jnp.zeros_like(qp), qp)
        qim_scr[2 * p + 1] = jnp.where(upper, qp, jnp.zeros_like(qp))
    wscale = IDX_HEADS ** -0.5 * IDX_DIM ** -0.5
    for h in range(IDX_HEADS):
        wib_scr[h] = jnp.broadcast_to(wi_ref[0, :, h:h + 1] * wscale, (BLK, LANES))

    row_t = lax.broadcasted_iota(I32, (BLK, ck), 0) + i * BLK
    col_l = lax.broadcasted_iota(I32, (BLK, ck), 1)

    def score_chunk(j, carry):
        kt = kit_ref[0, j]
        sc = jnp.zeros((BLK, ck), F32)
        for h in range(IDX_HEADS):
            d = jnp.dot(qim_scr[h], kt, preferred_element_type=F32)
            w = wib_scr[h]
            sc = sc + jnp.maximum(d, 0.0) * jnp.concatenate([w] * sub, axis=1)
        bits = pltpu.bitcast(sc, I32)
        key = bits ^ ((bits >> 31) & 0x7FFFFFFF)
        key = jnp.where(col_l + j * ck <= row_t, key, INT_MIN)
        for u in range(sub):
            tile = key[:, u * LANES:(u + 1) * LANES]
            sk_scr[j * sub + u] = tile
            tile_t = tile.T
            hi_scr[j * sub + u] = (tile_t >> 16).astype(I16)
            lo_scr[j * sub + u] = ((tile_t & 0xFFFF) - HALF).astype(I16)
        return carry

    lax.fori_loop(0, nck, score_chunk, 0)

    packed = 2 * SUBLANES

    def count(scr, cond):
        def group(g, acc):
            parts = []
            for u in range(gt):
                hit = jnp.where(cond(scr[g * gt + u]), jnp.int16(1), jnp.int16(0))
                parts += [hit[r:r + packed] for r in range(0, LANES, packed)]
            while len(parts) > 1:
                parts = [a + b for a, b in zip(parts[0::2], parts[1::2])]
            return acc + parts[0]
        acc = lax.fori_loop(0, ngr, group, jnp.zeros((packed, LANES), I16))
        return jnp.sum(acc.astype(I32), axis=0, keepdims=True)

    def select16(scr, target):
        def bit_pass(t, v):
            c = v | lax.shift_left(jnp.int32(1), 15 - t)
            c16 = (c - HALF).astype(I16)
            return jnp.where(count(scr, lambda x: x >= c16) >= target, c, v)
        return lax.fori_loop(0, 16, bit_pass, jnp.zeros((1, LANES), I32))

    v_hi = select16(hi_scr, k_sel)
    hi16 = (v_hi - HALF).astype(I16)
    above = count(hi_scr, lambda x: x > hi16)
    def keep_low(g, carry):
        for u in range(gt):
            idx = g * gt + u
            lo_scr[idx] = jnp.where(hi_scr[idx] == hi16, lo_scr[idx], jnp.int16(-HALF))
        return carry
    lax.fori_loop(0, ngr, keep_low, 0)
    v_lo = select16(lo_scr, k_sel - above)
    thr_q = jnp.maximum(lax.shift_left(v_hi - HALF, 16) | v_lo, INT_MIN + 1)
    thr = jnp.broadcast_to(thr_q, (BLK, LANES)).T

    m_scr[...] = jnp.full(m_scr.shape, NEG, F32)
    acc_scr[...] = jnp.zeros(acc_scr.shape, F32)

    def attend_group(g, carry):
        for u in range(gt):
            mb_scr[u] = jnp.where(sk_scr[g * gt + u] >= thr, 0.0, NEG).astype(BF16)
        tiles = [jnp.clip(i - (g * gt + u), 0, N_NEAR) for u in range(gt)]

        def logits(h):
            mx = None
            for c in range(gc):
                s = jnp.dot(qm_scr[h], kct_ref[0, g * gc + c], preferred_element_type=F32).astype(BF16)
                for u in range(sub):
                    t = c * sub + u
                    piece = s[:, u * LANES:(u + 1) * LANES] + mb_scr[t] + strip_ref[h, tiles[t]]
                    s_scr[h % nbuf, t] = piece
                    mx = piece if mx is None else jnp.maximum(mx, piece)
            m_old = m_scr[h]
            m_new = jnp.maximum(m_old, jnp.max(mx.astype(F32), axis=1, keepdims=True))
            m_scr[h] = m_new
            return m_old, m_new

        def accumulate(h, m_old, m_new):
            m16 = m_new.astype(BF16)
            p = jnp.concatenate([jnp.exp(s_scr[h % nbuf, t] - m16) for t in range(gt)], axis=1)
            pv = jnp.dot(p, (va_ref if h % 2 == 0 else vb_ref)[0, g], preferred_element_type=F32)
            acc_scr[h] = jnp.exp(m_old - m_new) * acc_scr[h] + pv

        stats = [logits(h) for h in range(DSA_SKEW)]
        for h in range(C_HEADS):
            if h + DSA_SKEW < C_HEADS:
                stats.append(logits(h + DSA_SKEW))
            accumulate(h, *stats[h])
        return carry

    lax.fori_loop(0, ngr, attend_group, 0)

    for p in range(C_HEADS // 2):
        a0 = acc_scr[2 * p]
        a1 = acc_scr[2 * p + 1]
        o = jnp.where(upper, a1 / pltpu.roll(a1, HEAD_DIM, 1), a0 / pltpu.roll(a0, HEAD_DIM, 1))
        o_ref[0, :, p * LANES:(p + 1) * LANES] = o.astype(o_ref.dtype)


def dsa_attention(q, qi, wi, ki, kc, vc, table):
    B, S, _ = q.shape
    ck = DSA_CK
    assert S % DSA_GK == 0
    nchunks = S // ck
    k_sel = min(TOPK_MAX, S // 4)

    def chunked_t(t):
        tt = jnp.transpose(t.reshape(B, nchunks, ck, HEAD_DIM), (0, 1, 3, 2))
        return jnp.concatenate([tt, tt], axis=2)

    ngroups = S // DSA_GK
    ones = jnp.ones_like(vc)
    va = jnp.concatenate([vc, ones], axis=-1).reshape(B, ngroups, DSA_GK, LANES)
    vb = jnp.concatenate([ones, vc], axis=-1).reshape(B, ngroups, DSA_GK, LANES)
    j = np.arange(2 * BLK)
    offs = np.where(j <= BLK, -j, 2 * BLK - j)
    v = table[_bucket_np(np.arange(N_NEAR)[:, None] * BLK + offs[None, :])]
    v = jnp.transpose(v, (2, 0, 1)).astype(F32)
    near = jnp.tile(v, (1, 1, BLK))[..., :BLK * (2 * BLK - 1)]
    near = near.reshape(C_HEADS, N_NEAR, BLK, 2 * BLK - 1)[..., :BLK]
    far = jnp.broadcast_to(table[REL_BUCKETS - 1].astype(F32)[:, None, None, None],
                           (C_HEADS, 1, BLK, LANES))
    strip = jnp.concatenate([near, far], axis=1).astype(BF16)

    once = pl.Buffered(1)
    qblk = lambda b, i: (b, i, 0)
    per_b = lambda b, i: (b, 0, 0, 0)
    return pl.pallas_call(
        functools.partial(_dsa_kernel, k_sel=k_sel),
        grid=(B, S // BLK),
        in_specs=[pl.BlockSpec((1, BLK, C_HEADS * HEAD_DIM), qblk),
                  pl.BlockSpec((1, BLK, IDX_HEADS * IDX_DIM), qblk),
                  pl.BlockSpec((1, BLK, IDX_HEADS), qblk),
                  pl.BlockSpec((1, nchunks, LANES, ck), per_b, pipeline_mode=once),
                  pl.BlockSpec((1, nchunks, LANES, ck), per_b, pipeline_mode=once),
                  pl.BlockSpec((1, ngroups, DSA_GK, LANES), per_b, pipeline_mode=once),
                  pl.BlockSpec((1, ngroups, DSA_GK, LANES), per_b, pipeline_mode=once),
                  pl.BlockSpec(strip.shape, lambda b, i: (0, 0, 0, 0), pipeline_mode=once)],
        out_specs=pl.BlockSpec((1, BLK, C_HEADS * HEAD_DIM), qblk),
        scratch_shapes=[pltpu.VMEM((S // LANES, BLK, LANES), I32),
                        pltpu.VMEM((S // LANES, LANES, BLK), I16),
                        pltpu.VMEM((S // LANES, LANES, BLK), I16),
                        pltpu.VMEM((C_HEADS, BLK, LANES), BF16),
                        pltpu.VMEM((IDX_HEADS, BLK, LANES), BF16),
                        pltpu.VMEM((IDX_HEADS, BLK, LANES), F32),
                        pltpu.VMEM((DSA_GK // LANES, BLK, LANES), BF16),
                        pltpu.VMEM((DSA_SKEW + 1, DSA_GK // LANES, BLK, LANES), BF16),
                        pltpu.VMEM((C_HEADS, BLK, LANES), F32),
                        pltpu.VMEM((C_HEADS, BLK, LANES), F32)],
        out_shape=jax.ShapeDtypeStruct((B, S, C_HEADS * HEAD_DIM), BF16),
        compiler_params=_cparams(("parallel", "arbitrary")),
        name="dsa_attention",
    )(q, qi, wi, chunked_t(ki), chunked_t(kc), va, vb, strip)


def _ffn_step(x2d, w1_ref, w3_ref, w2_ref, acc):
    xb = x2d[...]
    h1 = jnp.dot(xb, w1_ref[0], preferred_element_type=F32)
    h3 = jnp.dot(xb, w3_ref[0], preferred_element_type=F32)
    a = (h1 * jax.nn.sigmoid(h1) * h3).astype(BF16)
    acc[...] += jnp.dot(a, w2_ref[0], preferred_element_type=F32)


def _ffn_dense_kernel(x_ref, sc_ref, sh_ref, gate_ref, g_ref, b_ref, w1_ref, w3_ref, w2_ref, out_ref,
                      x2d, acc, *, nf):
    f = pl.program_id(2)

    @pl.when(f == 0)
    def _():
        x2d[...] = (x_ref[0] * sc_ref[0] + sh_ref[0]).astype(BF16)
        acc[...] = jnp.zeros(acc.shape, F32)

    _ffn_step(x2d, w1_ref, w3_ref, w2_ref, acc)

    @pl.when(f == nf - 1)
    def _():
        z = ALPHA * x_ref[0] + gate_ref[0] * acc[...]
        out_ref[0] = _layer_norm(z, g_ref[...], b_ref[...])


def ffn_dense_ln(x, scale, shift, gate, ln_g, ln_b, w1, w3, w2, tm=512, tf=FFN_TF):
    B, S, D = x.shape
    nf = D_FF // tf
    xi = lambda b, i, f: (b, i, 0)
    bi = lambda b, i, f: (b, 0, 0)
    c2 = lambda b, i, f: (0, 0)
    return pl.pallas_call(
        functools.partial(_ffn_dense_kernel, nf=nf),
        grid=(B, S // tm, nf),
        in_specs=[pl.BlockSpec((1, tm, D), xi),
                  pl.BlockSpec((1, 1, D), bi), pl.BlockSpec((1, 1, D), bi), pl.BlockSpec((1, 1, D), bi),
                  pl.BlockSpec((1, D), c2), pl.BlockSpec((1, D), c2),
                  pl.BlockSpec((1, D, tf), lambda b, i, f: (0, 0, f)),
                  pl.BlockSpec((1, D, tf), lambda b, i, f: (0, 0, f)),
                  pl.BlockSpec((1, tf, D), lambda b, i, f: (0, f, 0))],
        out_specs=pl.BlockSpec((1, tm, D), xi),
        out_shape=jax.ShapeDtypeStruct((B, S, D), F32),
        scratch_shapes=[pltpu.VMEM((tm, D), BF16), pltpu.VMEM((tm, D), F32)],
        compiler_params=_cparams(("parallel", "parallel", "arbitrary")),
        name="ffn_dense_ln",
    )(x, scale, shift, gate, ln_g.reshape(1, D), ln_b.reshape(1, D), w1, w3, w2)


def _ffn_expert_kernel(be_ref, tok_ref, tok_next_ref, dst_ref, h_hbm, w1_ref, w3_ref, w2_ref, y_hbm,
                       xbuf, ybuf, x2d, acc, gsem, ssem, *, nf, tm):
    i = pl.program_id(0)
    f = pl.program_id(1)
    n_blocks = pl.num_programs(0)
    slot = i % 2
    rows = tm * D_TILES

    def start_gather(idx_ref, s):
        def body(r, carry):
            src = pl.multiple_of(idx_ref[0, 0, r] * D_TILES, D_TILES)
            dst = pl.multiple_of(r * D_TILES, D_TILES)
            pltpu.make_async_copy(h_hbm.at[pl.ds(src, D_TILES)], xbuf.at[s, pl.ds(dst, D_TILES)],
                                  gsem.at[s]).start()
            return carry
        lax.fori_loop(0, tm, body, 0, unroll=8)

    def wait_gather(s):
        pltpu.make_async_copy(h_hbm.at[pl.ds(0, rows)], xbuf.at[s], gsem.at[s]).wait()

    def start_scatter():
        def body(r, carry):
            src = pl.multiple_of(r * D_TILES, D_TILES)
            dst = pl.multiple_of(dst_ref[0, 0, r] * D_TILES, D_TILES)
            pltpu.make_async_copy(ybuf.at[pl.ds(src, D_TILES)], y_hbm.at[pl.ds(dst, D_TILES)], ssem).start()
            return carry
        lax.fori_loop(0, tm, body, 0, unroll=8)

    def wait_scatter():
        pltpu.make_async_copy(ybuf, y_hbm.at[pl.ds(0, rows)], ssem).wait()

    @pl.when(f == 0)
    def _():
        @pl.when(i == 0)
        def _():
            start_gather(tok_ref, 0)

        @pl.when(i + 1 < n_blocks)
        def _():
            start_gather(tok_next_ref, 1 - slot)

        wait_gather(slot)
        for j in range(D_TILES):
            x2d[:, j * LANES:(j + 1) * LANES] = xbuf[slot, pl.ds(j, tm, stride=D_TILES), :].astype(BF16)
        acc[...] = jnp.zeros(acc.shape, F32)

    _ffn_step(x2d, w1_ref, w3_ref, w2_ref, acc)

    @pl.when(f == nf - 1)
    def _():
        @pl.when(i > 0)
        def _():
            wait_scatter()

        for j in range(D_TILES):
            ybuf[pl.ds(j, tm, stride=D_TILES), :] = acc[:, j * LANES:(j + 1) * LANES]
        start_scatter()

        @pl.when(i == n_blocks - 1)
        def _():
            wait_scatter()


def ffn_experts(h_tiles, slot_tok, slot_dst, block_expert, w1, w3, w2, tf=FFN_TF):
    tm = MOE_BLOCK
    P = slot_tok.shape[0]
    n_blocks = P // tm
    nf = D_FF // tf
    tok3 = slot_tok.reshape(n_blocks, 1, tm)
    dst3 = slot_dst.reshape(n_blocks, 1, tm)
    cur = lambda i, f, be: (i, 0, 0)
    nxt = lambda i, f, be: (jnp.minimum(i + 1, n_blocks - 1), 0, 0)
    return pl.pallas_call(
        functools.partial(_ffn_expert_kernel, nf=nf, tm=tm),
        grid_spec=pltpu.PrefetchScalarGridSpec(
            num_scalar_prefetch=1,
            grid=(n_blocks, nf),
            in_specs=[pl.BlockSpec((1, 1, tm), cur, memory_space=pltpu.SMEM),
                      pl.BlockSpec((1, 1, tm), nxt, memory_space=pltpu.SMEM),
                      pl.BlockSpec((1, 1, tm), cur, memory_space=pltpu.SMEM),
                      pl.BlockSpec(memory_space=pl.ANY),
                      pl.BlockSpec((1, D_MODEL, tf), lambda i, f, be: (be[i], 0, f)),
                      pl.BlockSpec((1, D_MODEL, tf), lambda i, f, be: (be[i], 0, f)),
                      pl.BlockSpec((1, tf, D_MODEL), lambda i, f, be: (be[i], f, 0))],
            out_specs=pl.BlockSpec(memory_space=pl.ANY),
            scratch_shapes=[pltpu.VMEM((2, tm * D_TILES, LANES), F32),
                            pltpu.VMEM((tm * D_TILES, LANES), F32),
                            pltpu.VMEM((tm, D_MODEL), BF16),
                            pltpu.VMEM((tm, D_MODEL), F32),
                            pltpu.SemaphoreType.DMA((2,)),
                            pltpu.SemaphoreType.DMA(())]),
        out_shape=jax.ShapeDtypeStruct((P * D_TILES, LANES), F32),
        compiler_params=_cparams(("arbitrary", "arbitrary")),
        name="ffn_experts",
    )(block_expert, tok3, tok3, dst3, h_tiles, w1, w3, w2)


def _router_kernel(x_ref, sc_ref, sh_ref, wr_ref, h_ref, idx_ref, gate_ref, *, tm):
    h = x_ref[0] * sc_ref[0] + sh_ref[0]
    for j in range(D_TILES):
        h_ref[pl.ds(j, tm, stride=D_TILES), :] = h[:, j * LANES:(j + 1) * LANES]
    logits = jnp.dot(h, wr_ref[...], preferred_element_type=F32, precision=lax.Precision.HIGHEST)
    lane = lax.broadcasted_iota(I32, logits.shape, 1)
    lg = jnp.where(lane < N_EXPERTS, logits, -jnp.inf)
    m1 = jnp.max(lg, axis=1, keepdims=True)
    i1 = jnp.min(jnp.where(lg == m1, lane, LANES), axis=1, keepdims=True)
    lg2 = jnp.where(lane == i1, -jnp.inf, lg)
    m2 = jnp.max(lg2, axis=1, keepdims=True)
    i2 = jnp.min(jnp.where(lg2 == m2, lane, LANES), axis=1, keepdims=True)
    e = jnp.exp(m2 - m1)
    idx_ref[:, 0:1] = i1
    idx_ref[:, 1:2] = i2
    gate_ref[:, 0:1] = 1.0 / (1.0 + e)
    gate_ref[:, 1:2] = e / (1.0 + e)


def route_tokens(x, scale, shift, w_router, tm=512):
    B, S, D = x.shape
    T = B * S
    nb = S // tm
    wr = jnp.pad(w_router, ((0, 0), (0, LANES - N_EXPERTS)))
    return pl.pallas_call(
        functools.partial(_router_kernel, tm=tm),
        grid=(B, nb),
        in_specs=[pl.BlockSpec((1, tm, D), lambda b, i: (b, i, 0)),
                  pl.BlockSpec((1, 1, D), lambda b, i: (b, 0, 0)),
                  pl.BlockSpec((1, 1, D), lambda b, i: (b, 0, 0)),
                  pl.BlockSpec((D, LANES), lambda b, i: (0, 0))],
        out_specs=[pl.BlockSpec((tm * D_TILES, LANES), lambda b, i: (b * nb + i, 0)),
                   pl.BlockSpec((tm, TOP_K), lambda b, i: (b * nb + i, 0)),
                   pl.BlockSpec((tm, TOP_K), lambda b, i: (b * nb + i, 0))],
        out_shape=[jax.ShapeDtypeStruct((T * D_TILES, LANES), F32),
                   jax.ShapeDtypeStruct((T, TOP_K), I32),
                   jax.ShapeDtypeStruct((T, TOP_K), F32)],
        compiler_params=_cparams(("parallel", "parallel")),
        name="route_tokens",
    )(x, scale, shift, wr)


def _moe_combine_kernel(y_ref, gt_ref, x_ref, gate_ref, g_ref, b_ref, out_ref, *, tm):
    g0 = gt_ref[:, 0:1]
    g1 = gt_ref[:, 1:2]
    stride = TOP_K * D_TILES
    pieces = []
    for j in range(D_TILES):
        sl = slice(j * LANES, (j + 1) * LANES)
        ff = g0 * y_ref[pl.ds(j, tm, stride=stride), :] + g1 * y_ref[pl.ds(D_TILES + j, tm, stride=stride), :]
        pieces.append(ALPHA * x_ref[0, :, sl] + gate_ref[0, :, sl] * ff)
    z = jnp.concatenate(pieces, axis=1)
    out_ref[0] = _layer_norm(z, g_ref[...], b_ref[...])


def moe_combine_ln(yg, gates, x, gate, ln_g, ln_b, tm=512):
    B, S, D = x.shape
    nb = S // tm
    return pl.pallas_call(
        functools.partial(_moe_combine_kernel, tm=tm),
        grid=(B, nb),
        in_specs=[pl.BlockSpec((tm * TOP_K * D_TILES, LANES), lambda b, i: (b * nb + i, 0)),
                  pl.BlockSpec((tm, TOP_K), lambda b, i: (b * nb + i, 0)),
                  pl.BlockSpec((1, tm, D), lambda b, i: (b, i, 0)),
                  pl.BlockSpec((1, 1, D), lambda b, i: (b, 0, 0)),
                  pl.BlockSpec((1, D), lambda b, i: (0, 0)),
                  pl.BlockSpec((1, D), lambda b, i: (0, 0))],
        out_specs=pl.BlockSpec((1, tm, D), lambda b, i: (b, i, 0)),
        out_shape=jax.ShapeDtypeStruct((B, S, D), F32),
        compiler_params=_cparams(("parallel", "parallel")),
        name="moe_combine_ln",
    )(yg, gates, x, gate, ln_g.reshape(1, D), ln_b.reshape(1, D))


def _scale_cols(w, start, width, factor):
    return w.at[:, start:start + width].multiply(factor)


def even_mixer_layer(x, mod, w_in, w_out, rel_table, sinks, ln_g, ln_b):
    B, S, D = x.shape
    assert S % A_PAD == 0
    shift, scale, gate = mod
    ah, bh, bk = A_HEADS * HEAD_DIM, B_HEADS * HEAD_DIM, B_KV_HEADS * HEAD_DIM
    w = _scale_cols(w_in, 0, ah, HEAD_DIM ** -0.5)
    w = _scale_cols(w, 3 * ah, bh, HEAD_DIM ** -0.5).astype(BF16)
    groups = [(ah, BF16)] * 3 + [(bh, BF16), (bk, BF16), (bk, BF16)]
    qa, ka, va, qb, kb, vb = in_projection(x, scale, shift, w, groups)

    outs, lses = [], []
    for window, d in A_PATTERNS:
        ld = S // d

        def regroup(t):
            return jnp.transpose(t.reshape(B, ld, d, ah), (0, 2, 1, 3)).reshape(B * d, ld, ah)

        bias = band_bias(rel_table[:, :A_HEADS], d, window // d)
        o, lse = banded_attention(regroup(qa), regroup(ka), regroup(va), bias, None, F32)
        outs.append(jnp.transpose(o.reshape(B, d, ld, ah), (0, 2, 1, 3)).reshape(B, S, ah))
        lses.append(jnp.transpose(lse.reshape(B, d, ld, A_HEADS), (0, 2, 1, 3)).reshape(B, S, A_HEADS))
    oa = combine_patterns(outs, lses)

    bias_b = band_bias(rel_table[:, A_HEADS:A_HEADS + B_HEADS], 1, B_WINDOW - 1)
    ob, _ = banded_attention(qb, kb, vb, bias_b, sinks.astype(F32), BF16)

    wo = w_out.astype(BF16)
    return out_projection_ln([oa, ob], [wo[:ah], wo[ah:]], x, gate, ln_g, ln_b)


def dsa_mixer_layer(x, mod, w_in, w_out, rel_table, ln_g, ln_b):
    shift, scale, gate = mod
    qw = C_HEADS * HEAD_DIM
    iw = IDX_HEADS * IDX_DIM
    q_w, kc_w, vc_w, qi_w, ki_w, wi_w = jnp.split(
        w_in, [qw, qw + HEAD_DIM, qw + 2 * HEAD_DIM, qw + 2 * HEAD_DIM + iw, qw + 2 * HEAD_DIM + iw + IDX_DIM],
        axis=1)
    pad = jnp.zeros((D_MODEL, LANES - IDX_DIM - IDX_HEADS), w_in.dtype)
    w = jnp.concatenate([q_w * HEAD_DIM ** -0.5, kc_w, vc_w, qi_w, ki_w, wi_w, pad], axis=1).astype(BF16)
    groups = [(qw, BF16), (2 * HEAD_DIM, BF16), (iw, BF16), (LANES, F32)]
    q, kv, qi, kw = in_projection(x, scale, shift, w, groups)
    kc, vc = kv[..., :HEAD_DIM], kv[..., HEAD_DIM:]
    ki = kw[..., :IDX_DIM].astype(BF16)
    wi = kw[..., IDX_DIM:IDX_DIM + IDX_HEADS]
    o = dsa_attention(q, qi, wi, ki, kc, vc, rel_table[:, :C_HEADS])
    return out_projection_ln([o], [w_out.astype(BF16)], x, gate, ln_g, ln_b)


def moe_layer(x, mod, w_router, w1, w3, w2, ln_g, ln_b):
    B, S, D = x.shape
    shift, scale, gate = mod
    T = B * S
    A = T * TOP_K
    h_tiles, top_idx, gates = route_tokens(x, scale, shift, w_router)

    e_flat = top_idx.reshape(-1)
    onehot = (e_flat[:, None] == jnp.arange(N_EXPERTS)[None, :]).astype(I32)
    rank = jnp.take_along_axis(jnp.cumsum(onehot, axis=0) - onehot, e_flat[:, None], axis=1)[:, 0]
    counts = jnp.sum(onehot, axis=0)
    padded = (counts + MOE_BLOCK - 1) // MOE_BLOCK * MOE_BLOCK
    pends = jnp.cumsum(padded)
    dest = (pends - padded)[e_flat] + rank
    n_blocks = -(-A // MOE_BLOCK) + N_EXPERTS
    P = n_blocks * MOE_BLOCK
    slot_src = jnp.full((P,), -1, I32).at[dest].set(jnp.arange(A, dtype=I32))
    is_pad = slot_src < 0
    slot_tok = jnp.where(is_pad, 0, slot_src // TOP_K)
    slot_dst = jnp.where(is_pad, A - 1 + jnp.cumsum(is_pad.astype(I32)), slot_src)
    block_expert = jnp.minimum(
        jnp.searchsorted(pends, jnp.arange(n_blocks) * MOE_BLOCK, side='right'), N_EXPERTS - 1).astype(I32)

    yg = ffn_experts(h_tiles, slot_tok, slot_dst, block_expert,
                     w1.astype(BF16), w3.astype(BF16), w2.astype(BF16))
    return moe_combine_ln(yg, gates, x, gate, ln_g, ln_b)


def kernel(x, c, rel_table, w_in_even, w_out_even, sinks, w_in_odd, w_out_odd, ffn_w1, ffn_w3, ffn_w2,
           router, exp_w1, exp_w3, exp_w2, ada_w, ada_b, ln_g, ln_b):
    D = D_MODEL
    mods = ada_modulation_all(c, ada_w, ada_b)

    def mod(layer, sub):
        m = mods[2 * layer + sub]
        return m[:, None, :D], m[:, None, D:2 * D], m[:, None, 2 * D:]

    for layer in range(DEPTH):
        i = layer // 2
        if layer % 2 == 0:
            x = even_mixer_layer(x, mod(layer, 0), w_in_even[i], w_out_even[i], rel_table, sinks[i],
                                 ln_g[layer, 0], ln_b[layer, 0])
            shift, scale, gate = mod(layer, 1)
            x = ffn_dense_ln(x, scale, shift, gate, ln_g[layer, 1], ln_b[layer, 1],
                             ffn_w1[i][None].astype(BF16), ffn_w3[i][None].astype(BF16),
                             ffn_w2[i][None].astype(BF16))
        else:
            x = dsa_mixer_layer(x, mod(layer, 0), w_in_odd[i], w_out_odd[i], rel_table,
                                ln_g[layer, 0], ln_b[layer, 0])
            x = moe_layer(x, mod(layer, 1), router[i], exp_w1[i], exp_w3[i], exp_w2[i],
                          ln_g[layer, 1], ln_b[layer, 1])
    return x
```

```python
import functools
import math

import numpy as np
import jax
import jax.numpy as jnp
from jax import lax
from jax.experimental import pallas as pl
from jax.experimental.pallas import tpu as pltpu

F32 = jnp.float32
BF16 = jnp.bfloat16
I32 = jnp.int32
I16 = jnp.int16

D_MODEL = 1024
HEAD_DIM = 64
BLK = 128
A_HEADS = 8
A_PATTERNS = ((128, 1), (512, 4), (2048, 16))
A_PAD = BLK * 16
B_HEADS = 8
B_KV_HEADS = 2
B_WINDOW = 128
C_HEADS = 16
IDX_HEADS = 8
IDX_DIM = 64
TOPK_MAX = 256
REL_BUCKETS = 32
REL_MAX_DIST = 2048
D_FF = 3584
N_EXPERTS = 8
TOP_K = 2
MOE_BLOCK = 512
DEPTH = 4
ALPHA = (2 * DEPTH) ** 0.25
LN_EPS = 1e-5

LANES = 128
SUBLANES = 8
D_TILES = D_MODEL // LANES
NEG = -(2.0 ** 100)
INT_MIN = -(2 ** 31)
HALF = 2 ** 15
VMEM_LIMIT = 56 * 1024 * 1024

FFN_TF = 1792
DSA_CK = 256
DSA_GK = 512
DSA_HPS = 4
DSA_SKEW = 2


def _bucket_np(dist):
    n = np.maximum(dist, 0)
    max_exact = REL_BUCKETS // 2
    nf = np.maximum(n, 1).astype(np.float32)
    large = max_exact + (np.log(nf / np.float32(max_exact)) / np.float32(math.log(REL_MAX_DIST / max_exact))
                         * np.float32(REL_BUCKETS - max_exact)).astype(np.int32)
    large = np.minimum(large, REL_BUCKETS - 1)
    return np.where(n < max_exact, n, large).astype(np.int32)


def _far_distance():
    b = _bucket_np(np.arange(0, 2 * REL_MAX_DIST))
    return int(np.max(np.nonzero(b != REL_BUCKETS - 1)[0])) + 1


FAR_DIST = _far_distance()
N_NEAR = -(-(FAR_DIST + BLK - 1) // BLK)


def _cparams(sem):
    return pltpu.CompilerParams(dimension_semantics=sem, vmem_limit_bytes=VMEM_LIMIT)


def _layer_norm(z, g, b):
    mu = jnp.mean(z, axis=-1, keepdims=True)
    zc = z - mu
    var = jnp.mean(zc * zc, axis=-1, keepdims=True)
    return zc * lax.rsqrt(var + LN_EPS) * g + b


def _ada_kernel(c_ref, w_ref, b_ref, o_ref):
    j = pl.program_id(1)
    c = c_ref[...]
    sc = c * jax.nn.sigmoid(c)
    mod = jnp.dot(sc, w_ref[0], preferred_element_type=F32, precision=lax.Precision.HIGHEST)
    o_ref[0] = mod + b_ref[0] + jnp.where(j >= 1, 1.0, 0.0)


def ada_modulation_all(c, ada_w, ada_b):
    B, D = c.shape
    n = ada_w.shape[0] * ada_w.shape[1]
    rows = -(-B // SUBLANES) * SUBLANES
    cp = jnp.pad(c, ((0, rows - B), (0, 0)))
    w = ada_w.reshape(n, D, 3 * D)
    b = ada_b.reshape(n, 1, 3 * D)
    out = pl.pallas_call(
        _ada_kernel,
        grid=(n, 3),
        in_specs=[pl.BlockSpec((rows, D), lambda l, j: (0, 0)),
                  pl.BlockSpec((1, D, D), lambda l, j: (l, 0, j)),
                  pl.BlockSpec((1, 1, D), lambda l, j: (l, 0, j))],
        out_specs=pl.BlockSpec((1, rows, D), lambda l, j: (l, 0, j)),
        out_shape=jax.ShapeDtypeStruct((n, rows, 3 * D), F32),
        compiler_params=_cparams(("arbitrary", "arbitrary")),
        name="ada_modulation",
    )(cp, w, b)
    return out[:, :B]


def _inproj_kernel(x_ref, sc_ref, sh_ref, w_ref, *o_refs, splits):
    h = (x_ref[0] * sc_ref[0] + sh_ref[0]).astype(BF16)
    for o_ref, (start, width) in zip(o_refs, splits):
        o_ref[0] = jnp.dot(h, w_ref[:, start:start + width],
                           preferred_element_type=F32).astype(o_ref.dtype)


def in_projection(x, scale, shift, w, groups, tm=512):
    B, S, D = x.shape
    splits, start = [], 0
    for width, _ in groups:
        splits.append((start, width))
        start += width
    assert start == w.shape[1] and S % tm == 0
    return pl.pallas_call(
        functools.partial(_inproj_kernel, splits=tuple(splits)),
        grid=(B, S // tm),
        in_specs=[pl.BlockSpec((1, tm, D), lambda b, i: (b, i, 0)),
                  pl.BlockSpec((1, 1, D), lambda b, i: (b, 0, 0)),
                  pl.BlockSpec((1, 1, D), lambda b, i: (b, 0, 0)),
                  pl.BlockSpec(w.shape, lambda b, i: (0, 0))],
        out_specs=[pl.BlockSpec((1, tm, width), lambda b, i: (b, i, 0)) for width, _ in groups],
        out_shape=[jax.ShapeDtypeStruct((B, S, width), dt) for width, dt in groups],
        compiler_params=_cparams(("parallel", "parallel")),
        name="in_projection",
    )(x, scale, shift, w)


def _band_kernel(sink_ref, q_ref, kp_ref, ko_ref, vp_ref, vo_ref, bias_ref, o_ref, lse_ref, *,
                 hq, hk, use_sinks):
    b = pl.program_id(1)
    q = q_ref[0]
    kk = jnp.concatenate([kp_ref[0], ko_ref[0]], axis=0)
    vv = jnp.concatenate([vp_ref[0], vo_ref[0]], axis=0)
    col = lax.broadcasted_iota(I32, (BLK, 2 * BLK), 1)
    first_mask = jnp.where(jnp.logical_and(b == 0, col < BLK), NEG, 0.0)
    group = hq // hk
    for h in range(hq):
        g = h // group
        qh = q[:, h * HEAD_DIM:(h + 1) * HEAD_DIM]
        kh = kk[:, g * HEAD_DIM:(g + 1) * HEAD_DIM]
        vh = vv[:, g * HEAD_DIM:(g + 1) * HEAD_DIM]
        s = lax.dot_general(qh, kh, (((1,), (1,)), ((), ())), preferred_element_type=F32)
        s = s + bias_ref[h] + first_mask
        m = jnp.max(s, axis=-1, keepdims=True)
        if use_sinks:
            m = jnp.maximum(m, sink_ref[h])
        p = jnp.exp(s - m)
        l = jnp.sum(p, axis=-1, keepdims=True)
        if use_sinks:
            l = l + jnp.exp(sink_ref[h] - m)
        o = jnp.dot(p.astype(BF16), vh, preferred_element_type=F32)
        o_ref[0, :, h * HEAD_DIM:(h + 1) * HEAD_DIM] = (o / l).astype(o_ref.dtype)
        lse_ref[0, :, h:h + 1] = m + jnp.log(l)


def banded_attention(q, k, v, bias, sinks, out_dtype):
    N, L, qc = q.shape
    hq = qc // HEAD_DIM
    hk = k.shape[2] // HEAD_DIM
    use_sinks = sinks is not None
    if sinks is None:
        sinks = jnp.zeros((hq,), F32)
    cur = lambda n, b, s: (n, b, 0)
    prev = lambda n, b, s: (n, jnp.maximum(b - 1, 0), 0)
    return pl.pallas_call(
        functools.partial(_band_kernel, hq=hq, hk=hk, use_sinks=use_sinks),
        grid_spec=pltpu.PrefetchScalarGridSpec(
            num_scalar_prefetch=1,
            grid=(N, L // BLK),
            in_specs=[pl.BlockSpec((1, BLK, qc), cur),
                      pl.BlockSpec((1, BLK, hk * HEAD_DIM), prev),
                      pl.BlockSpec((1, BLK, hk * HEAD_DIM), cur),
                      pl.BlockSpec((1, BLK, hk * HEAD_DIM), prev),
                      pl.BlockSpec((1, BLK, hk * HEAD_DIM), cur),
                      pl.BlockSpec(bias.shape, lambda n, b, s: (0, 0, 0))],
            out_specs=[pl.BlockSpec((1, BLK, qc), cur),
                       pl.BlockSpec((1, BLK, hq), cur)]),
        out_shape=[jax.ShapeDtypeStruct((N, L, qc), out_dtype),
                   jax.ShapeDtypeStruct((N, L, hq), F32)],
        compiler_params=_cparams(("parallel", "arbitrary")),
        name="banded_attention",
    )(sinks, q, k, k, v, v, bias)


def band_bias(table, dilation, max_dist):
    qi = np.arange(BLK)[:, None]
    kj = np.arange(2 * BLK)[None, :]
    dist = qi + BLK - kj
    allowed = (dist >= 0) & (dist <= max_dist)
    bias = table[_bucket_np(dist * dilation)]
    bias = jnp.where(allowed[:, :, None], bias, NEG)
    return jnp.transpose(bias, (2, 0, 1)).astype(F32)


def _combine_kernel(*refs, n_pat, heads):
    o_refs, l_refs, out_ref = refs[:n_pat], refs[n_pat:2 * n_pat], refs[2 * n_pat]
    lses = [r[0] for r in l_refs]
    m = functools.reduce(jnp.maximum, lses)
    es = [jnp.exp(l - m) for l in lses]
    tot = functools.reduce(lambda a, b: a + b, es)
    ws = [e / tot for e in es]
    for h in range(heads):
        sl = slice(h * HEAD_DIM, (h + 1) * HEAD_DIM)
        acc = ws[0][:, h:h + 1] * o_refs[0][0, :, sl]
        for p in range(1, n_pat):
            acc = acc + ws[p][:, h:h + 1] * o_refs[p][0, :, sl]
        out_ref[0, :, sl] = acc.astype(out_ref.dtype)


def combine_patterns(outs, lses, tm=512):
    B, S, C = outs[0].shape
    heads = C // HEAD_DIM
    n_pat = len(outs)
    idx = lambda b, i: (b, i, 0)
    return pl.pallas_call(
        functools.partial(_combine_kernel, n_pat=n_pat, heads=heads),
        grid=(B, S // tm),
        in_specs=[pl.BlockSpec((1, tm, C), idx)] * n_pat + [pl.BlockSpec((1, tm, heads), idx)] * n_pat,
        out_specs=pl.BlockSpec((1, tm, C), idx),
        out_shape=jax.ShapeDtypeStruct((B, S, C), BF16),
        compiler_params=_cparams(("parallel", "parallel")),
        name="combine_patterns",
    )(*outs, *lses)


def _outproj_kernel(*refs, n_parts):
    o_refs, w_refs = refs[:n_parts], refs[n_parts:2 * n_parts]
    x_ref, gate_ref, g_ref, b_ref, out_ref = refs[2 * n_parts:]
    mix = jnp.dot(o_refs[0][0], w_refs[0][...], preferred_element_type=F32)
    for o_ref, w_ref in zip(o_refs[1:], w_refs[1:]):
        mix = mix + jnp.dot(o_ref[0], w_ref[...], preferred_element_type=F32)
    z = ALPHA * x_ref[0] + gate_ref[0] * mix
    out_ref[0] = _layer_norm(z, g_ref[...], b_ref[...])


def out_projection_ln(parts, weights, x, gate, ln_g, ln_b, tm=512):
    B, S, D = x.shape
    idx = lambda b, i: (b, i, 0)
    const2 = lambda b, i: (0, 0)
    return pl.pallas_call(
        functools.partial(_outproj_kernel, n_parts=len(parts)),
        grid=(B, S // tm),
        in_specs=([pl.BlockSpec((1, tm, p.shape[2]), idx) for p in parts]
                  + [pl.BlockSpec(w.shape, const2) for w in weights]
                  + [pl.BlockSpec((1, tm, D), idx),
                     pl.BlockSpec((1, 1, D), lambda b, i: (b, 0, 0)),
                     pl.BlockSpec((1, D), const2),
                     pl.BlockSpec((1, D), const2)]),
        out_specs=pl.BlockSpec((1, tm, D), idx),
        out_shape=jax.ShapeDtypeStruct((B, S, D), F32),
        compiler_params=_cparams(("parallel", "parallel")),
        name="out_projection_ln",
    )(*parts, *weights, x, gate, ln_g.reshape(1, D), ln_b.reshape(1, D))


def _dsa_kernel(q_ref, qi_ref, wi_ref, kit_ref, kct_ref, va_ref, vb_ref, strip_ref, o_ref,
                sk_scr, hi_scr, lo_scr, qm_scr, qim_scr, wib_scr, mb_scr, s_scr, m_scr, acc_scr, *, k_sel):
    i = pl.program_id(1)
    ck = DSA_CK
    sub = ck // LANES
    gt = DSA_GK // LANES
    gc = DSA_GK // ck
    nbuf = DSA_SKEW + 1
    ngr = (i * BLK + BLK + DSA_GK - 1) // DSA_GK
    nck = ngr * gc
    lane = lax.broadcasted_iota(I32, (BLK, LANES), 1)
    upper = lane >= HEAD_DIM

    half = C_HEADS // 2
    hps = DSA_HPS
    rows = hps * BLK

    def slot_rows(slot):
        return slot // hps, slice((slot % hps) * BLK, (slot % hps + 1) * BLK)

    def slot_head(slot):
        return 2 * (slot % half) + slot // half

    for p in range(half):
        qp = q_ref[0, :, p * LANES:(p + 1) * LANES]
        k0, r0 = slot_rows(p)
        k1, r1 = slot_rows(half + p)
        qm_scr[k0, r0] = jnp.where(upper, jnp.zeros_like(qp), qp)
        qm_scr[k1, r1] = jnp.where(upper, qp, jnp.zeros_like(qp))
    for p in range(IDX_HEADS // 2):
        qp = qi_ref[0, :, p * LANES:(p + 1) * LANES]
        qim_scr[2 * p] = jnp.where(upper, jnp.zeros_like(qp), qp)
        qim_scr[2 * p + 1] = jnp.where(upper, qp, jnp.zeros_like(qp))
    wscale = IDX_HEADS ** -0.5 * IDX_DIM ** -0.5
    for h in range(IDX_HEADS):
        wib_scr[h] = jnp.broadcast_to(wi_ref[0, :, h:h + 1] * wscale, (BLK, LANES))

    row_t = lax.broadcasted_iota(I32, (BLK, ck), 0) + i * BLK
    col_l = lax.broadcasted_iota(I32, (BLK, ck), 1)

    def score_chunk(j, carry):
        kt = kit_ref[0, j]
        sc = jnp.zeros((BLK, ck), F32)
        for h in range(IDX_HEADS):
            d = jnp.dot(qim_scr[h], kt, preferred_element_type=F32)
            w = wib_scr[h]
            sc = sc + jnp.maximum(d, 0.0) * jnp.concatenate([w] * sub, axis=1)
        bits = pltpu.bitcast(sc, I32)
        key = bits ^ ((bits >> 31) & 0x7FFFFFFF)
        key = jnp.where(col_l + j * ck <= row_t, key, INT_MIN)
        for u in range(sub):
            tile = key[:, u * LANES:(u + 1) * LANES]
            sk_scr[j * sub + u] = tile
            tile_t = tile.T
            hi_scr[j * sub + u] = (tile_t >> 16).astype(I16)
            lo_scr[j * sub + u] = ((tile_t & 0xFFFF) - HALF).astype(I16)
        return carry

    lax.fori_loop(0, nck, score_chunk, 0)

    packed = 2 * SUBLANES

    def count(scr, cond):
        def group(g, acc):
            parts = []
            for u in range(gt):
                hit = jnp.where(cond(scr[g * gt + u]), jnp.int16(1), jnp.int16(0))
                parts += [hit[r:r + packed] for r in range(0, LANES, packed)]
            while len(parts) > 1:
                parts = [a + b for a, b in zip(parts[0::2], parts[1::2])]
            return acc + parts[0]
        acc = lax.fori_loop(0, ngr, group, jnp.zeros((packed, LANES), I16))
        return jnp.sum(acc.astype(I32), axis=0, keepdims=True)

    def select16(scr, target):
        def bit_pass(t, v):
            c = v | lax.shift_left(jnp.int32(1), 15 - t)
            c16 = (c - HALF).astype(I16)
            return jnp.where(count(scr, lambda x: x >= c16) >= target, c, v)
        return lax.fori_loop(0, 16, bit_pass, jnp.zeros((1, LANES), I32))

    v_hi = select16(hi_scr, k_sel)
    hi16 = (v_hi - HALF).astype(I16)
    above = count(hi_scr, lambda x: x > hi16)
    def keep_low(g, carry):
        for u in range(gt):
            idx = g * gt + u
            lo_scr[idx] = jnp.where(hi_scr[idx] == hi16, lo_scr[idx], jnp.int16(-HALF))
        return carry
    lax.fori_loop(0, ngr, keep_low, 0)
    v_lo = select16(lo_scr, k_sel - above)
    thr_q = jnp.maximum(lax.shift_left(v_hi - HALF, 16) | v_lo, INT_MIN + 1)
    thr = jnp.broadcast_to(thr_q, (BLK, LANES)).T

    m_scr[...] = jnp.full(m_scr.shape, NEG, F32)
    acc_scr[...] = jnp.zeros(acc_scr.shape, F32)

    def attend_group(g, carry):
        for u in range(gt):
            mb = jnp.where(sk_scr[g * gt + u] >= thr, 0.0, NEG).astype(BF16)
            for r in range(hps):
                mb_scr[u, r * BLK:(r + 1) * BLK] = mb
        tiles = [jnp.clip(i - (g * gt + u), 0, N_NEAR) for u in range(gt)]

        def logits(k):
            heads = [slot_head(k * hps + r) for r in range(hps)]
            mx = None
            for c in range(gc):
                s = jnp.dot(qm_scr[k], kct_ref[0, g * gc + c], preferred_element_type=F32).astype(BF16)
                for u in range(sub):
                    t = c * sub + u
                    bias = jnp.concatenate([strip_ref[h, tiles[t]] for h in heads], axis=0)
                    piece = s[:, u * LANES:(u + 1) * LANES] + mb_scr[t] + bias
                    s_scr[k % nbuf, t] = piece
                    mx = piece if mx is None else jnp.maximum(mx, piece)
            m_old = m_scr[k]
            m_new = jnp.maximum(m_old, jnp.max(mx.astype(F32), axis=1, keepdims=True))
            m_scr[k] = m_new
            return m_old, m_new

        def accumulate(k, m_old, m_new):
            m16 = m_new.astype(BF16)
            p = jnp.concatenate([jnp.exp(s_scr[k % nbuf, t] - m16) for t in range(gt)], axis=1)
            v_ref = va_ref if k * hps < half else vb_ref
            pv = jnp.dot(p, v_ref[0, g], preferred_element_type=F32)
            acc_scr[k] = jnp.exp(m_old - m_new) * acc_scr[k] + pv

        nstk = C_HEADS // hps
        stats = [logits(k) for k in range(DSA_SKEW)]
        for k in range(nstk):
            if k + DSA_SKEW < nstk:
                stats.append(logits(k + DSA_SKEW))
            accumulate(k, *stats[k])
        return carry

    lax.fori_loop(0, ngr, attend_group, 0)

    for p in range(half):
        k0, r0 = slot_rows(p)
        k1, r1 = slot_rows(half + p)
        a0 = acc_scr[k0, r0]
        a1 = acc_scr[k1, r1]
        o = jnp.where(upper, a1 / pltpu.roll(a1, HEAD_DIM, 1), a0 / pltpu.roll(a0, HEAD_DIM, 1))
        o_ref[0, :, p * LANES:(p + 1) * LANES] = o.astype(o_ref.dtype)


def dsa_attention(q, qi, wi, ki, kc, vc, table):
    B, S, _ = q.shape
    ck = DSA_CK
    assert S % DSA_GK == 0
    nchunks = S // ck
    k_sel = min(TOPK_MAX, S // 4)

    def chunked_t(t):
        tt = jnp.transpose(t.reshape(B, nchunks, ck, HEAD_DIM), (0, 1, 3, 2))
        return jnp.concatenate([tt, tt], axis=2)

    ngroups = S // DSA_GK
    ones = jnp.ones_like(vc)
    va = jnp.concatenate([vc, ones], axis=-1).reshape(B, ngroups, DSA_GK, LANES)
    vb = jnp.concatenate([ones, vc], axis=-1).reshape(B, ngroups, DSA_GK, LANES)
    j = np.arange(2 * BLK)
    offs = np.where(j <= BLK, -j, 2 * BLK - j)
    v = table[_bucket_np(np.arange(N_NEAR)[:, None] * BLK + offs[None, :])]
    v = jnp.transpose(v, (2, 0, 1)).astype(F32)
    near = jnp.tile(v, (1, 1, BLK))[..., :BLK * (2 * BLK - 1)]
    near = near.reshape(C_HEADS, N_NEAR, BLK, 2 * BLK - 1)[..., :BLK]
    far = jnp.broadcast_to(table[REL_BUCKETS - 1].astype(F32)[:, None, None, None],
                           (C_HEADS, 1, BLK, LANES))
    strip = jnp.concatenate([near, far], axis=1).astype(BF16)

    once = pl.Buffered(1)
    qblk = lambda b, i: (b, i, 0)
    per_b = lambda b, i: (b, 0, 0, 0)
    return pl.pallas_call(
        functools.partial(_dsa_kernel, k_sel=k_sel),
        grid=(B, S // BLK),
        in_specs=[pl.BlockSpec((1, BLK, C_HEADS * HEAD_DIM), qblk),
                  pl.BlockSpec((1, BLK, IDX_HEADS * IDX_DIM), qblk),
                  pl.BlockSpec((1, BLK, IDX_HEADS), qblk),
                  pl.BlockSpec((1, nchunks, LANES, ck), per_b, pipeline_mode=once),
                  pl.BlockSpec((1, nchunks, LANES, ck), per_b, pipeline_mode=once),
                  pl.BlockSpec((1, ngroups, DSA_GK, LANES), per_b, pipeline_mode=once),
                  pl.BlockSpec((1, ngroups, DSA_GK, LANES), per_b, pipeline_mode=once),
                  pl.BlockSpec(strip.shape, lambda b, i: (0, 0, 0, 0), pipeline_mode=once)],
        out_specs=pl.BlockSpec((1, BLK, C_HEADS * HEAD_DIM), qblk),
        scratch_shapes=[pltpu.VMEM((S // LANES, BLK, LANES), I32),
                        pltpu.VMEM((S // LANES, LANES, BLK), I16),
                        pltpu.VMEM((S // LANES, LANES, BLK), I16),
                        pltpu.VMEM((C_HEADS // DSA_HPS, DSA_HPS * BLK, LANES), BF16),
                        pltpu.VMEM((IDX_HEADS, BLK, LANES), BF16),
                        pltpu.VMEM((IDX_HEADS, BLK, LANES), F32),
                        pltpu.VMEM((DSA_GK // LANES, DSA_HPS * BLK, LANES), BF16),
                        pltpu.VMEM((DSA_SKEW + 1, DSA_GK // LANES, DSA_HPS * BLK, LANES), BF16),
                        pltpu.VMEM((C_HEADS // DSA_HPS, DSA_HPS * BLK, LANES), F32),
                        pltpu.VMEM((C_HEADS // DSA_HPS, DSA_HPS * BLK, LANES), F32)],
        out_shape=jax.ShapeDtypeStruct((B, S, C_HEADS * HEAD_DIM), BF16),
        compiler_params=_cparams(("parallel", "arbitrary")),
        name="dsa_attention",
    )(q, qi, wi, chunked_t(ki), chunked_t(kc), va, vb, strip)


def _ffn_step(x2d, w1_ref, w3_ref, w2_ref, acc):
    xb = x2d[...]
    h1 = jnp.dot(xb, w1_ref[0], preferred_element_type=F32)
    h3 = jnp.dot(xb, w3_ref[0], preferred_element_type=F32)
    a = (h1 * jax.nn.sigmoid(h1) * h3).astype(BF16)
    acc[...] += jnp.dot(a, w2_ref[0], preferred_element_type=F32)


def _ffn_dense_kernel(x_ref, sc_ref, sh_ref, gate_ref, g_ref, b_ref, w1_ref, w3_ref, w2_ref, out_ref,
                      x2d, acc, *, nf):
    f = pl.program_id(2)

    @pl.when(f == 0)
    def _():
        x2d[...] = (x_ref[0] * sc_ref[0] + sh_ref[0]).astype(BF16)
        acc[...] = jnp.zeros(acc.shape, F32)

    _ffn_step(x2d, w1_ref, w3_ref, w2_ref, acc)

    @pl.when(f == nf - 1)
    def _():
        z = ALPHA * x_ref[0] + gate_ref[0] * acc[...]
        out_ref[0] = _layer_norm(z, g_ref[...], b_ref[...])


def ffn_dense_ln(x, scale, shift, gate, ln_g, ln_b, w1, w3, w2, tm=512, tf=FFN_TF):
    B, S, D = x.shape
    nf = D_FF // tf
    xi = lambda b, i, f: (b, i, 0)
    bi = lambda b, i, f: (b, 0, 0)
    c2 = lambda b, i, f: (0, 0)
    return pl.pallas_call(
        functools.partial(_ffn_dense_kernel, nf=nf),
        grid=(B, S // tm, nf),
        in_specs=[pl.BlockSpec((1, tm, D), xi),
                  pl.BlockSpec((1, 1, D), bi), pl.BlockSpec((1, 1, D), bi), pl.BlockSpec((1, 1, D), bi),
                  pl.BlockSpec((1, D), c2), pl.BlockSpec((1, D), c2),
                  pl.BlockSpec((1, D, tf), lambda b, i, f: (0, 0, f)),
                  pl.BlockSpec((1, D, tf), lambda b, i, f: (0, 0, f)),
                  pl.BlockSpec((1, tf, D), lambda b, i, f: (0, f, 0))],
        out_specs=pl.BlockSpec((1, tm, D), xi),
        out_shape=jax.ShapeDtypeStruct((B, S, D), F32),
        scratch_shapes=[pltpu.VMEM((tm, D), BF16), pltpu.VMEM((tm, D), F32)],
        compiler_params=_cparams(("parallel", "parallel", "arbitrary")),
        name="ffn_dense_ln",
    )(x, scale, shift, gate, ln_g.reshape(1, D), ln_b.reshape(1, D), w1, w3, w2)


def _ffn_expert_kernel(be_ref, tok_ref, tok_next_ref, dst_ref, h_hbm, w1_ref, w3_ref, w2_ref, y_hbm,
                       xbuf, ybuf, x2d, acc, gsem, ssem, *, nf, tm):
    i = pl.program_id(0)
    f = pl.program_id(1)
    n_blocks = pl.num_programs(0)
    slot = i % 2
    rows = tm * D_TILES

    def start_gather(idx_ref, s):
        def body(r, carry):
            src = pl.multiple_of(idx_ref[0, 0, r] * D_TILES, D_TILES)
            dst = pl.multiple_of(r * D_TILES, D_TILES)
            pltpu.make_async_copy(h_hbm.at[pl.ds(src, D_TILES)], xbuf.at[s, pl.ds(dst, D_TILES)],
                                  gsem.at[s]).start()
            return carry
        lax.fori_loop(0, tm, body, 0, unroll=8)

    def wait_gather(s):
        pltpu.make_async_copy(h_hbm.at[pl.ds(0, rows)], xbuf.at[s], gsem.at[s]).wait()

    def start_scatter():
        def body(r, carry):
            src = pl.multiple_of(r * D_TILES, D_TILES)
            dst = pl.multiple_of(dst_ref[0, 0, r] * D_TILES, D_TILES)
            pltpu.make_async_copy(ybuf.at[pl.ds(src, D_TILES)], y_hbm.at[pl.ds(dst, D_TILES)], ssem).start()
            return carry
        lax.fori_loop(0, tm, body, 0, unroll=8)

    def wait_scatter():
        pltpu.make_async_copy(ybuf, y_hbm.at[pl.ds(0, rows)], ssem).wait()

    @pl.when(f == 0)
    def _():
        @pl.when(i == 0)
        def _():
            start_gather(tok_ref, 0)

        @pl.when(i + 1 < n_blocks)
        def _():
            start_gather(tok_next_ref, 1 - slot)

        wait_gather(slot)
        for j in range(D_TILES):
            x2d[:, j * LANES:(j + 1) * LANES] = xbuf[slot, pl.ds(j, tm, stride=D_TILES), :].astype(BF16)
        acc[...] = jnp.zeros(acc.shape, F32)

    _ffn_step(x2d, w1_ref, w3_ref, w2_ref, acc)

    @pl.when(f == nf - 1)
    def _():
        @pl.when(i > 0)
        def _():
            wait_scatter()

        for j in range(D_TILES):
            ybuf[pl.ds(j, tm, stride=D_TILES), :] = acc[:, j * LANES:(j + 1) * LANES]
        start_scatter()

        @pl.when(i == n_blocks - 1)
        def _():
            wait_scatter()


def ffn_experts(h_tiles, slot_tok, slot_dst, block_expert, w1, w3, w2, tf=FFN_TF):
    tm = MOE_BLOCK
    P = slot_tok.shape[0]
    n_blocks = P // tm
    nf = D_FF // tf
    tok3 = slot_tok.reshape(n_blocks, 1, tm)
    dst3 = slot_dst.reshape(n_blocks, 1, tm)
    cur = lambda i, f, be: (i, 0, 0)
    nxt = lambda i, f, be: (jnp.minimum(i + 1, n_blocks - 1), 0, 0)
    return pl.pallas_call(
        functools.partial(_ffn_expert_kernel, nf=nf, tm=tm),
        grid_spec=pltpu.PrefetchScalarGridSpec(
            num_scalar_prefetch=1,
            grid=(n_blocks, nf),
            in_specs=[pl.BlockSpec((1, 1, tm), cur, memory_space=pltpu.SMEM),
                      pl.BlockSpec((1, 1, tm), nxt, memory_space=pltpu.SMEM),
                      pl.BlockSpec((1, 1, tm), cur, memory_space=pltpu.SMEM),
                      pl.BlockSpec(memory_space=pl.ANY),
                      pl.BlockSpec((1, D_MODEL, tf), lambda i, f, be: (be[i], 0, f)),
                      pl.BlockSpec((1, D_MODEL, tf), lambda i, f, be: (be[i], 0, f)),
                      pl.BlockSpec((1, tf, D_MODEL), lambda i, f, be: (be[i], f, 0))],
            out_specs=pl.BlockSpec(memory_space=pl.ANY),
            scratch_shapes=[pltpu.VMEM((2, tm * D_TILES, LANES), F32),
                            pltpu.VMEM((tm * D_TILES, LANES), F32),
                            pltpu.VMEM((tm, D_MODEL), BF16),
                            pltpu.VMEM((tm, D_MODEL), F32),
                            pltpu.SemaphoreType.DMA((2,)),
                            pltpu.SemaphoreType.DMA(())]),
        out_shape=jax.ShapeDtypeStruct((P * D_TILES, LANES), F32),
        compiler_params=_cparams(("arbitrary", "arbitrary")),
        name="ffn_experts",
    )(block_expert, tok3, tok3, dst3, h_tiles, w1, w3, w2)


def _router_kernel(x_ref, sc_ref, sh_ref, wr_ref, h_ref, idx_ref, gate_ref, *, tm):
    h = x_ref[0] * sc_ref[0] + sh_ref[0]
    for j in range(D_TILES):
        h_ref[pl.ds(j, tm, stride=D_TILES), :] = h[:, j * LANES:(j + 1) * LANES]
    logits = jnp.dot(h, wr_ref[...], preferred_element_type=F32, precision=lax.Precision.HIGHEST)
    lane = lax.broadcasted_iota(I32, logits.shape, 1)
    lg = jnp.where(lane < N_EXPERTS, logits, -jnp.inf)
    m1 = jnp.max(lg, axis=1, keepdims=True)
    i1 = jnp.min(jnp.where(lg == m1, lane, LANES), axis=1, keepdims=True)
    lg2 = jnp.where(lane == i1, -jnp.inf, lg)
    m2 = jnp.max(lg2, axis=1, keepdims=True)
    i2 = jnp.min(jnp.where(lg2 == m2, lane, LANES), axis=1, keepdims=True)
    e = jnp.exp(m2 - m1)
    idx_ref[:, 0:1] = i1
    idx_ref[:, 1:2] = i2
    gate_ref[:, 0:1] = 1.0 / (1.0 + e)
    gate_ref[:, 1:2] = e / (1.0 + e)


def route_tokens(x, scale, shift, w_router, tm=512):
    B, S, D = x.shape
    T = B * S
    nb = S // tm
    wr = jnp.pad(w_router, ((0, 0), (0, LANES - N_EXPERTS)))
    return pl.pallas_call(
        functools.partial(_router_kernel, tm=tm),
        grid=(B, nb),
        in_specs=[pl.BlockSpec((1, tm, D), lambda b, i: (b, i, 0)),
                  pl.BlockSpec((1, 1, D), lambda b, i: (b, 0, 0)),
                  pl.BlockSpec((1, 1, D), lambda b, i: (b, 0, 0)),
                  pl.BlockSpec((D, LANES), lambda b, i: (0, 0))],
        out_specs=[pl.BlockSpec((tm * D_TILES, LANES), lambda b, i: (b * nb + i, 0)),
                   pl.BlockSpec((tm, TOP_K), lambda b, i: (b * nb + i, 0)),
                   pl.BlockSpec((tm, TOP_K), lambda b, i: (b * nb + i, 0))],
        out_shape=[jax.ShapeDtypeStruct((T * D_TILES, LANES), F32),
                   jax.ShapeDtypeStruct((T, TOP_K), I32),
                   jax.ShapeDtypeStruct((T, TOP_K), F32)],
        compiler_params=_cparams(("parallel", "parallel")),
        name="route_tokens",
    )(x, scale, shift, wr)


def _moe_combine_kernel(y_ref, gt_ref, x_ref, gate_ref, g_ref, b_ref, out_ref, *, tm):
    g0 = gt_ref[:, 0:1]
    g1 = gt_ref[:, 1:2]
    stride = TOP_K * D_TILES
    pieces = []
    for j in range(D_TILES):
        sl = slice(j * LANES, (j + 1) * LANES)
        ff = g0 * y_ref[pl.ds(j, tm, stride=stride), :] + g1 * y_ref[pl.ds(D_TILES + j, tm, stride=stride), :]
        pieces.append(ALPHA * x_ref[0, :, sl] + gate_ref[0, :, sl] * ff)
    z = jnp.concatenate(pieces, axis=1)
    out_ref[0] = _layer_norm(z, g_ref[...], b_ref[...])


def moe_combine_ln(yg, gates, x, gate, ln_g, ln_b, tm=512):
    B, S, D = x.shape
    nb = S // tm
    return pl.pallas_call(
        functools.partial(_moe_combine_kernel, tm=tm),
        grid=(B, nb),
        in_specs=[pl.BlockSpec((tm * TOP_K * D_TILES, LANES), lambda b, i: (b * nb + i, 0)),
                  pl.BlockSpec((tm, TOP_K), lambda b, i: (b * nb + i, 0)),
                  pl.BlockSpec((1, tm, D), lambda b, i: (b, i, 0)),
                  pl.BlockSpec((1, 1, D), lambda b, i: (b, 0, 0)),
                  pl.BlockSpec((1, D), lambda b, i: (0, 0)),
                  pl.BlockSpec((1, D), lambda b, i: (0, 0))],
        out_specs=pl.BlockSpec((1, tm, D), lambda b, i: (b, i, 0)),
        out_shape=jax.ShapeDtypeStruct((B, S, D), F32),
        compiler_params=_cparams(("parallel", "parallel")),
        name="moe_combine_ln",
    )(yg, gates, x, gate, ln_g.reshape(1, D), ln_b.reshape(1, D))


def _scale_cols(w, start, width, factor):
    return w.at[:, start:start + width].multiply(factor)


def even_mixer_layer(x, mod, w_in, w_out, rel_table, sinks, ln_g, ln_b):
    B, S, D = x.shape
    assert S % A_PAD == 0
    shift, scale, gate = mod
    ah, bh, bk = A_HEADS * HEAD_DIM, B_HEADS * HEAD_DIM, B_KV_HEADS * HEAD_DIM
    w = _scale_cols(w_in, 0, ah, HEAD_DIM ** -0.5)
    w = _scale_cols(w, 3 * ah, bh, HEAD_DIM ** -0.5).astype(BF16)
    groups = [(ah, BF16)] * 3 + [(bh, BF16), (bk, BF16), (bk, BF16)]
    qa, ka, va, qb, kb, vb = in_projection(x, scale, shift, w, groups)

    outs, lses = [], []
    for window, d in A_PATTERNS:
        ld = S // d

        def regroup(t):
            return jnp.transpose(t.reshape(B, ld, d, ah), (0, 2, 1, 3)).reshape(B * d, ld, ah)

        bias = band_bias(rel_table[:, :A_HEADS], d, window // d)
        o, lse = banded_attention(regroup(qa), regroup(ka), regroup(va), bias, None, F32)
        outs.append(jnp.transpose(o.reshape(B, d, ld, ah), (0, 2, 1, 3)).reshape(B, S, ah))
        lses.append(jnp.transpose(lse.reshape(B, d, ld, A_HEADS), (0, 2, 1, 3)).reshape(B, S, A_HEADS))
    oa = combine_patterns(outs, lses)

    bias_b = band_bias(rel_table[:, A_HEADS:A_HEADS + B_HEADS], 1, B_WINDOW - 1)
    ob, _ = banded_attention(qb, kb, vb, bias_b, sinks.astype(F32), BF16)

    wo = w_out.astype(BF16)
    return out_projection_ln([oa, ob], [wo[:ah], wo[ah:]], x, gate, ln_g, ln_b)


def dsa_mixer_layer(x, mod, w_in, w_out, rel_table, ln_g, ln_b):
    shift, scale, gate = mod
    qw = C_HEADS * HEAD_DIM
    iw = IDX_HEADS * IDX_DIM
    q_w, kc_w, vc_w, qi_w, ki_w, wi_w = jnp.split(
        w_in, [qw, qw + HEAD_DIM, qw + 2 * HEAD_DIM, qw + 2 * HEAD_DIM + iw, qw + 2 * HEAD_DIM + iw + IDX_DIM],
        axis=1)
    pad = jnp.zeros((D_MODEL, LANES - IDX_DIM - IDX_HEADS), w_in.dtype)
    w = jnp.concatenate([q_w * HEAD_DIM ** -0.5, kc_w, vc_w, qi_w, ki_w, wi_w, pad], axis=1).astype(BF16)
    groups = [(qw, BF16), (2 * HEAD_DIM, BF16), (iw, BF16), (LANES, F32)]
    q, kv, qi, kw = in_projection(x, scale, shift, w, groups)
    kc, vc = kv[..., :HEAD_DIM], kv[..., HEAD_DIM:]
    ki = kw[..., :IDX_DIM].astype(BF16)
    wi = kw[..., IDX_DIM:IDX_DIM + IDX_HEADS]
    o = dsa_attention(q, qi, wi, ki, kc, vc, rel_table[:, :C_HEADS])
    return out_projection_ln([o], [w_out.astype(BF16)], x, gate, ln_g, ln_b)


def moe_layer(x, mod, w_router, w1, w3, w2, ln_g, ln_b):
    B, S, D = x.shape
    shift, scale, gate = mod
    T = B * S
    A = T * TOP_K
    h_tiles, top_idx, gates = route_tokens(x, scale, shift, w_router)

    e_flat = top_idx.reshape(-1)
    onehot = (e_flat[:, None] == jnp.arange(N_EXPERTS)[None, :]).astype(I32)
    rank = jnp.take_along_axis(jnp.cumsum(onehot, axis=0) - onehot, e_flat[:, None], axis=1)[:, 0]
    counts = jnp.sum(onehot, axis=0)
    padded = (counts + MOE_BLOCK - 1) // MOE_BLOCK * MOE_BLOCK
    pends = jnp.cumsum(padded)
    dest = (pends - padded)[e_flat] + rank
    n_blocks = -(-A // MOE_BLOCK) + N_EXPERTS
    P = n_blocks * MOE_BLOCK
    slot_src = jnp.full((P,), -1, I32).at[dest].set(jnp.arange(A, dtype=I32))
    is_pad = slot_src < 0
    slot_tok = jnp.where(is_pad, 0, slot_src // TOP_K)
    slot_dst = jnp.where(is_pad, A - 1 + jnp.cumsum(is_pad.astype(I32)), slot_src)
    block_expert = jnp.minimum(
        jnp.searchsorted(pends, jnp.arange(n_blocks) * MOE_BLOCK, side='right'), N_EXPERTS - 1).astype(I32)

    yg = ffn_experts(h_tiles, slot_tok, slot_dst, block_expert,
                     w1.astype(BF16), w3.astype(BF16), w2.astype(BF16))
    return moe_combine_ln(yg, gates, x, gate, ln_g, ln_b)


def kernel(x, c, rel_table, w_in_even, w_out_even, sinks, w_in_odd, w_out_odd, ffn_w1, ffn_w3, ffn_w2,
           router, exp_w1, exp_w3, exp_w2, ada_w, ada_b, ln_g, ln_b):
    D = D_MODEL
    mods = ada_modulation_all(c, ada_w, ada_b)

    def mod(layer, sub):
        m = mods[2 * layer + sub]
        return m[:, None, :D], m[:, None, D:2 * D], m[:, None, 2 * D:]

    for layer in range(DEPTH):
        i = layer // 2
        if layer % 2 == 0:
            x = even_mixer_layer(x, mod(layer, 0), w_in_even[i], w_out_even[i], rel_table, sinks[i],
                                 ln_g[layer, 0], ln_b[layer, 0])
            shift, scale, gate = mod(layer, 1)
            x = ffn_dense_ln(x, scale, shift, gate, ln_g[layer, 1], ln_b[layer, 1],
                             ffn_w1[i][None].astype(BF16), ffn_w3[i][None].astype(BF16),
                             ffn_w2[i][None].astype(BF16))
        else:
            x = dsa_mixer_layer(x, mod(layer, 0), w_in_odd[i], w_out_odd[i], rel_table,
                                ln_g[layer, 0], ln_b[layer, 0])
            x = moe_layer(x, mod(layer, 1), router[i], exp_w1[i], exp_w3[i], exp_w2[i],
                          ln_g[layer, 1], ln_b[layer, 1])
    return x
```

```python
import functools
import math

import numpy as np
import jax
import jax.numpy as jnp
from jax import lax
from jax.experimental import pallas as pl
from jax.experimental.pallas import tpu as pltpu

F32 = jnp.float32
BF16 = jnp.bfloat16
I32 = jnp.int32
I16 = jnp.int16

D_MODEL = 1024
HEAD_DIM = 64
BLK = 128
A_HEADS = 8
A_PATTERNS = ((128, 1), (512, 4), (2048, 16))
A_PAD = BLK * 16
B_HEADS = 8
B_KV_HEADS = 2
B_WINDOW = 128
C_HEADS = 16
IDX_HEADS = 8
IDX_DIM = 64
TOPK_MAX = 256
REL_BUCKETS = 32
REL_MAX_DIST = 2048
D_FF = 3584
N_EXPERTS = 8
TOP_K = 2
MOE_BLOCK = 512
DEPTH = 4
ALPHA = (2 * DEPTH) ** 0.25
LN_EPS = 1e-5

LANES = 128
SUBLANES = 8
D_TILES = D_MODEL // LANES
NEG = -(2.0 ** 100)
INT_MIN = -(2 ** 31)
HALF = 2 ** 15
VMEM_LIMIT = 56 * 1024 * 1024

FFN_TF = 1792
DSA_CK = 256
DSA_GK = 512
DSA_HPS = 4
DSA_SKEW = 2


def _bucket_np(dist):
    n = np.maximum(dist, 0)
    max_exact = REL_BUCKETS // 2
    nf = np.maximum(n, 1).astype(np.float32)
    large = max_exact + (np.log(nf / np.float32(max_exact)) / np.float32(math.log(REL_MAX_DIST / max_exact))
                         * np.float32(REL_BUCKETS - max_exact)).astype(np.int32)
    large = np.minimum(large, REL_BUCKETS - 1)
    return np.where(n < max_exact, n, large).astype(np.int32)


def _far_distance():
    b = _bucket_np(np.arange(0, 2 * REL_MAX_DIST))
    return int(np.max(np.nonzero(b != REL_BUCKETS - 1)[0])) + 1


FAR_DIST = _far_distance()
N_NEAR = -(-(FAR_DIST + BLK - 1) // BLK)


def _cparams(sem):
    return pltpu.CompilerParams(dimension_semantics=sem, vmem_limit_bytes=VMEM_LIMIT)


def _layer_norm(z, g, b):
    mu = jnp.mean(z, axis=-1, keepdims=True)
    zc = z - mu
    var = jnp.mean(zc * zc, axis=-1, keepdims=True)
    return zc * lax.rsqrt(var + LN_EPS) * g + b


def _ada_kernel(c_ref, w_ref, b_ref, o_ref):
    j = pl.program_id(1)
    c = c_ref[...]
    sc = c * jax.nn.sigmoid(c)
    mod = jnp.dot(sc, w_ref[0], preferred_element_type=F32, precision=lax.Precision.HIGHEST)
    o_ref[0] = mod + b_ref[0] + jnp.where(j >= 1, 1.0, 0.0)


def ada_modulation_all(c, ada_w, ada_b):
    B, D = c.shape
    n = ada_w.shape[0] * ada_w.shape[1]
    rows = -(-B // SUBLANES) * SUBLANES
    cp = jnp.pad(c, ((0, rows - B), (0, 0)))
    w = ada_w.reshape(n, D, 3 * D)
    b = ada_b.reshape(n, 1, 3 * D)
    out = pl.pallas_call(
        _ada_kernel,
        grid=(n, 3),
        in_specs=[pl.BlockSpec((rows, D), lambda l, j: (0, 0)),
                  pl.BlockSpec((1, D, D), lambda l, j: (l, 0, j)),
                  pl.BlockSpec((1, 1, D), lambda l, j: (l, 0, j))],
        out_specs=pl.BlockSpec((1, rows, D), lambda l, j: (l, 0, j)),
        out_shape=jax.ShapeDtypeStruct((n, rows, 3 * D), F32),
        compiler_params=_cparams(("arbitrary", "arbitrary")),
        name="ada_modulation",
    )(cp, w, b)
    return out[:, :B]


def _inproj_kernel(x_ref, sc_ref, sh_ref, w_ref, *o_refs, splits):
    h = (x_ref[0] * sc_ref[0] + sh_ref[0]).astype(BF16)
    for o_ref, (start, width) in zip(o_refs, splits):
        o_ref[0] = jnp.dot(h, w_ref[:, start:start + width],
                           preferred_element_type=F32).astype(o_ref.dtype)


def in_projection(x, scale, shift, w, groups, tm=512):
    B, S, D = x.shape
    splits, start = [], 0
    for width, _ in groups:
        splits.append((start, width))
        start += width
    assert start == w.shape[1] and S % tm == 0
    return pl.pallas_call(
        functools.partial(_inproj_kernel, splits=tuple(splits)),
        grid=(B, S // tm),
        in_specs=[pl.BlockSpec((1, tm, D), lambda b, i: (b, i, 0)),
                  pl.BlockSpec((1, 1, D), lambda b, i: (b, 0, 0)),
                  pl.BlockSpec((1, 1, D), lambda b, i: (b, 0, 0)),
                  pl.BlockSpec(w.shape, lambda b, i: (0, 0))],
        out_specs=[pl.BlockSpec((1, tm, width), lambda b, i: (b, i, 0)) for width, _ in groups],
        out_shape=[jax.ShapeDtypeStruct((B, S, width), dt) for width, dt in groups],
        compiler_params=_cparams(("parallel", "parallel")),
        name="in_projection",
    )(x, scale, shift, w)


def _band_kernel(sink_ref, q_ref, kp_ref, ko_ref, vp_ref, vo_ref, bias_ref, o_ref, lse_ref, *,
                 hq, hk, use_sinks):
    b = pl.program_id(2)
    q = q_ref[0]
    kk = jnp.concatenate([kp_ref[0], ko_ref[0]], axis=0)
    vv = jnp.concatenate([vp_ref[0], vo_ref[0]], axis=0)
    col = lax.broadcasted_iota(I32, (BLK, 2 * BLK), 1)
    first_mask = jnp.where(jnp.logical_and(b == 0, col < BLK), NEG, 0.0)
    group = hq // hk
    for h in range(hq):
        g = h // group
        qh = q[:, h * HEAD_DIM:(h + 1) * HEAD_DIM]
        kh = kk[:, g * HEAD_DIM:(g + 1) * HEAD_DIM]
        vh = vv[:, g * HEAD_DIM:(g + 1) * HEAD_DIM]
        s = lax.dot_general(qh, kh, (((1,), (1,)), ((), ())), preferred_element_type=F32)
        s = s + bias_ref[h] + first_mask
        m = jnp.max(s, axis=-1, keepdims=True)
        if use_sinks:
            m = jnp.maximum(m, sink_ref[h])
        p = jnp.exp(s - m)
        l = jnp.sum(p, axis=-1, keepdims=True)
        if use_sinks:
            l = l + jnp.exp(sink_ref[h] - m)
        o = jnp.dot(p.astype(BF16), vh, preferred_element_type=F32)
        o_ref[0, :, h * HEAD_DIM:(h + 1) * HEAD_DIM] = (o / l).astype(o_ref.dtype)
        lse_ref[0, 0, :, h:h + 1] = m + jnp.log(l)


def banded_attention(q, k, v, bias, sinks, out_dtype, dil=1):
    B, S, qc = q.shape
    kc = k.shape[2]
    hq = qc // HEAD_DIM
    hk = kc // HEAD_DIM
    ld = S // dil
    use_sinks = sinks is not None
    if sinks is None:
        sinks = jnp.zeros((hq,), F32)
    view = lambda t: t.reshape(B, ld, dil * t.shape[2])
    cur = lambda n, r, b, s: (n, b, r)
    prev = lambda n, r, b, s: (n, jnp.maximum(b - 1, 0), r)
    o, lse = pl.pallas_call(
        functools.partial(_band_kernel, hq=hq, hk=hk, use_sinks=use_sinks),
        grid_spec=pltpu.PrefetchScalarGridSpec(
            num_scalar_prefetch=1,
            grid=(B, dil, ld // BLK),
            in_specs=[pl.BlockSpec((1, BLK, qc), cur),
                      pl.BlockSpec((1, BLK, kc), prev),
                      pl.BlockSpec((1, BLK, kc), cur),
                      pl.BlockSpec((1, BLK, kc), prev),
                      pl.BlockSpec((1, BLK, kc), cur),
                      pl.BlockSpec(bias.shape, lambda n, r, b, s: (0, 0, 0))],
            out_specs=[pl.BlockSpec((1, BLK, qc), cur),
                       pl.BlockSpec((1, 1, BLK, hq), lambda n, r, b, s: (n, r, b, 0))]),
        out_shape=[jax.ShapeDtypeStruct((B, ld, dil * qc), out_dtype),
                   jax.ShapeDtypeStruct((B, dil, ld, hq), F32)],
        compiler_params=_cparams(("parallel", "parallel", "arbitrary")),
        name="banded_attention",
    )(sinks, view(q), view(k), view(k), view(v), view(v), bias)
    return o.reshape(B, S, qc), lse


def band_bias(table, dilation, max_dist):
    qi = np.arange(BLK)[:, None]
    kj = np.arange(2 * BLK)[None, :]
    dist = qi + BLK - kj
    allowed = (dist >= 0) & (dist <= max_dist)
    bias = table[_bucket_np(dist * dilation)]
    bias = jnp.where(allowed[:, :, None], bias, NEG)
    return jnp.transpose(bias, (2, 0, 1)).astype(F32)


def _combine_kernel(*refs, n_pat, heads):
    o_refs, l_refs, out_ref = refs[:n_pat], refs[n_pat:2 * n_pat], refs[2 * n_pat]
    lses = [r[0] for r in l_refs]
    m = functools.reduce(jnp.maximum, lses)
    es = [jnp.exp(l - m) for l in lses]
    tot = functools.reduce(lambda a, b: a + b, es)
    ws = [e / tot for e in es]
    for h in range(heads):
        sl = slice(h * HEAD_DIM, (h + 1) * HEAD_DIM)
        acc = ws[0][:, h:h + 1] * o_refs[0][0, :, sl]
        for p in range(1, n_pat):
            acc = acc + ws[p][:, h:h + 1] * o_refs[p][0, :, sl]
        out_ref[0, :, sl] = acc.astype(out_ref.dtype)


def combine_patterns(outs, lses, tm=512):
    B, S, C = outs[0].shape
    heads = C // HEAD_DIM
    n_pat = len(outs)
    idx = lambda b, i: (b, i, 0)
    return pl.pallas_call(
        functools.partial(_combine_kernel, n_pat=n_pat, heads=heads),
        grid=(B, S // tm),
        in_specs=[pl.BlockSpec((1, tm, C), idx)] * n_pat + [pl.BlockSpec((1, tm, heads), idx)] * n_pat,
        out_specs=pl.BlockSpec((1, tm, C), idx),
        out_shape=jax.ShapeDtypeStruct((B, S, C), BF16),
        compiler_params=_cparams(("parallel", "parallel")),
        name="combine_patterns",
    )(*outs, *lses)


def _outproj_kernel(*refs, n_parts):
    o_refs, w_refs = refs[:n_parts], refs[n_parts:2 * n_parts]
    x_ref, gate_ref, g_ref, b_ref, out_ref = refs[2 * n_parts:]
    mix = jnp.dot(o_refs[0][0], w_refs[0][...], preferred_element_type=F32)
    for o_ref, w_ref in zip(o_refs[1:], w_refs[1:]):
        mix = mix + jnp.dot(o_ref[0], w_ref[...], preferred_element_type=F32)
    z = ALPHA * x_ref[0] + gate_ref[0] * mix
    out_ref[0] = _layer_norm(z, g_ref[...], b_ref[...])


def out_projection_ln(parts, weights, x, gate, ln_g, ln_b, tm=512):
    B, S, D = x.shape
    idx = lambda b, i: (b, i, 0)
    const2 = lambda b, i: (0, 0)
    return pl.pallas_call(
        functools.partial(_outproj_kernel, n_parts=len(parts)),
        grid=(B, S // tm),
        in_specs=([pl.BlockSpec((1, tm, p.shape[2]), idx) for p in parts]
                  + [pl.BlockSpec(w.shape, const2) for w in weights]
                  + [pl.BlockSpec((1, tm, D), idx),
                     pl.BlockSpec((1, 1, D), lambda b, i: (b, 0, 0)),
                     pl.BlockSpec((1, D), const2),
                     pl.BlockSpec((1, D), const2)]),
        out_specs=pl.BlockSpec((1, tm, D), idx),
        out_shape=jax.ShapeDtypeStruct((B, S, D), F32),
        compiler_params=_cparams(("parallel", "parallel")),
        name="out_projection_ln",
    )(*parts, *weights, x, gate, ln_g.reshape(1, D), ln_b.reshape(1, D))


def _dsa_kernel(q_ref, qi_ref, wi_ref, kit_ref, kct_ref, va_ref, vb_ref, strip_ref, o_ref,
                sk_scr, hi_scr, lo_scr, qm_scr, qim_scr, wib_scr, mb_scr, s_scr, m_scr, acc_scr, *, k_sel):
    i = pl.program_id(1)
    ck = DSA_CK
    sub = ck // LANES
    gt = DSA_GK // LANES
    gc = DSA_GK // ck
    nbuf = DSA_SKEW + 1
    ngr = (i * BLK + BLK + DSA_GK - 1) // DSA_GK
    nck = ngr * gc
    lane = lax.broadcasted_iota(I32, (BLK, LANES), 1)
    upper = lane >= HEAD_DIM

    half = C_HEADS // 2
    hps = DSA_HPS
    rows = hps * BLK

    def slot_rows(slot):
        return slot // hps, slice((slot % hps) * BLK, (slot % hps + 1) * BLK)

    def slot_head(slot):
        return 2 * (slot % half) + slot // half

    for p in range(half):
        qp = q_ref[0, :, p * LANES:(p + 1) * LANES]
        k0, r0 = slot_rows(p)
        k1, r1 = slot_rows(half + p)
        qm_scr[k0, r0] = jnp.where(upper, jnp.zeros_like(qp), qp)
        qm_scr[k1, r1] = jnp.where(upper, qp, jnp.zeros_like(qp))
    for p in range(IDX_HEADS // 2):
        qp = qi_ref[0, :, p * LANES:(p + 1) * LANES]
        qim_scr[2 * p] = jnp.where(upper, jnp.zeros_like(qp), qp)
        qim_scr[2 * p + 1] = jnp.where(upper, qp, jnp.zeros_like(qp))
    wscale = IDX_HEADS ** -0.5 * IDX_DIM ** -0.5
    for h in range(IDX_HEADS):
        wib_scr[h] = jnp.broadcast_to(wi_ref[0, :, h:h + 1] * wscale, (BLK, LANES))

    row_t = lax.broadcasted_iota(I32, (BLK, ck), 0) + i * BLK
    col_l = lax.broadcasted_iota(I32, (BLK, ck), 1)

    def score_chunk(j, carry):
        kt = kit_ref[0, j]
        sc = jnp.zeros((BLK, ck), F32)
        for h in range(IDX_HEADS):
            d = jnp.dot(qim_scr[h], kt, preferred_element_type=F32)
            w = wib_scr[h]
            sc = sc + jnp.maximum(d, 0.0) * jnp.concatenate([w] * sub, axis=1)
        bits = pltpu.bitcast(sc, I32)
        key = bits ^ ((bits >> 31) & 0x7FFFFFFF)
        key = jnp.where(col_l + j * ck <= row_t, key, INT_MIN)
        for u in range(sub):
            tile = key[:, u * LANES:(u + 1) * LANES]
            sk_scr[j * sub + u] = tile
            tile_t = tile.T
            hi_scr[j * sub + u] = (tile_t >> 16).astype(I16)
            lo_scr[j * sub + u] = ((tile_t & 0xFFFF) - HALF).astype(I16)
        return carry

    lax.fori_loop(0, nck, score_chunk, 0)

    packed = 2 * SUBLANES

    def count(scr, cond):
        def group(g, acc):
            parts = []
            for u in range(gt):
                hit = jnp.where(cond(scr[g * gt + u]), jnp.int16(1), jnp.int16(0))
                parts += [hit[r:r + packed] for r in range(0, LANES, packed)]
            while len(parts) > 1:
                parts = [a + b for a, b in zip(parts[0::2], parts[1::2])]
            return acc + parts[0]
        acc = lax.fori_loop(0, ngr, group, jnp.zeros((packed, LANES), I16))
        return jnp.sum(acc.astype(I32), axis=0, keepdims=True)

    def select16(scr, target):
        def bit_pass(t, v):
            c = v | lax.shift_left(jnp.int32(1), 15 - t)
            c16 = (c - HALF).astype(I16)
            return jnp.where(count(scr, lambda x: x >= c16) >= target, c, v)
        return lax.fori_loop(0, 16, bit_pass, jnp.zeros((1, LANES), I32))

    v_hi = select16(hi_scr, k_sel)
    hi16 = (v_hi - HALF).astype(I16)
    above = count(hi_scr, lambda x: x > hi16)
    def keep_low(g, carry):
        for u in range(gt):
            idx = g * gt + u
            lo_scr[idx] = jnp.where(hi_scr[idx] == hi16, lo_scr[idx], jnp.int16(-HALF))
        return carry
    lax.fori_loop(0, ngr, keep_low, 0)
    v_lo = select16(lo_scr, k_sel - above)
    thr_q = jnp.maximum(lax.shift_left(v_hi - HALF, 16) | v_lo, INT_MIN + 1)
    thr = jnp.broadcast_to(thr_q, (BLK, LANES)).T

    m_scr[...] = jnp.full(m_scr.shape, NEG, F32)
    acc_scr[...] = jnp.zeros(acc_scr.shape, F32)

    def attend_group(g, carry):
        for u in range(gt):
            mb = jnp.where(sk_scr[g * gt + u] >= thr, 0.0, NEG).astype(BF16)
            for r in range(hps):
                mb_scr[u, r * BLK:(r + 1) * BLK] = mb
        tiles = [jnp.clip(i - (g * gt + u), 0, N_NEAR) for u in range(gt)]

        def logits(k):
            heads = [slot_head(k * hps + r) for r in range(hps)]
            mx = None
            for c in range(gc):
                s = jnp.dot(qm_scr[k], kct_ref[0, g * gc + c], preferred_element_type=F32).astype(BF16)
                for u in range(sub):
                    t = c * sub + u
                    bias = jnp.concatenate([strip_ref[h, tiles[t]] for h in heads], axis=0)
                    piece = s[:, u * LANES:(u + 1) * LANES] + mb_scr[t] + bias
                    s_scr[k % nbuf, t] = piece
                    mx = piece if mx is None else jnp.maximum(mx, piece)
            m_old = m_scr[k]
            m_new = jnp.maximum(m_old, jnp.max(mx.astype(F32), axis=1, keepdims=True))
            m_scr[k] = m_new
            return m_old, m_new

        def accumulate(k, m_old, m_new):
            m16 = m_new.astype(BF16)
            p = jnp.concatenate([jnp.exp(s_scr[k % nbuf, t] - m16) for t in range(gt)], axis=1)
            v_ref = va_ref if k * hps < half else vb_ref
            pv = jnp.dot(p, v_ref[0, g], preferred_element_type=F32)
            acc_scr[k] = jnp.exp(m_old - m_new) * acc_scr[k] + pv

        nstk = C_HEADS // hps
        stats = [logits(k) for k in range(DSA_SKEW)]
        for k in range(nstk):
            if k + DSA_SKEW < nstk:
                stats.append(logits(k + DSA_SKEW))
            accumulate(k, *stats[k])
        return carry

    lax.fori_loop(0, ngr, attend_group, 0)

    for p in range(half):
        k0, r0 = slot_rows(p)
        k1, r1 = slot_rows(half + p)
        a0 = acc_scr[k0, r0]
        a1 = acc_scr[k1, r1]
        o = jnp.where(upper, a1 / pltpu.roll(a1, HEAD_DIM, 1), a0 / pltpu.roll(a0, HEAD_DIM, 1))
        o_ref[0, :, p * LANES:(p + 1) * LANES] = o.astype(o_ref.dtype)


def dsa_attention(q, qi, wi, ki, kc, vc, table):
    B, S, _ = q.shape
    ck = DSA_CK
    assert S % DSA_GK == 0
    nchunks = S // ck
    k_sel = min(TOPK_MAX, S // 4)

    def chunked_t(t):
        tt = jnp.transpose(t.reshape(B, nchunks, ck, HEAD_DIM), (0, 1, 3, 2))
        return jnp.concatenate([tt, tt], axis=2)

    ngroups = S // DSA_GK
    ones = jnp.ones_like(vc)
    va = jnp.concatenate([vc, ones], axis=-1).reshape(B, ngroups, DSA_GK, LANES)
    vb = jnp.concatenate([ones, vc], axis=-1).reshape(B, ngroups, DSA_GK, LANES)
    j = np.arange(2 * BLK)
    offs = np.where(j <= BLK, -j, 2 * BLK - j)
    v = table[_bucket_np(np.arange(N_NEAR)[:, None] * BLK + offs[None, :])]
    v = jnp.transpose(v, (2, 0, 1)).astype(F32)
    near = jnp.tile(v, (1, 1, BLK))[..., :BLK * (2 * BLK - 1)]
    near = near.reshape(C_HEADS, N_NEAR, BLK, 2 * BLK - 1)[..., :BLK]
    far = jnp.broadcast_to(table[REL_BUCKETS - 1].astype(F32)[:, None, None, None],
                           (C_HEADS, 1, BLK, LANES))
    strip = jnp.concatenate([near, far], axis=1).astype(BF16)

    once = pl.Buffered(1)
    qblk = lambda b, i: (b, i, 0)
    per_b = lambda b, i: (b, 0, 0, 0)
    return pl.pallas_call(
        functools.partial(_dsa_kernel, k_sel=k_sel),
        grid=(B, S // BLK),
        in_specs=[pl.BlockSpec((1, BLK, C_HEADS * HEAD_DIM), qblk),
                  pl.BlockSpec((1, BLK, IDX_HEADS * IDX_DIM), qblk),
                  pl.BlockSpec((1, BLK, IDX_HEADS), qblk),
                  pl.BlockSpec((1, nchunks, LANES, ck), per_b, pipeline_mode=once),
                  pl.BlockSpec((1, nchunks, LANES, ck), per_b, pipeline_mode=once),
                  pl.BlockSpec((1, ngroups, DSA_GK, LANES), per_b, pipeline_mode=once),
                  pl.BlockSpec((1, ngroups, DSA_GK, LANES), per_b, pipeline_mode=once),
                  pl.BlockSpec(strip.shape, lambda b, i: (0, 0, 0, 0), pipeline_mode=once)],
        out_specs=pl.BlockSpec((1, BLK, C_HEADS * HEAD_DIM), qblk),
        scratch_shapes=[pltpu.VMEM((S // LANES, BLK, LANES), I32),
                        pltpu.VMEM((S // LANES, LANES, BLK), I16),
                        pltpu.VMEM((S // LANES, LANES, BLK), I16),
                        pltpu.VMEM((C_HEADS // DSA_HPS, DSA_HPS * BLK, LANES), BF16),
                        pltpu.VMEM((IDX_HEADS, BLK, LANES), BF16),
                        pltpu.VMEM((IDX_HEADS, BLK, LANES), F32),
                        pltpu.VMEM((DSA_GK // LANES, DSA_HPS * BLK, LANES), BF16),
                        pltpu.VMEM((DSA_SKEW + 1, DSA_GK // LANES, DSA_HPS * BLK, LANES), BF16),
                        pltpu.VMEM((C_HEADS // DSA_HPS, DSA_HPS * BLK, LANES), F32),
                        pltpu.VMEM((C_HEADS // DSA_HPS, DSA_HPS * BLK, LANES), F32)],
        out_shape=jax.ShapeDtypeStruct((B, S, C_HEADS * HEAD_DIM), BF16),
        compiler_params=_cparams(("parallel", "arbitrary")),
        name="dsa_attention",
    )(q, qi, wi, chunked_t(ki), chunked_t(kc), va, vb, strip)


def _ffn_step(x2d, w1_ref, w3_ref, w2_ref, acc):
    xb = x2d[...]
    h1 = jnp.dot(xb, w1_ref[0], preferred_element_type=F32)
    h3 = jnp.dot(xb, w3_ref[0], preferred_element_type=F32)
    a = (h1 * jax.nn.sigmoid(h1) * h3).astype(BF16)
    acc[...] += jnp.dot(a, w2_ref[0], preferred_element_type=F32)


def _ffn_dense_kernel(x_ref, sc_ref, sh_ref, gate_ref, g_ref, b_ref, w1_ref, w3_ref, w2_ref, out_ref,
                      x2d, acc, *, nf):
    f = pl.program_id(2)

    @pl.when(f == 0)
    def _():
        x2d[...] = (x_ref[0] * sc_ref[0] + sh_ref[0]).astype(BF16)
        acc[...] = jnp.zeros(acc.shape, F32)

    _ffn_step(x2d, w1_ref, w3_ref, w2_ref, acc)

    @pl.when(f == nf - 1)
    def _():
        z = ALPHA * x_ref[0] + gate_ref[0] * acc[...]
        out_ref[0] = _layer_norm(z, g_ref[...], b_ref[...])


def ffn_dense_ln(x, scale, shift, gate, ln_g, ln_b, w1, w3, w2, tm=512, tf=FFN_TF):
    B, S, D = x.shape
    nf = D_FF // tf
    xi = lambda b, i, f: (b, i, 0)
    bi = lambda b, i, f: (b, 0, 0)
    c2 = lambda b, i, f: (0, 0)
    return pl.pallas_call(
        functools.partial(_ffn_dense_kernel, nf=nf),
        grid=(B, S // tm, nf),
        in_specs=[pl.BlockSpec((1, tm, D), xi),
                  pl.BlockSpec((1, 1, D), bi), pl.BlockSpec((1, 1, D), bi), pl.BlockSpec((1, 1, D), bi),
                  pl.BlockSpec((1, D), c2), pl.BlockSpec((1, D), c2),
                  pl.BlockSpec((1, D, tf), lambda b, i, f: (0, 0, f)),
                  pl.BlockSpec((1, D, tf), lambda b, i, f: (0, 0, f)),
                  pl.BlockSpec((1, tf, D), lambda b, i, f: (0, f, 0))],
        out_specs=pl.BlockSpec((1, tm, D), xi),
        out_shape=jax.ShapeDtypeStruct((B, S, D), F32),
        scratch_shapes=[pltpu.VMEM((tm, D), BF16), pltpu.VMEM((tm, D), F32)],
        compiler_params=_cparams(("parallel", "parallel", "arbitrary")),
        name="ffn_dense_ln",
    )(x, scale, shift, gate, ln_g.reshape(1, D), ln_b.reshape(1, D), w1, w3, w2)


def _ffn_expert_kernel(be_ref, tok_ref, tok_next_ref, dst_ref, h_hbm, w1_ref, w3_ref, w2_ref, y_hbm,
                       xbuf, ybuf, x2d, acc, gsem, ssem, *, nf, tm):
    i = pl.program_id(0)
    f = pl.program_id(1)
    n_blocks = pl.num_programs(0)
    slot = i % 2
    rows = tm * D_TILES

    def start_gather(idx_ref, s):
        def body(r, carry):
            src = pl.multiple_of(idx_ref[0, 0, r] * D_TILES, D_TILES)
            dst = pl.multiple_of(r * D_TILES, D_TILES)
            pltpu.make_async_copy(h_hbm.at[pl.ds(src, D_TILES)], xbuf.at[s, pl.ds(dst, D_TILES)],
                                  gsem.at[s]).start()
            return carry
        lax.fori_loop(0, tm, body, 0, unroll=8)

    def wait_gather(s):
        pltpu.make_async_copy(h_hbm.at[pl.ds(0, rows)], xbuf.at[s], gsem.at[s]).wait()

    def start_scatter():
        def body(r, carry):
            src = pl.multiple_of(r * D_TILES, D_TILES)
            dst = pl.multiple_of(dst_ref[0, 0, r] * D_TILES, D_TILES)
            pltpu.make_async_copy(ybuf.at[pl.ds(src, D_TILES)], y_hbm.at[pl.ds(dst, D_TILES)], ssem).start()
            return carry
        lax.fori_loop(0, tm, body, 0, unroll=8)

    def wait_scatter():
        pltpu.make_async_copy(ybuf, y_hbm.at[pl.ds(0, rows)], ssem).wait()

    @pl.when(f == 0)
    def _():
        @pl.when(i == 0)
        def _():
            start_gather(tok_ref, 0)

        @pl.when(i + 1 < n_blocks)
        def _():
            start_gather(tok_next_ref, 1 - slot)

        wait_gather(slot)
        for j in range(D_TILES):
            x2d[:, j * LANES:(j + 1) * LANES] = xbuf[slot, pl.ds(j, tm, stride=D_TILES), :].astype(BF16)
        acc[...] = jnp.zeros(acc.shape, F32)

    _ffn_step(x2d, w1_ref, w3_ref, w2_ref, acc)

    @pl.when(f == nf - 1)
    def _():
        @pl.when(i > 0)
        def _():
            wait_scatter()

        for j in range(D_TILES):
            ybuf[pl.ds(j, tm, stride=D_TILES), :] = acc[:, j * LANES:(j + 1) * LANES]
        start_scatter()

        @pl.when(i == n_blocks - 1)
        def _():
            wait_scatter()


def ffn_experts(h_tiles, slot_tok, slot_dst, block_expert, w1, w3, w2, tf=FFN_TF):
    tm = MOE_BLOCK
    P = slot_tok.shape[0]
    n_blocks = P // tm
    nf = D_FF // tf
    tok3 = slot_tok.reshape(n_blocks, 1, tm)
    dst3 = slot_dst.reshape(n_blocks, 1, tm)
    cur = lambda i, f, be: (i, 0, 0)
    nxt = lambda i, f, be: (jnp.minimum(i + 1, n_blocks - 1), 0, 0)
    return pl.pallas_call(
        functools.partial(_ffn_expert_kernel, nf=nf, tm=tm),
        grid_spec=pltpu.PrefetchScalarGridSpec(
            num_scalar_prefetch=1,
            grid=(n_blocks, nf),
            in_specs=[pl.BlockSpec((1, 1, tm), cur, memory_space=pltpu.SMEM),
                      pl.BlockSpec((1, 1, tm), nxt, memory_space=pltpu.SMEM),
                      pl.BlockSpec((1, 1, tm), cur, memory_space=pltpu.SMEM),
                      pl.BlockSpec(memory_space=pl.ANY),
                      pl.BlockSpec((1, D_MODEL, tf), lambda i, f, be: (be[i], 0, f)),
                      pl.BlockSpec((1, D_MODEL, tf), lambda i, f, be: (be[i], 0, f)),
                      pl.BlockSpec((1, tf, D_MODEL), lambda i, f, be: (be[i], f, 0))],
            out_specs=pl.BlockSpec(memory_space=pl.ANY),
            scratch_shapes=[pltpu.VMEM((2, tm * D_TILES, LANES), F32),
                            pltpu.VMEM((tm * D_TILES, LANES), F32),
                            pltpu.VMEM((tm, D_MODEL), BF16),
                            pltpu.VMEM((tm, D_MODEL), F32),
                            pltpu.SemaphoreType.DMA((2,)),
                            pltpu.SemaphoreType.DMA(())]),
        out_shape=jax.ShapeDtypeStruct((P * D_TILES, LANES), F32),
        compiler_params=_cparams(("arbitrary", "arbitrary")),
        name="ffn_experts",
    )(block_expert, tok3, tok3, dst3, h_tiles, w1, w3, w2)


def _router_kernel(x_ref, sc_ref, sh_ref, wr_ref, h_ref, idx_ref, gate_ref, *, tm):
    h = x_ref[0] * sc_ref[0] + sh_ref[0]
    for j in range(D_TILES):
        h_ref[pl.ds(j, tm, stride=D_TILES), :] = h[:, j * LANES:(j + 1) * LANES]
    logits = jnp.dot(h, wr_ref[...], preferred_element_type=F32, precision=lax.Precision.HIGHEST)
    lane = lax.broadcasted_iota(I32, logits.shape, 1)
    lg = jnp.where(lane < N_EXPERTS, logits, -jnp.inf)
    m1 = jnp.max(lg, axis=1, keepdims=True)
    i1 = jnp.min(jnp.where(lg == m1, lane, LANES), axis=1, keepdims=True)
    lg2 = jnp.where(lane == i1, -jnp.inf, lg)
    m2 = jnp.max(lg2, axis=1, keepdims=True)
    i2 = jnp.min(jnp.where(lg2 == m2, lane, LANES), axis=1, keepdims=True)
    e = jnp.exp(m2 - m1)
    idx_ref[:, 0:1] = i1
    idx_ref[:, 1:2] = i2
    gate_ref[:, 0:1] = 1.0 / (1.0 + e)
    gate_ref[:, 1:2] = e / (1.0 + e)


def route_tokens(x, scale, shift, w_router, tm=512):
    B, S, D = x.shape
    T = B * S
    nb = S // tm
    wr = jnp.pad(w_router, ((0, 0), (0, LANES - N_EXPERTS)))
    return pl.pallas_call(
        functools.partial(_router_kernel, tm=tm),
        grid=(B, nb),
        in_specs=[pl.BlockSpec((1, tm, D), lambda b, i: (b, i, 0)),
                  pl.BlockSpec((1, 1, D), lambda b, i: (b, 0, 0)),
                  pl.BlockSpec((1, 1, D), lambda b, i: (b, 0, 0)),
                  pl.BlockSpec((D, LANES), lambda b, i: (0, 0))],
        out_specs=[pl.BlockSpec((tm * D_TILES, LANES), lambda b, i: (b * nb + i, 0)),
                   pl.BlockSpec((tm, TOP_K), lambda b, i: (b * nb + i, 0)),
                   pl.BlockSpec((tm, TOP_K), lambda b, i: (b * nb + i, 0))],
        out_shape=[jax.ShapeDtypeStruct((T * D_TILES, LANES), F32),
                   jax.ShapeDtypeStruct((T, TOP_K), I32),
                   jax.ShapeDtypeStruct((T, TOP_K), F32)],
        compiler_params=_cparams(("parallel", "parallel")),
        name="route_tokens",
    )(x, scale, shift, wr)


def _moe_combine_kernel(y_ref, gt_ref, x_ref, gate_ref, g_ref, b_ref, out_ref, *, tm):
    g0 = gt_ref[:, 0:1]
    g1 = gt_ref[:, 1:2]
    stride = TOP_K * D_TILES
    pieces = []
    for j in range(D_TILES):
        sl = slice(j * LANES, (j + 1) * LANES)
        ff = g0 * y_ref[pl.ds(j, tm, stride=stride), :] + g1 * y_ref[pl.ds(D_TILES + j, tm, stride=stride), :]
        pieces.append(ALPHA * x_ref[0, :, sl] + gate_ref[0, :, sl] * ff)
    z = jnp.concatenate(pieces, axis=1)
    out_ref[0] = _layer_norm(z, g_ref[...], b_ref[...])


def moe_combine_ln(yg, gates, x, gate, ln_g, ln_b, tm=512):
    B, S, D = x.shape
    nb = S // tm
    return pl.pallas_call(
        functools.partial(_moe_combine_kernel, tm=tm),
        grid=(B, nb),
        in_specs=[pl.BlockSpec((tm * TOP_K * D_TILES, LANES), lambda b, i: (b * nb + i, 0)),
                  pl.BlockSpec((tm, TOP_K), lambda b, i: (b * nb + i, 0)),
                  pl.BlockSpec((1, tm, D), lambda b, i: (b, i, 0)),
                  pl.BlockSpec((1, 1, D), lambda b, i: (b, 0, 0)),
                  pl.BlockSpec((1, D), lambda b, i: (0, 0)),
                  pl.BlockSpec((1, D), lambda b, i: (0, 0))],
        out_specs=pl.BlockSpec((1, tm, D), lambda b, i: (b, i, 0)),
        out_shape=jax.ShapeDtypeStruct((B, S, D), F32),
        compiler_params=_cparams(("parallel", "parallel")),
        name="moe_combine_ln",
    )(yg, gates, x, gate, ln_g.reshape(1, D), ln_b.reshape(1, D))


def _scale_cols(w, start, width, factor):
    return w.at[:, start:start + width].multiply(factor)


def even_mixer_layer(x, mod, w_in, w_out, rel_table, sinks, ln_g, ln_b):
    B, S, D = x.shape
    assert S % A_PAD == 0
    shift, scale, gate = mod
    ah, bh, bk = A_HEADS * HEAD_DIM, B_HEADS * HEAD_DIM, B_KV_HEADS * HEAD_DIM
    w = _scale_cols(w_in, 0, ah, HEAD_DIM ** -0.5)
    w = _scale_cols(w, 3 * ah, bh, HEAD_DIM ** -0.5).astype(BF16)
    groups = [(ah, BF16)] * 3 + [(bh, BF16), (bk, BF16), (bk, BF16)]
    qa, ka, va, qb, kb, vb = in_projection(x, scale, shift, w, groups)

    outs, lses = [], []
    for window, d in A_PATTERNS:
        bias = band_bias(rel_table[:, :A_HEADS], d, window // d)
        o, lse = banded_attention(qa, ka, va, bias, None, F32, dil=d)
        outs.append(o)
        lses.append(jnp.transpose(lse, (0, 2, 1, 3)).reshape(B, S, A_HEADS))
    oa = combine_patterns(outs, lses)

    bias_b = band_bias(rel_table[:, A_HEADS:A_HEADS + B_HEADS], 1, B_WINDOW - 1)
    ob, _ = banded_attention(qb, kb, vb, bias_b, sinks.astype(F32), BF16)

    wo = w_out.astype(BF16)
    return out_projection_ln([oa, ob], [wo[:ah], wo[ah:]], x, gate, ln_g, ln_b)


def dsa_mixer_layer(x, mod, w_in, w_out, rel_table, ln_g, ln_b):
    shift, scale, gate = mod
    qw = C_HEADS * HEAD_DIM
    iw = IDX_HEADS * IDX_DIM
    q_w, kc_w, vc_w, qi_w, ki_w, wi_w = jnp.split(
        w_in, [qw, qw + HEAD_DIM, qw + 2 * HEAD_DIM, qw + 2 * HEAD_DIM + iw, qw + 2 * HEAD_DIM + iw + IDX_DIM],
        axis=1)
    pad = jnp.zeros((D_MODEL, LANES - IDX_DIM - IDX_HEADS), w_in.dtype)
    w = jnp.concatenate([q_w * HEAD_DIM ** -0.5, kc_w, vc_w, qi_w, ki_w, wi_w, pad], axis=1).astype(BF16)
    groups = [(qw, BF16), (2 * HEAD_DIM, BF16), (iw, BF16), (LANES, F32)]
    q, kv, qi, kw = in_projection(x, scale, shift, w, groups)
    kc, vc = kv[..., :HEAD_DIM], kv[..., HEAD_DIM:]
    ki = kw[..., :IDX_DIM].astype(BF16)
    wi = kw[..., IDX_DIM:IDX_DIM + IDX_HEADS]
    o = dsa_attention(q, qi, wi, ki, kc, vc, rel_table[:, :C_HEADS])
    return out_projection_ln([o], [w_out.astype(BF16)], x, gate, ln_g, ln_b)


def moe_layer(x, mod, w_router, w1, w3, w2, ln_g, ln_b):
    B, S, D = x.shape
    shift, scale, gate = mod
    T = B * S
    A = T * TOP_K
    h_tiles, top_idx, gates = route_tokens(x, scale, shift, w_router)

    e_flat = top_idx.reshape(-1)
    onehot = (e_flat[:, None] == jnp.arange(N_EXPERTS)[None, :]).astype(I32)
    counts = jnp.sum(onehot, axis=0)
    padded = (counts + MOE_BLOCK - 1) // MOE_BLOCK * MOE_BLOCK
    pends = jnp.cumsum(padded)
    dest = jnp.sum(onehot * (jnp.cumsum(onehot, axis=0) - onehot + (pends - padded)[None, :]), axis=1)
    n_blocks = -(-A // MOE_BLOCK) + N_EXPERTS
    P = n_blocks * MOE_BLOCK
    slot_src = jnp.full((P,), -1, I32).at[dest].set(jnp.arange(A, dtype=I32))
    is_pad = slot_src < 0
    slot_tok = jnp.where(is_pad, 0, slot_src // TOP_K)
    slot_dst = jnp.where(is_pad, A - 1 + jnp.cumsum(is_pad.astype(I32)), slot_src)
    block_expert = jnp.minimum(
        jnp.searchsorted(pends, jnp.arange(n_blocks) * MOE_BLOCK, side='right'), N_EXPERTS - 1).astype(I32)

    yg = ffn_experts(h_tiles, slot_tok, slot_dst, block_expert,
                     w1.astype(BF16), w3.astype(BF16), w2.astype(BF16))
    return moe_combine_ln(yg, gates, x, gate, ln_g, ln_b)


def kernel(x, c, rel_table, w_in_even, w_out_even, sinks, w_in_odd, w_out_odd, ffn_w1, ffn_w3, ffn_w2,
           router, exp_w1, exp_w3, exp_w2, ada_w, ada_b, ln_g, ln_b):
    D = D_MODEL
    mods = ada_modulation_all(c, ada_w, ada_b)

    def mod(layer, sub):
        m = mods[2 * layer + sub]
        return m[:, None, :D], m[:, None, D:2 * D], m[:, None, 2 * D:]

    for layer in range(DEPTH):
        i = layer // 2
        if layer % 2 == 0:
            x = even_mixer_layer(x, mod(layer, 0), w_in_even[i], w_out_even[i], rel_table, sinks[i],
                                 ln_g[layer, 0], ln_b[layer, 0])
            shift, scale, gate = mod(layer, 1)
            x = ffn_dense_ln(x, scale, shift, gate, ln_g[layer, 1], ln_b[layer, 1],
                             ffn_w1[i][None].astype(BF16), ffn_w3[i][None].astype(BF16),
                             ffn_w2[i][None].astype(BF16))
        else:
            x = dsa_mixer_layer(x, mod(layer, 0), w_in_odd[i], w_out_odd[i], rel_table,
                                ln_g[layer, 0], ln_b[layer, 0])
            x = moe_layer(x, mod(layer, 1), router[i], exp_w1[i], exp_w3[i], exp_w2[i],
                          ln_g[layer, 1], ln_b[layer, 1])
    return x
```

```python
import functools
import math

import numpy as np
import jax
import jax.numpy as jnp
from jax import lax
from jax.experimental import pallas as pl
from jax.experimental.pallas import tpu as pltpu

F32 = jnp.float32
BF16 = jnp.bfloat16
I32 = jnp.int32
I16 = jnp.int16

D_MODEL = 1024
HEAD_DIM = 64
BLK = 128
A_HEADS = 8
A_PATTERNS = ((128, 1), (512, 4), (2048, 16))
A_PAD = BLK * 16
B_HEADS = 8
B_KV_HEADS = 2
B_WINDOW = 128
C_HEADS = 16
IDX_HEADS = 8
IDX_DIM = 64
TOPK_MAX = 256
REL_BUCKETS = 32
REL_MAX_DIST = 2048
D_FF = 3584
N_EXPERTS = 8
TOP_K = 2
MOE_BLOCK = 512
DEPTH = 4
ALPHA = (2 * DEPTH) ** 0.25
LN_EPS = 1e-5

LANES = 128
SUBLANES = 8
D_TILES = D_MODEL // LANES
NEG = -(2.0 ** 100)
INT_MIN = -(2 ** 31)
HALF = 2 ** 15
VMEM_LIMIT = 56 * 1024 * 1024

FFN_TF = 1792
DSA_CK = 256
DSA_GK = 512
DSA_HPS = 4
DSA_SKEW = 2


def _bucket_np(dist):
    n = np.maximum(dist, 0)
    max_exact = REL_BUCKETS // 2
    nf = np.maximum(n, 1).astype(np.float32)
    large = max_exact + (np.log(nf / np.float32(max_exact)) / np.float32(math.log(REL_MAX_DIST / max_exact))
                         * np.float32(REL_BUCKETS - max_exact)).astype(np.int32)
    large = np.minimum(large, REL_BUCKETS - 1)
    return np.where(n < max_exact, n, large).astype(np.int32)


def _far_distance():
    b = _bucket_np(np.arange(0, 2 * REL_MAX_DIST))
    return int(np.max(np.nonzero(b != REL_BUCKETS - 1)[0])) + 1


FAR_DIST = _far_distance()
N_NEAR = -(-(FAR_DIST + BLK - 1) // BLK)


def _cparams(sem):
    return pltpu.CompilerParams(dimension_semantics=sem, vmem_limit_bytes=VMEM_LIMIT)


def _layer_norm(z, g, b):
    mu = jnp.mean(z, axis=-1, keepdims=True)
    zc = z - mu
    var = jnp.mean(zc * zc, axis=-1, keepdims=True)
    return zc * lax.rsqrt(var + LN_EPS) * g + b


def _ada_kernel(c_ref, w_ref, b_ref, o_ref):
    j = pl.program_id(1)
    c = c_ref[...]
    sc = c * jax.nn.sigmoid(c)
    mod = jnp.dot(sc, w_ref[0], preferred_element_type=F32, precision=lax.Precision.HIGHEST)
    o_ref[0] = mod + b_ref[0] + jnp.where(j >= 1, 1.0, 0.0)


def ada_modulation_all(c, ada_w, ada_b):
    B, D = c.shape
    n = ada_w.shape[0] * ada_w.shape[1]
    rows = -(-B // SUBLANES) * SUBLANES
    cp = jnp.pad(c, ((0, rows - B), (0, 0)))
    w = ada_w.reshape(n, D, 3 * D)
    b = ada_b.reshape(n, 1, 3 * D)
    out = pl.pallas_call(
        _ada_kernel,
        grid=(n, 3),
        in_specs=[pl.BlockSpec((rows, D), lambda l, j: (0, 0)),
                  pl.BlockSpec((1, D, D), lambda l, j: (l, 0, j)),
                  pl.BlockSpec((1, 1, D), lambda l, j: (l, 0, j))],
        out_specs=pl.BlockSpec((1, rows, D), lambda l, j: (l, 0, j)),
        out_shape=jax.ShapeDtypeStruct((n, rows, 3 * D), F32),
        compiler_params=_cparams(("arbitrary", "arbitrary")),
        name="ada_modulation",
    )(cp, w, b)
    return out[:, :B]


def _inproj_kernel(x_ref, sc_ref, sh_ref, w_ref, *o_refs, splits):
    h = (x_ref[0] * sc_ref[0] + sh_ref[0]).astype(BF16)
    for o_ref, (start, width) in zip(o_refs, splits):
        o_ref[0] = jnp.dot(h, w_ref[:, start:start + width],
                           preferred_element_type=F32).astype(o_ref.dtype)


def in_projection(x, scale, shift, w, groups, tm=512):
    B, S, D = x.shape
    splits, start = [], 0
    for width, _ in groups:
        splits.append((start, width))
        start += width
    assert start == w.shape[1] and S % tm == 0
    return pl.pallas_call(
        functools.partial(_inproj_kernel, splits=tuple(splits)),
        grid=(B, S // tm),
        in_specs=[pl.BlockSpec((1, tm, D), lambda b, i: (b, i, 0)),
                  pl.BlockSpec((1, 1, D), lambda b, i: (b, 0, 0)),
                  pl.BlockSpec((1, 1, D), lambda b, i: (b, 0, 0)),
                  pl.BlockSpec(w.shape, lambda b, i: (0, 0))],
        out_specs=[pl.BlockSpec((1, tm, width), lambda b, i: (b, i, 0)) for width, _ in groups],
        out_shape=[jax.ShapeDtypeStruct((B, S, width), dt) for width, dt in groups],
        compiler_params=_cparams(("parallel", "parallel")),
        name="in_projection",
    )(x, scale, shift, w)


def _band_kernel(sink_ref, q_ref, kp_ref, ko_ref, vp_ref, vo_ref, bias_ref, o_ref, lse_ref, *,
                 hq, hk, use_sinks):
    b = pl.program_id(2)
    q = q_ref[0]
    kk = jnp.concatenate([kp_ref[0], ko_ref[0]], axis=0)
    vv = jnp.concatenate([vp_ref[0], vo_ref[0]], axis=0)
    col = lax.broadcasted_iota(I32, (BLK, 2 * BLK), 1)
    first_mask = jnp.where(jnp.logical_and(b == 0, col < BLK), NEG, 0.0)
    group = hq // hk
    for h in range(hq):
        g = h // group
        qh = q[:, h * HEAD_DIM:(h + 1) * HEAD_DIM]
        kh = kk[:, g * HEAD_DIM:(g + 1) * HEAD_DIM]
        vh = vv[:, g * HEAD_DIM:(g + 1) * HEAD_DIM]
        s = lax.dot_general(qh, kh, (((1,), (1,)), ((), ())), preferred_element_type=F32)
        s = s + bias_ref[h] + first_mask
        m = jnp.max(s, axis=-1, keepdims=True)
        if use_sinks:
            m = jnp.maximum(m, sink_ref[h])
        p = jnp.exp(s - m)
        l = jnp.sum(p, axis=-1, keepdims=True)
        if use_sinks:
            l = l + jnp.exp(sink_ref[h] - m)
        o = jnp.dot(p.astype(BF16), vh, preferred_element_type=F32)
        o_ref[0, :, h * HEAD_DIM:(h + 1) * HEAD_DIM] = (o / l).astype(o_ref.dtype)
        lse_ref[0, 0, :, h:h + 1] = m + jnp.log(l)


def banded_attention(q, k, v, bias, sinks, out_dtype, dil=1):
    B, S, qc = q.shape
    kc = k.shape[2]
    hq = qc // HEAD_DIM
    hk = kc // HEAD_DIM
    ld = S // dil
    use_sinks = sinks is not None
    if sinks is None:
        sinks = jnp.zeros((hq,), F32)
    view = lambda t: t.reshape(B, ld, dil * t.shape[2])
    cur = lambda n, r, b, s: (n, b, r)
    prev = lambda n, r, b, s: (n, jnp.maximum(b - 1, 0), r)
    o, lse = pl.pallas_call(
        functools.partial(_band_kernel, hq=hq, hk=hk, use_sinks=use_sinks),
        grid_spec=pltpu.PrefetchScalarGridSpec(
            num_scalar_prefetch=1,
            grid=(B, dil, ld // BLK),
            in_specs=[pl.BlockSpec((1, BLK, qc), cur),
                      pl.BlockSpec((1, BLK, kc), prev),
                      pl.BlockSpec((1, BLK, kc), cur),
                      pl.BlockSpec((1, BLK, kc), prev),
                      pl.BlockSpec((1, BLK, kc), cur),
                      pl.BlockSpec(bias.shape, lambda n, r, b, s: (0, 0, 0))],
            out_specs=[pl.BlockSpec((1, BLK, qc), cur),
                       pl.BlockSpec((1, 1, BLK, hq), lambda n, r, b, s: (n, r, b, 0))]),
        out_shape=[jax.ShapeDtypeStruct((B, ld, dil * qc), out_dtype),
                   jax.ShapeDtypeStruct((B, dil, ld, hq), F32)],
        compiler_params=_cparams(("parallel", "parallel", "arbitrary")),
        name="banded_attention",
    )(sinks, view(q), view(k), view(k), view(v), view(v), bias)
    return o.reshape(B, S, qc), lse


def band_bias(table, dilation, max_dist):
    qi = np.arange(BLK)[:, None]
    kj = np.arange(2 * BLK)[None, :]
    dist = qi + BLK - kj
    allowed = (dist >= 0) & (dist <= max_dist)
    bias = table[_bucket_np(dist * dilation)]
    bias = jnp.where(allowed[:, :, None], bias, NEG)
    return jnp.transpose(bias, (2, 0, 1)).astype(F32)


def _combine_kernel(*refs, n_pat, heads):
    o_refs, l_refs, out_ref = refs[:n_pat], refs[n_pat:2 * n_pat], refs[2 * n_pat]
    lses = [r[0] for r in l_refs]
    m = functools.reduce(jnp.maximum, lses)
    es = [jnp.exp(l - m) for l in lses]
    tot = functools.reduce(lambda a, b: a + b, es)
    ws = [e / tot for e in es]
    for h in range(heads):
        sl = slice(h * HEAD_DIM, (h + 1) * HEAD_DIM)
        acc = ws[0][:, h:h + 1] * o_refs[0][0, :, sl]
        for p in range(1, n_pat):
            acc = acc + ws[p][:, h:h + 1] * o_refs[p][0, :, sl]
        out_ref[0, :, sl] = acc.astype(out_ref.dtype)


def combine_patterns(outs, lses, tm=512):
    B, S, C = outs[0].shape
    heads = C // HEAD_DIM
    n_pat = len(outs)
    idx = lambda b, i: (b, i, 0)
    return pl.pallas_call(
        functools.partial(_combine_kernel, n_pat=n_pat, heads=heads),
        grid=(B, S // tm),
        in_specs=[pl.BlockSpec((1, tm, C), idx)] * n_pat + [pl.BlockSpec((1, tm, heads), idx)] * n_pat,
        out_specs=pl.BlockSpec((1, tm, C), idx),
        out_shape=jax.ShapeDtypeStruct((B, S, C), BF16),
        compiler_params=_cparams(("parallel", "parallel")),
        name="combine_patterns",
    )(*outs, *lses)


def _outproj_kernel(*refs, n_parts):
    o_refs, w_refs = refs[:n_parts], refs[n_parts:2 * n_parts]
    x_ref, gate_ref, g_ref, b_ref, out_ref = refs[2 * n_parts:]
    mix = jnp.dot(o_refs[0][0], w_refs[0][...], preferred_element_type=F32)
    for o_ref, w_ref in zip(o_refs[1:], w_refs[1:]):
        mix = mix + jnp.dot(o_ref[0], w_ref[...], preferred_element_type=F32)
    z = ALPHA * x_ref[0] + gate_ref[0] * mix
    out_ref[0] = _layer_norm(z, g_ref[...], b_ref[...])


def out_projection_ln(parts, weights, x, gate, ln_g, ln_b, tm=512):
    B, S, D = x.shape
    idx = lambda b, i: (b, i, 0)
    const2 = lambda b, i: (0, 0)
    return pl.pallas_call(
        functools.partial(_outproj_kernel, n_parts=len(parts)),
        grid=(B, S // tm),
        in_specs=([pl.BlockSpec((1, tm, p.shape[2]), idx) for p in parts]
                  + [pl.BlockSpec(w.shape, const2) for w in weights]
                  + [pl.BlockSpec((1, tm, D), idx),
                     pl.BlockSpec((1, 1, D), lambda b, i: (b, 0, 0)),
                     pl.BlockSpec((1, D), const2),
                     pl.BlockSpec((1, D), const2)]),
        out_specs=pl.BlockSpec((1, tm, D), idx),
        out_shape=jax.ShapeDtypeStruct((B, S, D), F32),
        compiler_params=_cparams(("parallel", "parallel")),
        name="out_projection_ln",
    )(*parts, *weights, x, gate, ln_g.reshape(1, D), ln_b.reshape(1, D))


def _dsa_kernel(q_ref, qi_ref, wi_ref, kit_ref, kd_ref, va_ref, vb_ref, strip_ref, o_ref,
                skt_scr, hi_scr, lo_scr, qm_scr, qim_scr, wib_scr, mb_scr, s_scr, m_scr, acc_scr, *, k_sel):
    i = pl.program_id(1)
    ck = DSA_CK
    sub = ck // LANES
    gt = DSA_GK // LANES
    gc = DSA_GK // ck
    nbuf = DSA_SKEW + 1
    ngr = (i * BLK + BLK + DSA_GK - 1) // DSA_GK
    nck = ngr * gc
    lane = lax.broadcasted_iota(I32, (BLK, LANES), 1)
    upper = lane >= HEAD_DIM

    half = C_HEADS // 2
    hps = DSA_HPS
    upper_r = lax.broadcasted_iota(I32, (LANES, BLK), 0) >= HEAD_DIM

    def slot_cols(slot):
        return slot // hps, slice((slot % hps) * BLK, (slot % hps + 1) * BLK)

    def slot_head(slot):
        return 2 * (slot % half) + slot // half

    for p in range(half):
        qt = q_ref[0, :, p * LANES:(p + 1) * LANES].astype(F32).T
        k0, c0 = slot_cols(p)
        k1, c1 = slot_cols(half + p)
        qm_scr[k0, :, c0] = jnp.where(upper_r, 0.0, qt).astype(BF16)
        qm_scr[k1, :, c1] = jnp.where(upper_r, qt, 0.0).astype(BF16)
    for p in range(IDX_HEADS // 2):
        qp = qi_ref[0, :, p * LANES:(p + 1) * LANES]
        qim_scr[2 * p] = jnp.where(upper, jnp.zeros_like(qp), qp)
        qim_scr[2 * p + 1] = jnp.where(upper, qp, jnp.zeros_like(qp))
    wscale = IDX_HEADS ** -0.5 * IDX_DIM ** -0.5
    for h in range(IDX_HEADS):
        wib_scr[h] = jnp.broadcast_to(wi_ref[0, :, h:h + 1] * wscale, (BLK, LANES))

    row_t = lax.broadcasted_iota(I32, (BLK, ck), 0) + i * BLK
    col_l = lax.broadcasted_iota(I32, (BLK, ck), 1)

    def score_chunk(j, carry):
        kt = kit_ref[0, j]
        sc = jnp.zeros((BLK, ck), F32)
        for h in range(IDX_HEADS):
            d = jnp.dot(qim_scr[h], kt, preferred_element_type=F32)
            w = wib_scr[h]
            sc = sc + jnp.maximum(d, 0.0) * jnp.concatenate([w] * sub, axis=1)
        bits = pltpu.bitcast(sc, I32)
        key = bits ^ ((bits >> 31) & 0x7FFFFFFF)
        key = jnp.where(col_l + j * ck <= row_t, key, INT_MIN)
        for u in range(sub):
            tile_t = key[:, u * LANES:(u + 1) * LANES].T
            skt_scr[j * sub + u] = tile_t
            hi_scr[j * sub + u] = (tile_t >> 16).astype(I16)
            lo_scr[j * sub + u] = ((tile_t & 0xFFFF) - HALF).astype(I16)
        return carry

    lax.fori_loop(0, nck, score_chunk, 0)

    packed = 2 * SUBLANES

    def count(scr, cond):
        def group(g, acc):
            parts = []
            for u in range(gt):
                hit = jnp.where(cond(scr[g * gt + u]), jnp.int16(1), jnp.int16(0))
                parts += [hit[r:r + packed] for r in range(0, LANES, packed)]
            while len(parts) > 1:
                parts = [a + b for a, b in zip(parts[0::2], parts[1::2])]
            return acc + parts[0]
        acc = lax.fori_loop(0, ngr, group, jnp.zeros((packed, LANES), I16))
        return jnp.sum(acc.astype(I32), axis=0, keepdims=True)

    def select16(scr, target):
        def bit_pass(t, v):
            c = v | lax.shift_left(jnp.int32(1), 15 - t)
            c16 = (c - HALF).astype(I16)
            return jnp.where(count(scr, lambda x: x >= c16) >= target, c, v)
        return lax.fori_loop(0, 16, bit_pass, jnp.zeros((1, LANES), I32))

    v_hi = select16(hi_scr, k_sel)
    hi16 = (v_hi - HALF).astype(I16)
    above = count(hi_scr, lambda x: x > hi16)
    def keep_low(g, carry):
        for u in range(gt):
            idx = g * gt + u
            lo_scr[idx] = jnp.where(hi_scr[idx] == hi16, lo_scr[idx], jnp.int16(-HALF))
        return carry
    lax.fori_loop(0, ngr, keep_low, 0)
    v_lo = select16(lo_scr, k_sel - above)
    thr_q = jnp.maximum(lax.shift_left(v_hi - HALF, 16) | v_lo, INT_MIN + 1)

    m_scr[...] = jnp.full(m_scr.shape, NEG, F32)
    acc_scr[...] = jnp.zeros(acc_scr.shape, F32)

    def attend_group(g, carry):
        for u in range(gt):
            mb = jnp.where(skt_scr[g * gt + u] >= thr_q, 0.0, NEG).astype(BF16)
            for r in range(hps):
                mb_scr[u, :, r * BLK:(r + 1) * BLK] = mb
        tiles = [jnp.clip(i - (g * gt + u), 0, N_NEAR) for u in range(gt)]

        def logits(k):
            heads = [slot_head(k * hps + r) for r in range(hps)]
            s = jnp.dot(kd_ref[0, g], qm_scr[k], preferred_element_type=F32).astype(BF16)
            mx = None
            for t in range(gt):
                bias = jnp.concatenate([strip_ref[h, tiles[t]] for h in heads], axis=1)
                piece = s[t * BLK:(t + 1) * BLK] + mb_scr[t] + bias
                s_scr[k % nbuf, t] = piece
                mx = piece if mx is None else jnp.maximum(mx, piece)
            m_old = m_scr[k]
            m_new = jnp.maximum(m_old, jnp.max(mx.astype(F32), axis=0, keepdims=True))
            m_scr[k] = m_new
            return m_old, m_new

        def accumulate(k, m_old, m_new):
            m16 = m_new.astype(BF16)
            p = jnp.concatenate([jnp.exp(s_scr[k % nbuf, t] - m16) for t in range(gt)], axis=0)
            v_ref = va_ref if k * hps < half else vb_ref
            pv = jnp.dot(v_ref[0, g], p, preferred_element_type=F32)
            acc_scr[k] = jnp.exp(m_old - m_new) * acc_scr[k] + pv

        nstk = C_HEADS // hps
        stats = [logits(k) for k in range(DSA_SKEW)]
        for k in range(nstk):
            if k + DSA_SKEW < nstk:
                stats.append(logits(k + DSA_SKEW))
            accumulate(k, *stats[k])
        return carry

    lax.fori_loop(0, ngr, attend_group, 0)

    for p in range(half):
        k0, c0 = slot_cols(p)
        k1, c1 = slot_cols(half + p)
        a0 = acc_scr[k0, :, c0]
        a1 = acc_scr[k1, :, c1]
        ot = jnp.where(upper_r, a1 / pltpu.roll(a1, HEAD_DIM, 0), a0 / pltpu.roll(a0, HEAD_DIM, 0))
        o_ref[0, :, p * LANES:(p + 1) * LANES] = ot.T.astype(o_ref.dtype)


def dsa_attention(q, qi, wi, ki, kc, vc, table):
    B, S, _ = q.shape
    ck = DSA_CK
    assert S % DSA_GK == 0
    nchunks = S // ck
    k_sel = min(TOPK_MAX, S // 4)

    def chunked_t(t):
        tt = jnp.transpose(t.reshape(B, nchunks, ck, HEAD_DIM), (0, 1, 3, 2))
        return jnp.concatenate([tt, tt], axis=2)

    ngroups = S // DSA_GK
    ones = jnp.ones_like(vc)
    kd = jnp.concatenate([kc, kc], axis=-1).reshape(B, ngroups, DSA_GK, LANES)

    def grouped_t(t):
        return jnp.transpose(t.reshape(B, ngroups, DSA_GK, LANES), (0, 1, 3, 2))

    va = grouped_t(jnp.concatenate([vc, ones], axis=-1))
    vb = grouped_t(jnp.concatenate([ones, vc], axis=-1))
    j = np.arange(2 * BLK)
    offs = np.where(j <= BLK, -j, 2 * BLK - j)
    v = table[_bucket_np(np.arange(N_NEAR)[:, None] * BLK + offs[None, :])]
    v = jnp.transpose(v, (2, 0, 1)).astype(F32)
    near = jnp.tile(v, (1, 1, BLK))[..., :BLK * (2 * BLK - 1)]
    near = near.reshape(C_HEADS, N_NEAR, BLK, 2 * BLK - 1)[..., :BLK]
    near = jnp.swapaxes(near, 2, 3)
    far = jnp.broadcast_to(table[REL_BUCKETS - 1].astype(F32)[:, None, None, None],
                           (C_HEADS, 1, BLK, LANES))
    strip = jnp.concatenate([near, far], axis=1).astype(BF16)

    once = pl.Buffered(1)
    qblk = lambda b, i: (b, i, 0)
    per_b = lambda b, i: (b, 0, 0, 0)
    return pl.pallas_call(
        functools.partial(_dsa_kernel, k_sel=k_sel),
        grid=(B, S // BLK),
        in_specs=[pl.BlockSpec((1, BLK, C_HEADS * HEAD_DIM), qblk),
                  pl.BlockSpec((1, BLK, IDX_HEADS * IDX_DIM), qblk),
                  pl.BlockSpec((1, BLK, IDX_HEADS), qblk),
                  pl.BlockSpec((1, nchunks, LANES, ck), per_b, pipeline_mode=once),
                  pl.BlockSpec((1, ngroups, DSA_GK, LANES), per_b, pipeline_mode=once),
                  pl.BlockSpec((1, ngroups, LANES, DSA_GK), per_b, pipeline_mode=once),
                  pl.BlockSpec((1, ngroups, LANES, DSA_GK), per_b, pipeline_mode=once),
                  pl.BlockSpec(strip.shape, lambda b, i: (0, 0, 0, 0), pipeline_mode=once)],
        out_specs=pl.BlockSpec((1, BLK, C_HEADS * HEAD_DIM), qblk),
        scratch_shapes=[pltpu.VMEM((S // LANES, LANES, BLK), I32),
                        pltpu.VMEM((S // LANES, LANES, BLK), I16),
                        pltpu.VMEM((S // LANES, LANES, BLK), I16),
                        pltpu.VMEM((C_HEADS // DSA_HPS, LANES, DSA_HPS * BLK), BF16),
                        pltpu.VMEM((IDX_HEADS, BLK, LANES), BF16),
                        pltpu.VMEM((IDX_HEADS, BLK, LANES), F32),
                        pltpu.VMEM((DSA_GK // LANES, LANES, DSA_HPS * BLK), BF16),
                        pltpu.VMEM((DSA_SKEW + 1, DSA_GK // LANES, LANES, DSA_HPS * BLK), BF16),
                        pltpu.VMEM((C_HEADS // DSA_HPS, 1, DSA_HPS * BLK), F32),
                        pltpu.VMEM((C_HEADS // DSA_HPS, LANES, DSA_HPS * BLK), F32)],
        out_shape=jax.ShapeDtypeStruct((B, S, C_HEADS * HEAD_DIM), BF16),
        compiler_params=_cparams(("parallel", "arbitrary")),
        name="dsa_attention",
    )(q, qi, wi, chunked_t(ki), kd, va, vb, strip)


def _ffn_step(x2d, w1_ref, w3_ref, w2_ref, acc):
    xb = x2d[...]
    h1 = jnp.dot(xb, w1_ref[0], preferred_element_type=F32)
    h3 = jnp.dot(xb, w3_ref[0], preferred_element_type=F32)
    a = (h1 * jax.nn.sigmoid(h1) * h3).astype(BF16)
    acc[...] += jnp.dot(a, w2_ref[0], preferred_element_type=F32)


def _ffn_dense_kernel(x_ref, sc_ref, sh_ref, gate_ref, g_ref, b_ref, w1_ref, w3_ref, w2_ref, out_ref,
                      x2d, acc, *, nf):
    f = pl.program_id(2)

    @pl.when(f == 0)
    def _():
        x2d[...] = (x_ref[0] * sc_ref[0] + sh_ref[0]).astype(BF16)
        acc[...] = jnp.zeros(acc.shape, F32)

    _ffn_step(x2d, w1_ref, w3_ref, w2_ref, acc)

    @pl.when(f == nf - 1)
    def _():
        z = ALPHA * x_ref[0] + gate_ref[0] * acc[...]
        out_ref[0] = _layer_norm(z, g_ref[...], b_ref[...])


def ffn_dense_ln(x, scale, shift, gate, ln_g, ln_b, w1, w3, w2, tm=512, tf=FFN_TF):
    B, S, D = x.shape
    nf = D_FF // tf
    xi = lambda b, i, f: (b, i, 0)
    bi = lambda b, i, f: (b, 0, 0)
    c2 = lambda b, i, f: (0, 0)
    return pl.pallas_call(
        functools.partial(_ffn_dense_kernel, nf=nf),
        grid=(B, S // tm, nf),
        in_specs=[pl.BlockSpec((1, tm, D), xi),
                  pl.BlockSpec((1, 1, D), bi), pl.BlockSpec((1, 1, D), bi), pl.BlockSpec((1, 1, D), bi),
                  pl.BlockSpec((1, D), c2), pl.BlockSpec((1, D), c2),
                  pl.BlockSpec((1, D, tf), lambda b, i, f: (0, 0, f)),
                  pl.BlockSpec((1, D, tf), lambda b, i, f: (0, 0, f)),
                  pl.BlockSpec((1, tf, D), lambda b, i, f: (0, f, 0))],
        out_specs=pl.BlockSpec((1, tm, D), xi),
        out_shape=jax.ShapeDtypeStruct((B, S, D), F32),
        scratch_shapes=[pltpu.VMEM((tm, D), BF16), pltpu.VMEM((tm, D), F32)],
        compiler_params=_cparams(("parallel", "parallel", "arbitrary")),
        name="ffn_dense_ln",
    )(x, scale, shift, gate, ln_g.reshape(1, D), ln_b.reshape(1, D), w1, w3, w2)


def _ffn_expert_kernel(be_ref, tok_ref, tok_next_ref, dst_ref, h_hbm, w1_ref, w3_ref, w2_ref, y_hbm,
                       xbuf, ybuf, x2d, acc, gsem, ssem, *, nf, tm):
    i = pl.program_id(0)
    f = pl.program_id(1)
    n_blocks = pl.num_programs(0)
    slot = i % 2
    rows = tm * D_TILES

    def start_gather(idx_ref, s):
        def body(r, carry):
            src = pl.multiple_of(idx_ref[0, 0, r] * D_TILES, D_TILES)
            dst = pl.multiple_of(r * D_TILES, D_TILES)
            pltpu.make_async_copy(h_hbm.at[pl.ds(src, D_TILES)], xbuf.at[s, pl.ds(dst, D_TILES)],
                                  gsem.at[s]).start()
            return carry
        lax.fori_loop(0, tm, body, 0, unroll=8)

    def wait_gather(s):
        pltpu.make_async_copy(h_hbm.at[pl.ds(0, rows)], xbuf.at[s], gsem.at[s]).wait()

    def start_scatter():
        def body(r, carry):
            src = pl.multiple_of(r * D_TILES, D_TILES)
            dst = pl.multiple_of(dst_ref[0, 0, r] * D_TILES, D_TILES)
            pltpu.make_async_copy(ybuf.at[pl.ds(src, D_TILES)], y_hbm.at[pl.ds(dst, D_TILES)], ssem).start()
            return carry
        lax.fori_loop(0, tm, body, 0, unroll=8)

    def wait_scatter():
        pltpu.make_async_copy(ybuf, y_hbm.at[pl.ds(0, rows)], ssem).wait()

    @pl.when(f == 0)
    def _():
        @pl.when(i == 0)
        def _():
            start_gather(tok_ref, 0)

        @pl.when(i + 1 < n_blocks)
        def _():
            start_gather(tok_next_ref, 1 - slot)

        wait_gather(slot)
        for j in range(D_TILES):
            x2d[:, j * LANES:(j + 1) * LANES] = xbuf[slot, pl.ds(j, tm, stride=D_TILES), :].astype(BF16)
        acc[...] = jnp.zeros(acc.shape, F32)

    _ffn_step(x2d, w1_ref, w3_ref, w2_ref, acc)

    @pl.when(f == nf - 1)
    def _():
        @pl.when(i > 0)
        def _():
            wait_scatter()

        for j in range(D_TILES):
            ybuf[pl.ds(j, tm, stride=D_TILES), :] = acc[:, j * LANES:(j + 1) * LANES]
        start_scatter()

        @pl.when(i == n_blocks - 1)
        def _():
            wait_scatter()


def ffn_experts(h_tiles, slot_tok, slot_dst, block_expert, w1, w3, w2, tf=FFN_TF):
    tm = MOE_BLOCK
    P = slot_tok.shape[0]
    n_blocks = P // tm
    nf = D_FF // tf
    tok3 = slot_tok.reshape(n_blocks, 1, tm)
    dst3 = slot_dst.reshape(n_blocks, 1, tm)
    cur = lambda i, f, be: (i, 0, 0)
    nxt = lambda i, f, be: (jnp.minimum(i + 1, n_blocks - 1), 0, 0)
    return pl.pallas_call(
        functools.partial(_ffn_expert_kernel, nf=nf, tm=tm),
        grid_spec=pltpu.PrefetchScalarGridSpec(
            num_scalar_prefetch=1,
            grid=(n_blocks, nf),
            in_specs=[pl.BlockSpec((1, 1, tm), cur, memory_space=pltpu.SMEM),
                      pl.BlockSpec((1, 1, tm), nxt, memory_space=pltpu.SMEM),
                      pl.BlockSpec((1, 1, tm), cur, memory_space=pltpu.SMEM),
                      pl.BlockSpec(memory_space=pl.ANY),
                      pl.BlockSpec((1, D_MODEL, tf), lambda i, f, be: (be[i], 0, f)),
                      pl.BlockSpec((1, D_MODEL, tf), lambda i, f, be: (be[i], 0, f)),
                      pl.BlockSpec((1, tf, D_MODEL), lambda i, f, be: (be[i], f, 0))],
            out_specs=pl.BlockSpec(memory_space=pl.ANY),
            scratch_shapes=[pltpu.VMEM((2, tm * D_TILES, LANES), F32),
                            pltpu.VMEM((tm * D_TILES, LANES), F32),
                            pltpu.VMEM((tm, D_MODEL), BF16),
                            pltpu.VMEM((tm, D_MODEL), F32),
                            pltpu.SemaphoreType.DMA((2,)),
                            pltpu.SemaphoreType.DMA(())]),
        out_shape=jax.ShapeDtypeStruct((P * D_TILES, LANES), F32),
        compiler_params=_cparams(("arbitrary", "arbitrary")),
        name="ffn_experts",
    )(block_expert, tok3, tok3, dst3, h_tiles, w1, w3, w2)


def _router_kernel(x_ref, sc_ref, sh_ref, wr_ref, h_ref, idx_ref, gate_ref, *, tm):
    h = x_ref[0] * sc_ref[0] + sh_ref[0]
    for j in range(D_TILES):
        h_ref[pl.ds(j, tm, stride=D_TILES), :] = h[:, j * LANES:(j + 1) * LANES]
    logits = jnp.dot(h, wr_ref[...], preferred_element_type=F32, precision=lax.Precision.HIGHEST)
    lane = lax.broadcasted_iota(I32, logits.shape, 1)
    lg = jnp.where(lane < N_EXPERTS, logits, -jnp.inf)
    m1 = jnp.max(lg, axis=1, keepdims=True)
    i1 = jnp.min(jnp.where(lg == m1, lane, LANES), axis=1, keepdims=True)
    lg2 = jnp.where(lane == i1, -jnp.inf, lg)
    m2 = jnp.max(lg2, axis=1, keepdims=True)
    i2 = jnp.min(jnp.where(lg2 == m2, lane, LANES), axis=1, keepdims=True)
    e = jnp.exp(m2 - m1)
    idx_ref[:, 0:1] = i1
    idx_ref[:, 1:2] = i2
    gate_ref[:, 0:1] = 1.0 / (1.0 + e)
    gate_ref[:, 1:2] = e / (1.0 + e)


def route_tokens(x, scale, shift, w_router, tm=512):
    B, S, D = x.shape
    T = B * S
    nb = S // tm
    wr = jnp.pad(w_router, ((0, 0), (0, LANES - N_EXPERTS)))
    return pl.pallas_call(
        functools.partial(_router_kernel, tm=tm),
        grid=(B, nb),
        in_specs=[pl.BlockSpec((1, tm, D), lambda b, i: (b, i, 0)),
                  pl.BlockSpec((1, 1, D), lambda b, i: (b, 0, 0)),
                  pl.BlockSpec((1, 1, D), lambda b, i: (b, 0, 0)),
                  pl.BlockSpec((D, LANES), lambda b, i: (0, 0))],
        out_specs=[pl.BlockSpec((tm * D_TILES, LANES), lambda b, i: (b * nb + i, 0)),
                   pl.BlockSpec((tm, TOP_K), lambda b, i: (b * nb + i, 0)),
                   pl.BlockSpec((tm, TOP_K), lambda b, i: (b * nb + i, 0))],
        out_shape=[jax.ShapeDtypeStruct((T * D_TILES, LANES), F32),
                   jax.ShapeDtypeStruct((T, TOP_K), I32),
                   jax.ShapeDtypeStruct((T, TOP_K), F32)],
        compiler_params=_cparams(("parallel", "parallel")),
        name="route_tokens",
    )(x, scale, shift, wr)


def _moe_combine_kernel(y_ref, gt_ref, x_ref, gate_ref, g_ref, b_ref, out_ref, *, tm):
    g0 = gt_ref[:, 0:1]
    g1 = gt_ref[:, 1:2]
    stride = TOP_K * D_TILES
    pieces = []
    for j in range(D_TILES):
        sl = slice(j * LANES, (j + 1) * LANES)
        ff = g0 * y_ref[pl.ds(j, tm, stride=stride), :] + g1 * y_ref[pl.ds(D_TILES + j, tm, stride=stride), :]
        pieces.append(ALPHA * x_ref[0, :, sl] + gate_ref[0, :, sl] * ff)
    z = jnp.concatenate(pieces, axis=1)
    out_ref[0] = _layer_norm(z, g_ref[...], b_ref[...])


def moe_combine_ln(yg, gates, x, gate, ln_g, ln_b, tm=512):
    B, S, D = x.shape
    nb = S // tm
    return pl.pallas_call(
        functools.partial(_moe_combine_kernel, tm=tm),
        grid=(B, nb),
        in_specs=[pl.BlockSpec((tm * TOP_K * D_TILES, LANES), lambda b, i: (b * nb + i, 0)),
                  pl.BlockSpec((tm, TOP_K), lambda b, i: (b * nb + i, 0)),
                  pl.BlockSpec((1, tm, D), lambda b, i: (b, i, 0)),
                  pl.BlockSpec((1, 1, D), lambda b, i: (b, 0, 0)),
                  pl.BlockSpec((1, D), lambda b, i: (0, 0)),
                  pl.BlockSpec((1, D), lambda b, i: (0, 0))],
        out_specs=pl.BlockSpec((1, tm, D), lambda b, i: (b, i, 0)),
        out_shape=jax.ShapeDtypeStruct((B, S, D), F32),
        compiler_params=_cparams(("parallel", "parallel")),
        name="moe_combine_ln",
    )(yg, gates, x, gate, ln_g.reshape(1, D), ln_b.reshape(1, D))


def _scale_cols(w, start, width, factor):
    return w.at[:, start:start + width].multiply(factor)


def even_mixer_layer(x, mod, w_in, w_out, rel_table, sinks, ln_g, ln_b):
    B, S, D = x.shape
    assert S % A_PAD == 0
    shift, scale, gate = mod
    ah, bh, bk = A_HEADS * HEAD_DIM, B_HEADS * HEAD_DIM, B_KV_HEADS * HEAD_DIM
    w = _scale_cols(w_in, 0, ah, HEAD_DIM ** -0.5)
    w = _scale_cols(w, 3 * ah, bh, HEAD_DIM ** -0.5).astype(BF16)
    groups = [(ah, BF16)] * 3 + [(bh, BF16), (bk, BF16), (bk, BF16)]
    qa, ka, va, qb, kb, vb = in_projection(x, scale, shift, w, groups)

    outs, lses = [], []
    for window, d in A_PATTERNS:
        bias = band_bias(rel_table[:, :A_HEADS], d, window // d)
        o, lse = banded_attention(qa, ka, va, bias, None, F32, dil=d)
        outs.append(o)
        lses.append(jnp.transpose(lse, (0, 2, 1, 3)).reshape(B, S, A_HEADS))
    oa = combine_patterns(outs, lses)

    bias_b = band_bias(rel_table[:, A_HEADS:A_HEADS + B_HEADS], 1, B_WINDOW - 1)
    ob, _ = banded_attention(qb, kb, vb, bias_b, sinks.astype(F32), BF16)

    wo = w_out.astype(BF16)
    return out_projection_ln([oa, ob], [wo[:ah], wo[ah:]], x, gate, ln_g, ln_b)


def dsa_mixer_layer(x, mod, w_in, w_out, rel_table, ln_g, ln_b):
    shift, scale, gate = mod
    qw = C_HEADS * HEAD_DIM
    iw = IDX_HEADS * IDX_DIM
    q_w, kc_w, vc_w, qi_w, ki_w, wi_w = jnp.split(
        w_in, [qw, qw + HEAD_DIM, qw + 2 * HEAD_DIM, qw + 2 * HEAD_DIM + iw, qw + 2 * HEAD_DIM + iw + IDX_DIM],
        axis=1)
    pad = jnp.zeros((D_MODEL, LANES - IDX_DIM - IDX_HEADS), w_in.dtype)
    w = jnp.concatenate([q_w * HEAD_DIM ** -0.5, kc_w, vc_w, qi_w, ki_w, wi_w, pad], axis=1).astype(BF16)
    groups = [(qw, BF16), (2 * HEAD_DIM, BF16), (iw, BF16), (LANES, F32)]
    q, kv, qi, kw = in_projection(x, scale, shift, w, groups)
    kc, vc = kv[..., :HEAD_DIM], kv[..., HEAD_DIM:]
    ki = kw[..., :IDX_DIM].astype(BF16)
    wi = kw[..., IDX_DIM:IDX_DIM + IDX_HEADS]
    o = dsa_attention(q, qi, wi, ki, kc, vc, rel_table[:, :C_HEADS])
    return out_projection_ln([o], [w_out.astype(BF16)], x, gate, ln_g, ln_b)


def moe_layer(x, mod, w_router, w1, w3, w2, ln_g, ln_b):
    B, S, D = x.shape
    shift, scale, gate = mod
    T = B * S
    A = T * TOP_K
    h_tiles, top_idx, gates = route_tokens(x, scale, shift, w_router)

    e_flat = top_idx.reshape(-1)
    onehot = (e_flat[:, None] == jnp.arange(N_EXPERTS)[None, :]).astype(I32)
    counts = jnp.sum(onehot, axis=0)
    padded = (counts + MOE_BLOCK - 1) // MOE_BLOCK * MOE_BLOCK
    pends = jnp.cumsum(padded)
    dest = jnp.sum(onehot * (jnp.cumsum(onehot, axis=0) - onehot + (pends - padded)[None, :]), axis=1)
    n_blocks = -(-A // MOE_BLOCK) + N_EXPERTS
    P = n_blocks * MOE_BLOCK
    slot_src = jnp.full((P,), -1, I32).at[dest].set(jnp.arange(A, dtype=I32))
    is_pad = slot_src < 0
    slot_tok = jnp.where(is_pad, 0, slot_src // TOP_K)
    slot_dst = jnp.where(is_pad, A - 1 + jnp.cumsum(is_pad.astype(I32)), slot_src)
    block_expert = jnp.minimum(
        jnp.searchsorted(pends, jnp.arange(n_blocks) * MOE_BLOCK, side='right'), N_EXPERTS - 1).astype(I32)

    yg = ffn_experts(h_tiles, slot_tok, slot_dst, block_expert,
                     w1.astype(BF16), w3.astype(BF16), w2.astype(BF16))
    return moe_combine_ln(yg, gates, x, gate, ln_g, ln_b)


def kernel(x, c, rel_table, w_in_even, w_out_even, sinks, w_in_odd, w_out_odd, ffn_w1, ffn_w3, ffn_w2,
           router, exp_w1, exp_w3, exp_w2, ada_w, ada_b, ln_g, ln_b):
    D = D_MODEL
    mods = ada_modulation_all(c, ada_w, ada_b)

    def mod(layer, sub):
        m = mods[2 * layer + sub]
        return m[:, None, :D], m[:, None, D:2 * D], m[:, None, 2 * D:]

    for layer in range(DEPTH):
        i = layer // 2
        if layer % 2 == 0:
            x = even_mixer_layer(x, mod(layer, 0), w_in_even[i], w_out_even[i], rel_table, sinks[i],
                                 ln_g[layer, 0], ln_b[layer, 0])
            shift, scale, gate = mod(layer, 1)
            x = ffn_dense_ln(x, scale, shift, gate, ln_g[layer, 1], ln_b[layer, 1],
                             ffn_w1[i][None].astype(BF16), ffn_w3[i][None].astype(BF16),
                             ffn_w2[i][None].astype(BF16))
        else:
            x = dsa_mixer_layer(x, mod(layer, 0), w_in_odd[i], w_out_odd[i], rel_table,
                                ln_g[layer, 0], ln_b[layer, 0])
            x = moe_layer(x, mod(layer, 1), router[i], exp_w1[i], exp_w3[i], exp_w2[i],
                          ln_g[layer, 1], ln_b[layer, 1])
    return x
```

```python
import functools
import math

import numpy as np
import jax
import jax.numpy as jnp
from jax import lax
from jax.experimental import pallas as pl
from jax.experimental.pallas import tpu as pltpu

F32 = jnp.float32
BF16 = jnp.bfloat16
I32 = jnp.int32
I16 = jnp.int16

D_MODEL = 1024
HEAD_DIM = 64
BLK = 128
A_HEADS = 8
A_PATTERNS = ((128, 1), (512, 4), (2048, 16))
A_PAD = BLK * 16
B_HEADS = 8
B_KV_HEADS = 2
B_WINDOW = 128
C_HEADS = 16
IDX_HEADS = 8
IDX_DIM = 64
TOPK_MAX = 256
REL_BUCKETS = 32
REL_MAX_DIST = 2048
D_FF = 3584
N_EXPERTS = 8
TOP_K = 2
MOE_BLOCK = 512
DEPTH = 4
ALPHA = (2 * DEPTH) ** 0.25
LN_EPS = 1e-5

LANES = 128
SUBLANES = 8
D_TILES = D_MODEL // LANES
NEG = -(2.0 ** 100)
INT_MIN = -(2 ** 31)
HALF = 2 ** 15
VMEM_LIMIT = 56 * 1024 * 1024

FFN_TF = 1792
DSA_CK = 256
DSA_GK = 512
DSA_HPS = 4
DSA_SKEW = 2


def _bucket_np(dist):
    n = np.maximum(dist, 0)
    max_exact = REL_BUCKETS // 2
    nf = np.maximum(n, 1).astype(np.float32)
    large = max_exact + (np.log(nf / np.float32(max_exact)) / np.float32(math.log(REL_MAX_DIST / max_exact))
                         * np.float32(REL_BUCKETS - max_exact)).astype(np.int32)
    large = np.minimum(large, REL_BUCKETS - 1)
    return np.where(n < max_exact, n, large).astype(np.int32)


def _far_distance():
    b = _bucket_np(np.arange(0, 2 * REL_MAX_DIST))
    return int(np.max(np.nonzero(b != REL_BUCKETS - 1)[0])) + 1


FAR_DIST = _far_distance()
N_NEAR = -(-(FAR_DIST + BLK - 1) // BLK)


def _cparams(sem):
    return pltpu.CompilerParams(dimension_semantics=sem, vmem_limit_bytes=VMEM_LIMIT)


def _layer_norm(z, g, b):
    mu = jnp.mean(z, axis=-1, keepdims=True)
    zc = z - mu
    var = jnp.mean(zc * zc, axis=-1, keepdims=True)
    return zc * lax.rsqrt(var + LN_EPS) * g + b


def _ada_kernel(c_ref, w_ref, b_ref, o_ref):
    j = pl.program_id(1)
    c = c_ref[...]
    sc = c * jax.nn.sigmoid(c)
    mod = jnp.dot(sc, w_ref[0], preferred_element_type=F32, precision=lax.Precision.HIGHEST)
    o_ref[0] = mod + b_ref[0] + jnp.where(j >= 1, 1.0, 0.0)


def ada_modulation_all(c, ada_w, ada_b):
    B, D = c.shape
    n = ada_w.shape[0] * ada_w.shape[1]
    rows = -(-B // SUBLANES) * SUBLANES
    cp = jnp.pad(c, ((0, rows - B), (0, 0)))
    w = ada_w.reshape(n, D, 3 * D)
    b = ada_b.reshape(n, 1, 3 * D)
    out = pl.pallas_call(
        _ada_kernel,
        grid=(n, 3),
        in_specs=[pl.BlockSpec((rows, D), lambda l, j: (0, 0)),
                  pl.BlockSpec((1, D, D), lambda l, j: (l, 0, j)),
                  pl.BlockSpec((1, 1, D), lambda l, j: (l, 0, j))],
        out_specs=pl.BlockSpec((1, rows, D), lambda l, j: (l, 0, j)),
        out_shape=jax.ShapeDtypeStruct((n, rows, 3 * D), F32),
        compiler_params=_cparams(("arbitrary", "arbitrary")),
        name="ada_modulation",
    )(cp, w, b)
    return out[:, :B]


def _inproj_kernel(x_ref, sc_ref, sh_ref, w_ref, *o_refs, splits):
    h = (x_ref[0] * sc_ref[0] + sh_ref[0]).astype(BF16)
    for o_ref, (start, width) in zip(o_refs, splits):
        o_ref[0] = jnp.dot(h, w_ref[:, start:start + width],
                           preferred_element_type=F32).astype(o_ref.dtype)


def in_projection(x, scale, shift, w, groups, tm=512):
    B, S, D = x.shape
    splits, start = [], 0
    for width, _ in groups:
        splits.append((start, width))
        start += width
    assert start == w.shape[1] and S % tm == 0
    return pl.pallas_call(
        functools.partial(_inproj_kernel, splits=tuple(splits)),
        grid=(B, S // tm),
        in_specs=[pl.BlockSpec((1, tm, D), lambda b, i: (b, i, 0)),
                  pl.BlockSpec((1, 1, D), lambda b, i: (b, 0, 0)),
                  pl.BlockSpec((1, 1, D), lambda b, i: (b, 0, 0)),
                  pl.BlockSpec(w.shape, lambda b, i: (0, 0))],
        out_specs=[pl.BlockSpec((1, tm, width), lambda b, i: (b, i, 0)) for width, _ in groups],
        out_shape=[jax.ShapeDtypeStruct((B, S, width), dt) for width, dt in groups],
        compiler_params=_cparams(("parallel", "parallel")),
        name="in_projection",
    )(x, scale, shift, w)


def _band_kernel(sink_ref, q_ref, kp_ref, ko_ref, vp_ref, vo_ref, bias_ref, o_ref, lse_ref, *,
                 hq, hk, use_sinks):
    b = pl.program_id(2)
    q = q_ref[0]
    kk = jnp.concatenate([kp_ref[0], ko_ref[0]], axis=0)
    vv = jnp.concatenate([vp_ref[0], vo_ref[0]], axis=0)
    col = lax.broadcasted_iota(I32, (BLK, 2 * BLK), 1)
    first_mask = jnp.where(jnp.logical_and(b == 0, col < BLK), NEG, 0.0)
    group = hq // hk
    for h in range(hq):
        g = h // group
        qh = q[:, h * HEAD_DIM:(h + 1) * HEAD_DIM]
        kh = kk[:, g * HEAD_DIM:(g + 1) * HEAD_DIM]
        vh = vv[:, g * HEAD_DIM:(g + 1) * HEAD_DIM]
        s = lax.dot_general(qh, kh, (((1,), (1,)), ((), ())), preferred_element_type=F32)
        s = s + bias_ref[h] + first_mask
        m = jnp.max(s, axis=-1, keepdims=True)
        if use_sinks:
            m = jnp.maximum(m, sink_ref[h])
        p = jnp.exp(s - m)
        l = jnp.sum(p, axis=-1, keepdims=True)
        if use_sinks:
            l = l + jnp.exp(sink_ref[h] - m)
        o = jnp.dot(p.astype(BF16), vh, preferred_element_type=F32)
        o_ref[0, :, h * HEAD_DIM:(h + 1) * HEAD_DIM] = (o / l).astype(o_ref.dtype)
        lse_ref[0, 0, :, h:h + 1] = m + jnp.log(l)


def banded_attention(q, k, v, bias, sinks, out_dtype, dil=1):
    B, S, qc = q.shape
    kc = k.shape[2]
    hq = qc // HEAD_DIM
    hk = kc // HEAD_DIM
    ld = S // dil
    use_sinks = sinks is not None
    if sinks is None:
        sinks = jnp.zeros((hq,), F32)
    view = lambda t: t.reshape(B, ld, dil * t.shape[2])
    cur = lambda n, r, b, s: (n, b, r)
    prev = lambda n, r, b, s: (n, jnp.maximum(b - 1, 0), r)
    o, lse = pl.pallas_call(
        functools.partial(_band_kernel, hq=hq, hk=hk, use_sinks=use_sinks),
        grid_spec=pltpu.PrefetchScalarGridSpec(
            num_scalar_prefetch=1,
            grid=(B, dil, ld // BLK),
            in_specs=[pl.BlockSpec((1, BLK, qc), cur),
                      pl.BlockSpec((1, BLK, kc), prev),
                      pl.BlockSpec((1, BLK, kc), cur),
                      pl.BlockSpec((1, BLK, kc), prev),
                      pl.BlockSpec((1, BLK, kc), cur),
                      pl.BlockSpec(bias.shape, lambda n, r, b, s: (0, 0, 0))],
            out_specs=[pl.BlockSpec((1, BLK, qc), cur),
                       pl.BlockSpec((1, 1, BLK, hq), lambda n, r, b, s: (n, r, b, 0))]),
        out_shape=[jax.ShapeDtypeStruct((B, ld, dil * qc), out_dtype),
                   jax.ShapeDtypeStruct((B, dil, ld, hq), F32)],
        compiler_params=_cparams(("parallel", "parallel", "arbitrary")),
        name="banded_attention",
    )(sinks, view(q), view(k), view(k), view(v), view(v), bias)
    return o.reshape(B, S, qc), lse


def band_bias(table, dilation, max_dist):
    qi = np.arange(BLK)[:, None]
    kj = np.arange(2 * BLK)[None, :]
    dist = qi + BLK - kj
    allowed = (dist >= 0) & (dist <= max_dist)
    bias = table[_bucket_np(dist * dilation)]
    bias = jnp.where(allowed[:, :, None], bias, NEG)
    return jnp.transpose(bias, (2, 0, 1)).astype(F32)


def _combine_kernel(*refs, n_pat, heads):
    o_refs, l_refs, out_ref = refs[:n_pat], refs[n_pat:2 * n_pat], refs[2 * n_pat]
    lses = [r[0] for r in l_refs]
    m = functools.reduce(jnp.maximum, lses)
    es = [jnp.exp(l - m) for l in lses]
    tot = functools.reduce(lambda a, b: a + b, es)
    ws = [e / tot for e in es]
    for h in range(heads):
        sl = slice(h * HEAD_DIM, (h + 1) * HEAD_DIM)
        acc = ws[0][:, h:h + 1] * o_refs[0][0, :, sl]
        for p in range(1, n_pat):
            acc = acc + ws[p][:, h:h + 1] * o_refs[p][0, :, sl]
        out_ref[0, :, sl] = acc.astype(out_ref.dtype)


def combine_patterns(outs, lses, tm=512):
    B, S, C = outs[0].shape
    heads = C // HEAD_DIM
    n_pat = len(outs)
    idx = lambda b, i: (b, i, 0)
    return pl.pallas_call(
        functools.partial(_combine_kernel, n_pat=n_pat, heads=heads),
        grid=(B, S // tm),
        in_specs=[pl.BlockSpec((1, tm, C), idx)] * n_pat + [pl.BlockSpec((1, tm, heads), idx)] * n_pat,
        out_specs=pl.BlockSpec((1, tm, C), idx),
        out_shape=jax.ShapeDtypeStruct((B, S, C), BF16),
        compiler_params=_cparams(("parallel", "parallel")),
        name="combine_patterns",
    )(*outs, *lses)


def _outproj_kernel(*refs, n_parts):
    o_refs, w_refs = refs[:n_parts], refs[n_parts:2 * n_parts]
    x_ref, gate_ref, g_ref, b_ref, out_ref = refs[2 * n_parts:]
    mix = jnp.dot(o_refs[0][0], w_refs[0][...], preferred_element_type=F32)
    for o_ref, w_ref in zip(o_refs[1:], w_refs[1:]):
        mix = mix + jnp.dot(o_ref[0], w_ref[...], preferred_element_type=F32)
    z = ALPHA * x_ref[0] + gate_ref[0] * mix
    out_ref[0] = _layer_norm(z, g_ref[...], b_ref[...])


def out_projection_ln(parts, weights, x, gate, ln_g, ln_b, tm=512):
    B, S, D = x.shape
    idx = lambda b, i: (b, i, 0)
    const2 = lambda b, i: (0, 0)
    return pl.pallas_call(
        functools.partial(_outproj_kernel, n_parts=len(parts)),
        grid=(B, S // tm),
        in_specs=([pl.BlockSpec((1, tm, p.shape[2]), idx) for p in parts]
                  + [pl.BlockSpec(w.shape, const2) for w in weights]
                  + [pl.BlockSpec((1, tm, D), idx),
                     pl.BlockSpec((1, 1, D), lambda b, i: (b, 0, 0)),
                     pl.BlockSpec((1, D), const2),
                     pl.BlockSpec((1, D), const2)]),
        out_specs=pl.BlockSpec((1, tm, D), idx),
        out_shape=jax.ShapeDtypeStruct((B, S, D), F32),
        compiler_params=_cparams(("parallel", "parallel")),
        name="out_projection_ln",
    )(*parts, *weights, x, gate, ln_g.reshape(1, D), ln_b.reshape(1, D))


def _dsa_kernel(q_ref, qi_ref, wi_ref, kid_ref, kd_ref, va_ref, vb_ref, strip_ref, o_ref,
                skt_scr, hi_scr, lo_scr, qm_scr, qim_scr, mb_scr, s_scr, m_scr, acc_scr, *, k_sel):
    i = pl.program_id(1)
    ck = DSA_CK
    sub = ck // LANES
    gt = DSA_GK // LANES
    gc = DSA_GK // ck
    nbuf = DSA_SKEW + 1
    ngr = (i * BLK + BLK + DSA_GK - 1) // DSA_GK
    nck = ngr * gc

    half = C_HEADS // 2
    hps = DSA_HPS
    upper_r = lax.broadcasted_iota(I32, (LANES, BLK), 0) >= HEAD_DIM

    def slot_cols(slot):
        return slot // hps, slice((slot % hps) * BLK, (slot % hps + 1) * BLK)

    def slot_head(slot):
        return 2 * (slot % half) + slot // half

    for p in range(half):
        qt = q_ref[0, :, p * LANES:(p + 1) * LANES].astype(F32).T
        k0, c0 = slot_cols(p)
        k1, c1 = slot_cols(half + p)
        qm_scr[k0, :, c0] = jnp.where(upper_r, 0.0, qt).astype(BF16)
        qm_scr[k1, :, c1] = jnp.where(upper_r, qt, 0.0).astype(BF16)
    for p in range(IDX_HEADS // 2):
        qt = qi_ref[0, :, p * LANES:(p + 1) * LANES].astype(F32).T
        qim_scr[:, (2 * p) * BLK:(2 * p + 1) * BLK] = jnp.where(upper_r, 0.0, qt).astype(BF16)
        qim_scr[:, (2 * p + 1) * BLK:(2 * p + 2) * BLK] = jnp.where(upper_r, qt, 0.0).astype(BF16)
    wscale = IDX_HEADS ** -0.5 * IDX_DIM ** -0.5
    wt = wi_ref[0, 0] * wscale

    key_pos = lax.broadcasted_iota(I32, (ck, BLK), 0)
    qry_pos = lax.broadcasted_iota(I32, (ck, BLK), 1) + i * BLK

    def score_chunk(j, carry):
        kblk = kid_ref[0, j]
        sc = jnp.zeros((ck, BLK), F32)
        for p in range(IDX_HEADS // 2):
            d = jnp.dot(kblk, qim_scr[:, 2 * p * BLK:(2 * p + 2) * BLK], preferred_element_type=F32)
            for r in range(2):
                h = 2 * p + r
                sc = sc + jnp.maximum(d[:, r * BLK:(r + 1) * BLK], 0.0) * wt[h:h + 1]
        bits = pltpu.bitcast(sc, I32)
        key = bits ^ ((bits >> 31) & 0x7FFFFFFF)
        key = jnp.where(key_pos + j * ck <= qry_pos, key, INT_MIN)
        for u in range(sub):
            tile_t = key[u * LANES:(u + 1) * LANES]
            skt_scr[j * sub + u] = tile_t
            hi_scr[j * sub + u] = (tile_t >> 16).astype(I16)
            lo_scr[j * sub + u] = ((tile_t & 0xFFFF) - HALF).astype(I16)
        return carry

    lax.fori_loop(0, nck, score_chunk, 0)

    packed = 2 * SUBLANES

    def count(scr, cond):
        def group(g, acc):
            parts = []
            for u in range(gt):
                hit = jnp.where(cond(scr[g * gt + u]), jnp.int16(1), jnp.int16(0))
                parts += [hit[r:r + packed] for r in range(0, LANES, packed)]
            while len(parts) > 1:
                parts = [a + b for a, b in zip(parts[0::2], parts[1::2])]
            return acc + parts[0]
        acc = lax.fori_loop(0, ngr, group, jnp.zeros((packed, LANES), I16))
        return jnp.sum(acc.astype(I32), axis=0, keepdims=True)

    def select16(scr, target):
        def bit_pass(t, v):
            c = v | lax.shift_left(jnp.int32(1), 15 - t)
            c16 = (c - HALF).astype(I16)
            return jnp.where(count(scr, lambda x: x >= c16) >= target, c, v)
        return lax.fori_loop(0, 16, bit_pass, jnp.zeros((1, LANES), I32))

    v_hi = select16(hi_scr, k_sel)
    hi16 = (v_hi - HALF).astype(I16)
    above = count(hi_scr, lambda x: x > hi16)
    def keep_low(g, carry):
        for u in range(gt):
            idx = g * gt + u
            lo_scr[idx] = jnp.where(hi_scr[idx] == hi16, lo_scr[idx], jnp.int16(-HALF))
        return carry
    lax.fori_loop(0, ngr, keep_low, 0)
    v_lo = select16(lo_scr, k_sel - above)
    thr_q = jnp.maximum(lax.shift_left(v_hi - HALF, 16) | v_lo, INT_MIN + 1)

    m_scr[...] = jnp.full(m_scr.shape, NEG, F32)
    acc_scr[...] = jnp.zeros(acc_scr.shape, F32)

    def attend_group(g, carry):
        for u in range(gt):
            mb = jnp.where(skt_scr[g * gt + u] >= thr_q, 0.0, NEG).astype(BF16)
            for r in range(hps):
                mb_scr[u, :, r * BLK:(r + 1) * BLK] = mb
        tiles = [jnp.clip(i - (g * gt + u), 0, N_NEAR) for u in range(gt)]

        def logits(k):
            heads = [slot_head(k * hps + r) for r in range(hps)]
            s = jnp.dot(kd_ref[0, g], qm_scr[k], preferred_element_type=F32).astype(BF16)
            mx = None
            for t in range(gt):
                bias = jnp.concatenate([strip_ref[h, tiles[t]] for h in heads], axis=1)
                piece = s[t * BLK:(t + 1) * BLK] + mb_scr[t] + bias
                s_scr[k % nbuf, t] = piece
                mx = piece if mx is None else jnp.maximum(mx, piece)
            m_old = m_scr[k]
            m_new = jnp.maximum(m_old, jnp.max(mx.astype(F32), axis=0, keepdims=True))
            m_scr[k] = m_new
            return m_old, m_new

        def accumulate(k, m_old, m_new):
            m16 = m_new.astype(BF16)
            p = jnp.concatenate([jnp.exp(s_scr[k % nbuf, t] - m16) for t in range(gt)], axis=0)
            v_ref = va_ref if k * hps < half else vb_ref
            pv = jnp.dot(v_ref[0, g], p, preferred_element_type=F32)
            acc_scr[k] = jnp.exp(m_old - m_new) * acc_scr[k] + pv

        nstk = C_HEADS // hps
        stats = [logits(k) for k in range(DSA_SKEW)]
        for k in range(nstk):
            if k + DSA_SKEW < nstk:
                stats.append(logits(k + DSA_SKEW))
            accumulate(k, *stats[k])
        return carry

    lax.fori_loop(0, ngr, attend_group, 0)

    for p in range(half):
        k0, c0 = slot_cols(p)
        k1, c1 = slot_cols(half + p)
        a0 = acc_scr[k0, :, c0]
        a1 = acc_scr[k1, :, c1]
        ot = jnp.where(upper_r, a1 / pltpu.roll(a1, HEAD_DIM, 0), a0 / pltpu.roll(a0, HEAD_DIM, 0))
        o_ref[0, :, p * LANES:(p + 1) * LANES] = ot.T.astype(o_ref.dtype)


def dsa_attention(q, qi, wi, ki, kc, vc, table):
    B, S, _ = q.shape
    ck = DSA_CK
    assert S % DSA_GK == 0
    nchunks = S // ck
    k_sel = min(TOPK_MAX, S // 4)

    ngroups = S // DSA_GK
    ones = jnp.ones_like(vc)
    kid = jnp.concatenate([ki, ki], axis=-1).reshape(B, nchunks, ck, LANES)
    kd = jnp.concatenate([kc, kc], axis=-1).reshape(B, ngroups, DSA_GK, LANES)
    wit = jnp.transpose(wi.reshape(B, S // BLK, BLK, IDX_HEADS), (0, 1, 3, 2))

    def grouped_t(t):
        return jnp.transpose(t.reshape(B, ngroups, DSA_GK, LANES), (0, 1, 3, 2))

    va = grouped_t(jnp.concatenate([vc, ones], axis=-1))
    vb = grouped_t(jnp.concatenate([ones, vc], axis=-1))
    j = np.arange(2 * BLK)
    offs = np.where(j <= BLK, -j, 2 * BLK - j)
    v = table[_bucket_np(np.arange(N_NEAR)[:, None] * BLK + offs[None, :])]
    v = jnp.transpose(v, (2, 0, 1)).astype(F32)
    near = jnp.tile(v, (1, 1, BLK))[..., :BLK * (2 * BLK - 1)]
    near = near.reshape(C_HEADS, N_NEAR, BLK, 2 * BLK - 1)[..., :BLK]
    near = jnp.swapaxes(near, 2, 3)
    far = jnp.broadcast_to(table[REL_BUCKETS - 1].astype(F32)[:, None, None, None],
                           (C_HEADS, 1, BLK, LANES))
    strip = jnp.concatenate([near, far], axis=1).astype(BF16)

    once = pl.Buffered(1)
    qblk = lambda b, i: (b, i, 0)
    per_b = lambda b, i: (b, 0, 0, 0)
    return pl.pallas_call(
        functools.partial(_dsa_kernel, k_sel=k_sel),
        grid=(B, S // BLK),
        in_specs=[pl.BlockSpec((1, BLK, C_HEADS * HEAD_DIM), qblk),
                  pl.BlockSpec((1, BLK, IDX_HEADS * IDX_DIM), qblk),
                  pl.BlockSpec((1, 1, IDX_HEADS, BLK), lambda b, i: (b, i, 0, 0)),
                  pl.BlockSpec((1, nchunks, ck, LANES), per_b, pipeline_mode=once),
                  pl.BlockSpec((1, ngroups, DSA_GK, LANES), per_b, pipeline_mode=once),
                  pl.BlockSpec((1, ngroups, LANES, DSA_GK), per_b, pipeline_mode=once),
                  pl.BlockSpec((1, ngroups, LANES, DSA_GK), per_b, pipeline_mode=once),
                  pl.BlockSpec(strip.shape, lambda b, i: (0, 0, 0, 0), pipeline_mode=once)],
        out_specs=pl.BlockSpec((1, BLK, C_HEADS * HEAD_DIM), qblk),
        scratch_shapes=[pltpu.VMEM((S // LANES, LANES, BLK), I32),
                        pltpu.VMEM((S // LANES, LANES, BLK), I16),
                        pltpu.VMEM((S // LANES, LANES, BLK), I16),
                        pltpu.VMEM((C_HEADS // DSA_HPS, LANES, DSA_HPS * BLK), BF16),
                        pltpu.VMEM((LANES, IDX_HEADS * BLK), BF16),
                        pltpu.VMEM((DSA_GK // LANES, LANES, DSA_HPS * BLK), BF16),
                        pltpu.VMEM((DSA_SKEW + 1, DSA_GK // LANES, LANES, DSA_HPS * BLK), BF16),
                        pltpu.VMEM((C_HEADS // DSA_HPS, 1, DSA_HPS * BLK), F32),
                        pltpu.VMEM((C_HEADS // DSA_HPS, LANES, DSA_HPS * BLK), F32)],
        out_shape=jax.ShapeDtypeStruct((B, S, C_HEADS * HEAD_DIM), BF16),
        compiler_params=_cparams(("parallel", "arbitrary")),
        name="dsa_attention",
    )(q, qi, wit, kid, kd, va, vb, strip)


def _ffn_step(x2d, w1_ref, w3_ref, w2_ref, acc):
    xb = x2d[...]
    h1 = jnp.dot(xb, w1_ref[0], preferred_element_type=F32)
    h3 = jnp.dot(xb, w3_ref[0], preferred_element_type=F32)
    a = (h1 * jax.nn.sigmoid(h1) * h3).astype(BF16)
    acc[...] += jnp.dot(a, w2_ref[0], preferred_element_type=F32)


def _ffn_dense_kernel(x_ref, sc_ref, sh_ref, gate_ref, g_ref, b_ref, w1_ref, w3_ref, w2_ref, out_ref,
                      x2d, acc, *, nf):
    f = pl.program_id(2)

    @pl.when(f == 0)
    def _():
        x2d[...] = (x_ref[0] * sc_ref[0] + sh_ref[0]).astype(BF16)
        acc[...] = jnp.zeros(acc.shape, F32)

    _ffn_step(x2d, w1_ref, w3_ref, w2_ref, acc)

    @pl.when(f == nf - 1)
    def _():
        z = ALPHA * x_ref[0] + gate_ref[0] * acc[...]
        out_ref[0] = _layer_norm(z, g_ref[...], b_ref[...])


def ffn_dense_ln(x, scale, shift, gate, ln_g, ln_b, w1, w3, w2, tm=512, tf=FFN_TF):
    B, S, D = x.shape
    nf = D_FF // tf
    xi = lambda b, i, f: (b, i, 0)
    bi = lambda b, i, f: (b, 0, 0)
    c2 = lambda b, i, f: (0, 0)
    return pl.pallas_call(
        functools.partial(_ffn_dense_kernel, nf=nf),
        grid=(B, S // tm, nf),
        in_specs=[pl.BlockSpec((1, tm, D), xi),
                  pl.BlockSpec((1, 1, D), bi), pl.BlockSpec((1, 1, D), bi), pl.BlockSpec((1, 1, D), bi),
                  pl.BlockSpec((1, D), c2), pl.BlockSpec((1, D), c2),
                  pl.BlockSpec((1, D, tf), lambda b, i, f: (0, 0, f)),
                  pl.BlockSpec((1, D, tf), lambda b, i, f: (0, 0, f)),
                  pl.BlockSpec((1, tf, D), lambda b, i, f: (0, f, 0))],
        out_specs=pl.BlockSpec((1, tm, D), xi),
        out_shape=jax.ShapeDtypeStruct((B, S, D), F32),
        scratch_shapes=[pltpu.VMEM((tm, D), BF16), pltpu.VMEM((tm, D), F32)],
        compiler_params=_cparams(("parallel", "parallel", "arbitrary")),
        name="ffn_dense_ln",
    )(x, scale, shift, gate, ln_g.reshape(1, D), ln_b.reshape(1, D), w1, w3, w2)


def _ffn_expert_kernel(be_ref, tok_ref, tok_next_ref, dst_ref, h_hbm, w1_ref, w3_ref, w2_ref, y_hbm,
                       xbuf, ybuf, x2d, acc, gsem, ssem, *, nf, tm):
    i = pl.program_id(0)
    f = pl.program_id(1)
    n_blocks = pl.num_programs(0)
    slot = i % 2
    rows = tm * D_TILES

    def start_gather(idx_ref, s):
        def body(r, carry):
            src = pl.multiple_of(idx_ref[0, 0, r] * D_TILES, D_TILES)
            dst = pl.multiple_of(r * D_TILES, D_TILES)
            pltpu.make_async_copy(h_hbm.at[pl.ds(src, D_TILES)], xbuf.at[s, pl.ds(dst, D_TILES)],
                                  gsem.at[s]).start()
            return carry
        lax.fori_loop(0, tm, body, 0, unroll=8)

    def wait_gather(s):
        pltpu.make_async_copy(h_hbm.at[pl.ds(0, rows)], xbuf.at[s], gsem.at[s]).wait()

    def start_scatter():
        def body(r, carry):
            src = pl.multiple_of(r * D_TILES, D_TILES)
            dst = pl.multiple_of(dst_ref[0, 0, r] * D_TILES, D_TILES)
            pltpu.make_async_copy(ybuf.at[pl.ds(src, D_TILES)], y_hbm.at[pl.ds(dst, D_TILES)], ssem).start()
            return carry
        lax.fori_loop(0, tm, body, 0, unroll=8)

    def wait_scatter():
        pltpu.make_async_copy(ybuf, y_hbm.at[pl.ds(0, rows)], ssem).wait()

    @pl.when(f == 0)
    def _():
        @pl.when(i == 0)
        def _():
            start_gather(tok_ref, 0)

        @pl.when(i + 1 < n_blocks)
        def _():
            start_gather(tok_next_ref, 1 - slot)

        wait_gather(slot)
        for j in range(D_TILES):
            x2d[:, j * LANES:(j + 1) * LANES] = xbuf[slot, pl.ds(j, tm, stride=D_TILES), :].astype(BF16)
        acc[...] = jnp.zeros(acc.shape, F32)

    _ffn_step(x2d, w1_ref, w3_ref, w2_ref, acc)

    @pl.when(f == nf - 1)
    def _():
        @pl.when(i > 0)
        def _():
            wait_scatter()

        for j in range(D_TILES):
            ybuf[pl.ds(j, tm, stride=D_TILES), :] = acc[:, j * LANES:(j + 1) * LANES]
        start_scatter()

        @pl.when(i == n_blocks - 1)
        def _():
            wait_scatter()


def ffn_experts(h_tiles, slot_tok, slot_dst, block_expert, w1, w3, w2, tf=FFN_TF):
    tm = MOE_BLOCK
    P = slot_tok.shape[0]
    n_blocks = P // tm
    nf = D_FF // tf
    tok3 = slot_tok.reshape(n_blocks, 1, tm)
    dst3 = slot_dst.reshape(n_blocks, 1, tm)
    cur = lambda i, f, be: (i, 0, 0)
    nxt = lambda i, f, be: (jnp.minimum(i + 1, n_blocks - 1), 0, 0)
    return pl.pallas_call(
        functools.partial(_ffn_expert_kernel, nf=nf, tm=tm),
        grid_spec=pltpu.PrefetchScalarGridSpec(
            num_scalar_prefetch=1,
            grid=(n_blocks, nf),
            in_specs=[pl.BlockSpec((1, 1, tm), cur, memory_space=pltpu.SMEM),
                      pl.BlockSpec((1, 1, tm), nxt, memory_space=pltpu.SMEM),
                      pl.BlockSpec((1, 1, tm), cur, memory_space=pltpu.SMEM),
                      pl.BlockSpec(memory_space=pl.ANY),
                      pl.BlockSpec((1, D_MODEL, tf), lambda i, f, be: (be[i], 0, f)),
                      pl.BlockSpec((1, D_MODEL, tf), lambda i, f, be: (be[i], 0, f)),
                      pl.BlockSpec((1, tf, D_MODEL), lambda i, f, be: (be[i], f, 0))],
            out_specs=pl.BlockSpec(memory_space=pl.ANY),
            scratch_shapes=[pltpu.VMEM((2, tm * D_TILES, LANES), F32),
                            pltpu.VMEM((tm * D_TILES, LANES), F32),
                            pltpu.VMEM((tm, D_MODEL), BF16),
                            pltpu.VMEM((tm, D_MODEL), F32),
                            pltpu.SemaphoreType.DMA((2,)),
                            pltpu.SemaphoreType.DMA(())]),
        out_shape=jax.ShapeDtypeStruct((P * D_TILES, LANES), F32),
        compiler_params=_cparams(("arbitrary", "arbitrary")),
        name="ffn_experts",
    )(block_expert, tok3, tok3, dst3, h_tiles, w1, w3, w2)


def _router_kernel(x_ref, sc_ref, sh_ref, wr_ref, h_ref, idx_ref, gate_ref, *, tm):
    h = x_ref[0] * sc_ref[0] + sh_ref[0]
    for j in range(D_TILES):
        h_ref[pl.ds(j, tm, stride=D_TILES), :] = h[:, j * LANES:(j + 1) * LANES]
    logits = jnp.dot(h, wr_ref[...], preferred_element_type=F32, precision=lax.Precision.HIGHEST)
    lane = lax.broadcasted_iota(I32, logits.shape, 1)
    lg = jnp.where(lane < N_EXPERTS, logits, -jnp.inf)
    m1 = jnp.max(lg, axis=1, keepdims=True)
    i1 = jnp.min(jnp.where(lg == m1, lane, LANES), axis=1, keepdims=True)
    lg2 = jnp.where(lane == i1, -jnp.inf, lg)
    m2 = jnp.max(lg2, axis=1, keepdims=True)
    i2 = jnp.min(jnp.where(lg2 == m2, lane, LANES), axis=1, keepdims=True)
    e = jnp.exp(m2 - m1)
    idx_ref[:, 0:1] = i1
    idx_ref[:, 1:2] = i2
    gate_ref[:, 0:1] = 1.0 / (1.0 + e)
    gate_ref[:, 1:2] = e / (1.0 + e)


def route_tokens(x, scale, shift, w_router, tm=512):
    B, S, D = x.shape
    T = B * S
    nb = S // tm
    wr = jnp.pad(w_router, ((0, 0), (0, LANES - N_EXPERTS)))
    return pl.pallas_call(
        functools.partial(_router_kernel, tm=tm),
        grid=(B, nb),
        in_specs=[pl.BlockSpec((1, tm, D), lambda b, i: (b, i, 0)),
                  pl.BlockSpec((1, 1, D), lambda b, i: (b, 0, 0)),
                  pl.BlockSpec((1, 1, D), lambda b, i: (b, 0, 0)),
                  pl.BlockSpec((D, LANES), lambda b, i: (0, 0))],
        out_specs=[pl.BlockSpec((tm * D_TILES, LANES), lambda b, i: (b * nb + i, 0)),
                   pl.BlockSpec((tm, TOP_K), lambda b, i: (b * nb + i, 0)),
                   pl.BlockSpec((tm, TOP_K), lambda b, i: (b * nb + i, 0))],
        out_shape=[jax.ShapeDtypeStruct((T * D_TILES, LANES), F32),
                   jax.ShapeDtypeStruct((T, TOP_K), I32),
                   jax.ShapeDtypeStruct((T, TOP_K), F32)],
        compiler_params=_cparams(("parallel", "parallel")),
        name="route_tokens",
    )(x, scale, shift, wr)


def _moe_combine_kernel(y_ref, gt_ref, x_ref, gate_ref, g_ref, b_ref, out_ref, *, tm):
    g0 = gt_ref[:, 0:1]
    g1 = gt_ref[:, 1:2]
    stride = TOP_K * D_TILES
    pieces = []
    for j in range(D_TILES):
        sl = slice(j * LANES, (j + 1) * LANES)
        ff = g0 * y_ref[pl.ds(j, tm, stride=stride), :] + g1 * y_ref[pl.ds(D_TILES + j, tm, stride=stride), :]
        pieces.append(ALPHA * x_ref[0, :, sl] + gate_ref[0, :, sl] * ff)
    z = jnp.concatenate(pieces, axis=1)
    out_ref[0] = _layer_norm(z, g_ref[...], b_ref[...])


def moe_combine_ln(yg, gates, x, gate, ln_g, ln_b, tm=512):
    B, S, D = x.shape
    nb = S // tm
    return pl.pallas_call(
        functools.partial(_moe_combine_kernel, tm=tm),
        grid=(B, nb),
        in_specs=[pl.BlockSpec((tm * TOP_K * D_TILES, LANES), lambda b, i: (b * nb + i, 0)),
                  pl.BlockSpec((tm, TOP_K), lambda b, i: (b * nb + i, 0)),
                  pl.BlockSpec((1, tm, D), lambda b, i: (b, i, 0)),
                  pl.BlockSpec((1, 1, D), lambda b, i: (b, 0, 0)),
                  pl.BlockSpec((1, D), lambda b, i: (0, 0)),
                  pl.BlockSpec((1, D), lambda b, i: (0, 0))],
        out_specs=pl.BlockSpec((1, tm, D), lambda b, i: (b, i, 0)),
        out_shape=jax.ShapeDtypeStruct((B, S, D), F32),
        compiler_params=_cparams(("parallel", "parallel")),
        name="moe_combine_ln",
    )(yg, gates, x, gate, ln_g.reshape(1, D), ln_b.reshape(1, D))


def _scale_cols(w, start, width, factor):
    return w.at[:, start:start + width].multiply(factor)


def even_mixer_layer(x, mod, w_in, w_out, rel_table, sinks, ln_g, ln_b):
    B, S, D = x.shape
    assert S % A_PAD == 0
    shift, scale, gate = mod
    ah, bh, bk = A_HEADS * HEAD_DIM, B_HEADS * HEAD_DIM, B_KV_HEADS * HEAD_DIM
    w = _scale_cols(w_in, 0, ah, HEAD_DIM ** -0.5)
    w = _scale_cols(w, 3 * ah, bh, HEAD_DIM ** -0.5).astype(BF16)
    groups = [(ah, BF16)] * 3 + [(bh, BF16), (bk, BF16), (bk, BF16)]
    qa, ka, va, qb, kb, vb = in_projection(x, scale, shift, w, groups)

    outs, lses = [], []
    for window, d in A_PATTERNS:
        bias = band_bias(rel_table[:, :A_HEADS], d, window // d)
        o, lse = banded_attention(qa, ka, va, bias, None, F32, dil=d)
        outs.append(o)
        lses.append(jnp.transpose(lse, (0, 2, 1, 3)).reshape(B, S, A_HEADS))
    oa = combine_patterns(outs, lses)

    bias_b = band_bias(rel_table[:, A_HEADS:A_HEADS + B_HEADS], 1, B_WINDOW - 1)
    ob, _ = banded_attention(qb, kb, vb, bias_b, sinks.astype(F32), BF16)

    wo = w_out.astype(BF16)
    return out_projection_ln([oa, ob], [wo[:ah], wo[ah:]], x, gate, ln_g, ln_b)


def dsa_mixer_layer(x, mod, w_in, w_out, rel_table, ln_g, ln_b):
    shift, scale, gate = mod
    qw = C_HEADS * HEAD_DIM
    iw = IDX_HEADS * IDX_DIM
    q_w, kc_w, vc_w, qi_w, ki_w, wi_w = jnp.split(
        w_in, [qw, qw + HEAD_DIM, qw + 2 * HEAD_DIM, qw + 2 * HEAD_DIM + iw, qw + 2 * HEAD_DIM + iw + IDX_DIM],
        axis=1)
    pad = jnp.zeros((D_MODEL, LANES - IDX_DIM - IDX_HEADS), w_in.dtype)
    w = jnp.concatenate([q_w * HEAD_DIM ** -0.5, kc_w, vc_w, qi_w, ki_w, wi_w, pad], axis=1).astype(BF16)
    groups = [(qw, BF16), (2 * HEAD_DIM, BF16), (iw, BF16), (LANES, F32)]
    q, kv, qi, kw = in_projection(x, scale, shift, w, groups)
    kc, vc = kv[..., :HEAD_DIM], kv[..., HEAD_DIM:]
    ki = kw[..., :IDX_DIM].astype(BF16)
    wi = kw[..., IDX_DIM:IDX_DIM + IDX_HEADS]
    o = dsa_attention(q, qi, wi, ki, kc, vc, rel_table[:, :C_HEADS])
    return out_projection_ln([o], [w_out.astype(BF16)], x, gate, ln_g, ln_b)


def moe_layer(x, mod, w_router, w1, w3, w2, ln_g, ln_b):
    B, S, D = x.shape
    shift, scale, gate = mod
    T = B * S
    A = T * TOP_K
    h_tiles, top_idx, gates = route_tokens(x, scale, shift, w_router)

    e_flat = top_idx.reshape(-1)
    onehot = (e_flat[:, None] == jnp.arange(N_EXPERTS)[None, :]).astype(I32)
    counts = jnp.sum(onehot, axis=0)
    padded = (counts + MOE_BLOCK - 1) // MOE_BLOCK * MOE_BLOCK
    pends = jnp.cumsum(padded)
    dest = jnp.sum(onehot * (jnp.cumsum(onehot, axis=0) - onehot + (pends - padded)[None, :]), axis=1)
    n_blocks = -(-A // MOE_BLOCK) + N_EXPERTS
    P = n_blocks * MOE_BLOCK
    slot_src = jnp.full((P,), -1, I32).at[dest].set(jnp.arange(A, dtype=I32))
    is_pad = slot_src < 0
    slot_tok = jnp.where(is_pad, 0, slot_src // TOP_K)
    slot_dst = jnp.where(is_pad, A - 1 + jnp.cumsum(is_pad.astype(I32)), slot_src)
    block_expert = jnp.minimum(
        jnp.searchsorted(pends, jnp.arange(n_blocks) * MOE_BLOCK, side='right'), N_EXPERTS - 1).astype(I32)

    yg = ffn_experts(h_tiles, slot_tok, slot_dst, block_expert,
                     w1.astype(BF16), w3.astype(BF16), w2.astype(BF16))
    return moe_combine_ln(yg, gates, x, gate, ln_g, ln_b)


def kernel(x, c, rel_table, w_in_even, w_out_even, sinks, w_in_odd, w_out_odd, ffn_w1, ffn_w3, ffn_w2,
           router, exp_w1, exp_w3, exp_w2, ada_w, ada_b, ln_g, ln_b):
    D = D_MODEL
    mods = ada_modulation_all(c, ada_w, ada_b)

    def mod(layer, sub):
        m = mods[2 * layer + sub]
        return m[:, None, :D], m[:, None, D:2 * D], m[:, None, 2 * D:]

    for layer in range(DEPTH):
        i = layer // 2
        if layer % 2 == 0:
            x = even_mixer_layer(x, mod(layer, 0), w_in_even[i], w_out_even[i], rel_table, sinks[i],
                                 ln_g[layer, 0], ln_b[layer, 0])
            shift, scale, gate = mod(layer, 1)
            x = ffn_dense_ln(x, scale, shift, gate, ln_g[layer, 1], ln_b[layer, 1],
                             ffn_w1[i][None].astype(BF16), ffn_w3[i][None].astype(BF16),
                             ffn_w2[i][None].astype(BF16))
        else:
            x = dsa_mixer_layer(x, mod(layer, 0), w_in_odd[i], w_out_odd[i], rel_table,
                                ln_g[layer, 0], ln_b[layer, 0])
            x = moe_layer(x, mod(layer, 1), router[i], exp_w1[i], exp_w3[i], exp_w2[i],
                          ln_g[layer, 1], ln_b[layer, 1])
    return x
```

```python
import functools
import math

import numpy as np
import jax
import jax.numpy as jnp
from jax import lax
from jax.experimental import pallas as pl
from jax.experimental.pallas import tpu as pltpu

F32 = jnp.float32
BF16 = jnp.bfloat16
I32 = jnp.int32
I16 = jnp.int16

D_MODEL = 1024
HEAD_DIM = 64
BLK = 128
A_HEADS = 8
A_PATTERNS = ((128, 1), (512, 4), (2048, 16))
A_PAD = BLK * 16
B_HEADS = 8
B_KV_HEADS = 2
B_WINDOW = 128
C_HEADS = 16
IDX_HEADS = 8
IDX_DIM = 64
TOPK_MAX = 256
REL_BUCKETS = 32
REL_MAX_DIST = 2048
D_FF = 3584
N_EXPERTS = 8
TOP_K = 2
MOE_BLOCK = 512
DEPTH = 4
ALPHA = (2 * DEPTH) ** 0.25
LN_EPS = 1e-5

LANES = 128
SUBLANES = 8
D_TILES = D_MODEL // LANES
NEG = -(2.0 ** 100)
INT_MIN = -(2 ** 31)
HALF = 2 ** 15
VMEM_LIMIT = 56 * 1024 * 1024

FFN_TF = 1792
DSA_CK = 256
DSA_GK = 512
DSA_HPS = 4
DSA_SKEW = 2


def _bucket_np(dist):
    n = np.maximum(dist, 0)
    max_exact = REL_BUCKETS // 2
    nf = np.maximum(n, 1).astype(np.float32)
    large = max_exact + (np.log(nf / np.float32(max_exact)) / np.float32(math.log(REL_MAX_DIST / max_exact))
                         * np.float32(REL_BUCKETS - max_exact)).astype(np.int32)
    large = np.minimum(large, REL_BUCKETS - 1)
    return np.where(n < max_exact, n, large).astype(np.int32)


def _far_distance():
    b = _bucket_np(np.arange(0, 2 * REL_MAX_DIST))
    return int(np.max(np.nonzero(b != REL_BUCKETS - 1)[0])) + 1


FAR_DIST = _far_distance()
N_NEAR = -(-(FAR_DIST + BLK - 1) // BLK)


def _cparams(sem):
    return pltpu.CompilerParams(dimension_semantics=sem, vmem_limit_bytes=VMEM_LIMIT)


def _layer_norm(z, g, b):
    mu = jnp.mean(z, axis=-1, keepdims=True)
    zc = z - mu
    var = jnp.mean(zc * zc, axis=-1, keepdims=True)
    return zc * lax.rsqrt(var + LN_EPS) * g + b


def _ada_kernel(c_ref, w_ref, b_ref, o_ref):
    j = pl.program_id(1)
    c = c_ref[...]
    sc = c * jax.nn.sigmoid(c)
    mod = jnp.dot(sc, w_ref[0], preferred_element_type=F32, precision=lax.Precision.HIGHEST)
    o_ref[0] = mod + b_ref[0] + jnp.where(j >= 1, 1.0, 0.0)


def ada_modulation_all(c, ada_w, ada_b):
    B, D = c.shape
    n = ada_w.shape[0] * ada_w.shape[1]
    rows = -(-B // SUBLANES) * SUBLANES
    cp = jnp.pad(c, ((0, rows - B), (0, 0)))
    w = ada_w.reshape(n, D, 3 * D)
    b = ada_b.reshape(n, 1, 3 * D)
    out = pl.pallas_call(
        _ada_kernel,
        grid=(n, 3),
        in_specs=[pl.BlockSpec((rows, D), lambda l, j: (0, 0)),
                  pl.BlockSpec((1, D, D), lambda l, j: (l, 0, j)),
                  pl.BlockSpec((1, 1, D), lambda l, j: (l, 0, j))],
        out_specs=pl.BlockSpec((1, rows, D), lambda l, j: (l, 0, j)),
        out_shape=jax.ShapeDtypeStruct((n, rows, 3 * D), F32),
        compiler_params=_cparams(("arbitrary", "arbitrary")),
        name="ada_modulation",
    )(cp, w, b)
    return out[:, :B]


def _inproj_kernel(x_ref, sc_ref, sh_ref, w_ref, *o_refs, splits):
    h = (x_ref[0] * sc_ref[0] + sh_ref[0]).astype(BF16)
    for o_ref, (start, width) in zip(o_refs, splits):
        o_ref[0] = jnp.dot(h, w_ref[:, start:start + width],
                           preferred_element_type=F32).astype(o_ref.dtype)


def in_projection(x, scale, shift, w, groups, tm=512):
    B, S, D = x.shape
    splits, start = [], 0
    for width, _ in groups:
        splits.append((start, width))
        start += width
    assert start == w.shape[1] and S % tm == 0
    return pl.pallas_call(
        functools.partial(_inproj_kernel, splits=tuple(splits)),
        grid=(B, S // tm),
        in_specs=[pl.BlockSpec((1, tm, D), lambda b, i: (b, i, 0)),
                  pl.BlockSpec((1, 1, D), lambda b, i: (b, 0, 0)),
                  pl.BlockSpec((1, 1, D), lambda b, i: (b, 0, 0)),
                  pl.BlockSpec(w.shape, lambda b, i: (0, 0))],
        out_specs=[pl.BlockSpec((1, tm, width), lambda b, i: (b, i, 0)) for width, _ in groups],
        out_shape=[jax.ShapeDtypeStruct((B, S, width), dt) for width, dt in groups],
        compiler_params=_cparams(("parallel", "parallel")),
        name="in_projection",
    )(x, scale, shift, w)


def _band_kernel(sink_ref, q_ref, kp_ref, ko_ref, vp_ref, vo_ref, bias_ref, o_ref, lse_ref, *,
                 hq, hk, use_sinks):
    b = pl.program_id(2)
    q = q_ref[0]
    kk = jnp.concatenate([kp_ref[0], ko_ref[0]], axis=0)
    vv = jnp.concatenate([vp_ref[0], vo_ref[0]], axis=0)
    col = lax.broadcasted_iota(I32, (BLK, 2 * BLK), 1)
    first_mask = jnp.where(jnp.logical_and(b == 0, col < BLK), NEG, 0.0)
    group = hq // hk
    for h in range(hq):
        g = h // group
        qh = q[:, h * HEAD_DIM:(h + 1) * HEAD_DIM]
        kh = kk[:, g * HEAD_DIM:(g + 1) * HEAD_DIM]
        vh = vv[:, g * HEAD_DIM:(g + 1) * HEAD_DIM]
        s = lax.dot_general(qh, kh, (((1,), (1,)), ((), ())), preferred_element_type=F32)
        s = s + bias_ref[h] + first_mask
        m = jnp.max(s, axis=-1, keepdims=True)
        if use_sinks:
            m = jnp.maximum(m, sink_ref[h])
        p = jnp.exp(s - m)
        l = jnp.sum(p, axis=-1, keepdims=True)
        if use_sinks:
            l = l + jnp.exp(sink_ref[h] - m)
        o = jnp.dot(p.astype(BF16), vh, preferred_element_type=F32)
        o_ref[0, :, h * HEAD_DIM:(h + 1) * HEAD_DIM] = (o / l).astype(o_ref.dtype)
        lse_ref[0, 0, :, h:h + 1] = m + jnp.log(l)


def banded_attention(q, k, v, bias, sinks, out_dtype, dil=1):
    B, S, qc = q.shape
    kc = k.shape[2]
    hq = qc // HEAD_DIM
    hk = kc // HEAD_DIM
    ld = S // dil
    use_sinks = sinks is not None
    if sinks is None:
        sinks = jnp.zeros((hq,), F32)
    view = lambda t: t.reshape(B, ld, dil * t.shape[2])
    cur = lambda n, r, b, s: (n, b, r)
    prev = lambda n, r, b, s: (n, jnp.maximum(b - 1, 0), r)
    o, lse = pl.pallas_call(
        functools.partial(_band_kernel, hq=hq, hk=hk, use_sinks=use_sinks),
        grid_spec=pltpu.PrefetchScalarGridSpec(
            num_scalar_prefetch=1,
            grid=(B, dil, ld // BLK),
            in_specs=[pl.BlockSpec((1, BLK, qc), cur),
                      pl.BlockSpec((1, BLK, kc), prev),
                      pl.BlockSpec((1, BLK, kc), cur),
                      pl.BlockSpec((1, BLK, kc), prev),
                      pl.BlockSpec((1, BLK, kc), cur),
                      pl.BlockSpec(bias.shape, lambda n, r, b, s: (0, 0, 0))],
            out_specs=[pl.BlockSpec((1, BLK, qc), cur),
                       pl.BlockSpec((1, 1, BLK, hq), lambda n, r, b, s: (n, r, b, 0))]),
        out_shape=[jax.ShapeDtypeStruct((B, ld, dil * qc), out_dtype),
                   jax.ShapeDtypeStruct((B, dil, ld, hq), F32)],
        compiler_params=_cparams(("parallel", "parallel", "arbitrary")),
        name="banded_attention",
    )(sinks, view(q), view(k), view(k), view(v), view(v), bias)
    return o.reshape(B, S, qc), lse


def band_bias(table, dilation, max_dist):
    n = 4 * BLK
    j = np.arange(n)
    dist = BLK - np.where(j < 2 * BLK, j, j - n)
    allowed = (dist >= 0) & (dist <= max_dist)
    v = jnp.where(allowed[:, None], table[_bucket_np(dist * dilation)], NEG)
    v = jnp.transpose(v, (1, 0)).astype(F32)
    tile = jnp.tile(v, (1, BLK))[:, :BLK * (n - 1)].reshape(-1, BLK, n - 1)
    return tile[:, :, :2 * BLK]


def _combine_kernel(*refs, n_pat, heads):
    o_refs, l_refs, out_ref = refs[:n_pat], refs[n_pat:2 * n_pat], refs[2 * n_pat]
    lses = [r[0] for r in l_refs]
    m = functools.reduce(jnp.maximum, lses)
    es = [jnp.exp(l - m) for l in lses]
    tot = functools.reduce(lambda a, b: a + b, es)
    ws = [e / tot for e in es]
    for h in range(heads):
        sl = slice(h * HEAD_DIM, (h + 1) * HEAD_DIM)
        acc = ws[0][:, h:h + 1] * o_refs[0][0, :, sl]
        for p in range(1, n_pat):
            acc = acc + ws[p][:, h:h + 1] * o_refs[p][0, :, sl]
        out_ref[0, :, sl] = acc.astype(out_ref.dtype)


def combine_patterns(outs, lses, tm=512):
    B, S, C = outs[0].shape
    heads = C // HEAD_DIM
    n_pat = len(outs)
    idx = lambda b, i: (b, i, 0)
    return pl.pallas_call(
        functools.partial(_combine_kernel, n_pat=n_pat, heads=heads),
        grid=(B, S // tm),
        in_specs=[pl.BlockSpec((1, tm, C), idx)] * n_pat + [pl.BlockSpec((1, tm, heads), idx)] * n_pat,
        out_specs=pl.BlockSpec((1, tm, C), idx),
        out_shape=jax.ShapeDtypeStruct((B, S, C), BF16),
        compiler_params=_cparams(("parallel", "parallel")),
        name="combine_patterns",
    )(*outs, *lses)


def _outproj_kernel(*refs, n_parts):
    o_refs, w_refs = refs[:n_parts], refs[n_parts:2 * n_parts]
    x_ref, gate_ref, g_ref, b_ref, out_ref = refs[2 * n_parts:]
    mix = jnp.dot(o_refs[0][0], w_refs[0][...], preferred_element_type=F32)
    for o_ref, w_ref in zip(o_refs[1:], w_refs[1:]):
        mix = mix + jnp.dot(o_ref[0], w_ref[...], preferred_element_type=F32)
    z = ALPHA * x_ref[0] + gate_ref[0] * mix
    out_ref[0] = _layer_norm(z, g_ref[...], b_ref[...])


def out_projection_ln(parts, weights, x, gate, ln_g, ln_b, tm=512):
    B, S, D = x.shape
    idx = lambda b, i: (b, i, 0)
    const2 = lambda b, i: (0, 0)
    return pl.pallas_call(
        functools.partial(_outproj_kernel, n_parts=len(parts)),
        grid=(B, S // tm),
        in_specs=([pl.BlockSpec((1, tm, p.shape[2]), idx) for p in parts]
                  + [pl.BlockSpec(w.shape, const2) for w in weights]
                  + [pl.BlockSpec((1, tm, D), idx),
                     pl.BlockSpec((1, 1, D), lambda b, i: (b, 0, 0)),
                     pl.BlockSpec((1, D), const2),
                     pl.BlockSpec((1, D), const2)]),
        out_specs=pl.BlockSpec((1, tm, D), idx),
        out_shape=jax.ShapeDtypeStruct((B, S, D), F32),
        compiler_params=_cparams(("parallel", "parallel")),
        name="out_projection_ln",
    )(*parts, *weights, x, gate, ln_g.reshape(1, D), ln_b.reshape(1, D))


def _dsa_kernel(q_ref, qi_ref, wi_ref, kid_ref, kd_ref, va_ref, vb_ref, strip_ref, o_ref,
                skt_scr, hi_scr, lo_scr, qm_scr, qim_scr, mb_scr, s_scr, m_scr, acc_scr, *, k_sel):
    i = pl.program_id(1)
    ck = DSA_CK
    sub = ck // LANES
    gt = DSA_GK // LANES
    gc = DSA_GK // ck
    nbuf = DSA_SKEW + 1
    ngr = (i * BLK + BLK + DSA_GK - 1) // DSA_GK
    nck = ngr * gc

    half = C_HEADS // 2
    hps = DSA_HPS
    upper_r = lax.broadcasted_iota(I32, (LANES, BLK), 0) >= HEAD_DIM

    def slot_cols(slot):
        return slot // hps, slice((slot % hps) * BLK, (slot % hps + 1) * BLK)

    def slot_head(slot):
        return 2 * (slot % half) + slot // half

    for p in range(half):
        qt = q_ref[0, :, p * LANES:(p + 1) * LANES].astype(F32).T
        k0, c0 = slot_cols(p)
        k1, c1 = slot_cols(half + p)
        qm_scr[k0, :, c0] = jnp.where(upper_r, 0.0, qt).astype(BF16)
        qm_scr[k1, :, c1] = jnp.where(upper_r, qt, 0.0).astype(BF16)
    for p in range(IDX_HEADS // 2):
        qt = qi_ref[0, :, p * LANES:(p + 1) * LANES].astype(F32).T
        qim_scr[:, (2 * p) * BLK:(2 * p + 1) * BLK] = jnp.where(upper_r, 0.0, qt).astype(BF16)
        qim_scr[:, (2 * p + 1) * BLK:(2 * p + 2) * BLK] = jnp.where(upper_r, qt, 0.0).astype(BF16)
    wscale = IDX_HEADS ** -0.5 * IDX_DIM ** -0.5
    wt = wi_ref[0, 0] * wscale

    key_pos = lax.broadcasted_iota(I32, (ck, BLK), 0)
    qry_pos = lax.broadcasted_iota(I32, (ck, BLK), 1) + i * BLK

    def score_chunk(j, carry):
        kblk = kid_ref[0, j]
        sc = jnp.zeros((ck, BLK), F32)
        for p in range(IDX_HEADS // 2):
            d = jnp.dot(kblk, qim_scr[:, 2 * p * BLK:(2 * p + 2) * BLK], preferred_element_type=F32)
            for r in range(2):
                h = 2 * p + r
                sc = sc + jnp.maximum(d[:, r * BLK:(r + 1) * BLK], 0.0) * wt[h:h + 1]
        bits = pltpu.bitcast(sc, I32)
        key = bits ^ ((bits >> 31) & 0x7FFFFFFF)
        key = jnp.where(key_pos + j * ck <= qry_pos, key, INT_MIN)
        for u in range(sub):
            tile_t = key[u * LANES:(u + 1) * LANES]
            skt_scr[j * sub + u] = tile_t
            hi_scr[j * sub + u] = (tile_t >> 16).astype(I16)
            lo_scr[j * sub + u] = ((tile_t & 0xFFFF) - HALF).astype(I16)
        return carry

    lax.fori_loop(0, nck, score_chunk, 0)

    packed = 2 * SUBLANES

    def count(scr, cond):
        def group(g, acc):
            parts = []
            for u in range(gt):
                hit = jnp.where(cond(scr[g * gt + u]), jnp.int16(1), jnp.int16(0))
                parts += [hit[r:r + packed] for r in range(0, LANES, packed)]
            while len(parts) > 1:
                parts = [a + b for a, b in zip(parts[0::2], parts[1::2])]
            return acc + parts[0]
        acc = lax.fori_loop(0, ngr, group, jnp.zeros((packed, LANES), I16))
        return jnp.sum(acc.astype(I32), axis=0, keepdims=True)

    def select16(scr, target):
        def bit_pass(t, v):
            c = v | lax.shift_left(jnp.int32(1), 15 - t)
            c16 = (c - HALF).astype(I16)
            return jnp.where(count(scr, lambda x: x >= c16) >= target, c, v)
        return lax.fori_loop(0, 16, bit_pass, jnp.zeros((1, LANES), I32))

    v_hi = select16(hi_scr, k_sel)
    hi16 = (v_hi - HALF).astype(I16)
    above = count(hi_scr, lambda x: x > hi16)
    def keep_low(g, carry):
        for u in range(gt):
            idx = g * gt + u
            lo_scr[idx] = jnp.where(hi_scr[idx] == hi16, lo_scr[idx], jnp.int16(-HALF))
        return carry
    lax.fori_loop(0, ngr, keep_low, 0)
    v_lo = select16(lo_scr, k_sel - above)
    thr_q = jnp.maximum(lax.shift_left(v_hi - HALF, 16) | v_lo, INT_MIN + 1)

    m_scr[...] = jnp.full(m_scr.shape, NEG, F32)
    acc_scr[...] = jnp.zeros(acc_scr.shape, F32)

    def attend_group(g, carry):
        for u in range(gt):
            mb = jnp.where(skt_scr[g * gt + u] >= thr_q, 0.0, NEG).astype(BF16)
            for r in range(hps):
                mb_scr[u, :, r * BLK:(r + 1) * BLK] = mb
        tiles = [jnp.clip(i - (g * gt + u), 0, N_NEAR) for u in range(gt)]

        def logits(k):
            heads = [slot_head(k * hps + r) for r in range(hps)]
            s = jnp.dot(kd_ref[0, g], qm_scr[k], preferred_element_type=F32).astype(BF16)
            mx = None
            for t in range(gt):
                bias = jnp.concatenate([strip_ref[h, tiles[t]] for h in heads], axis=1)
                piece = s[t * BLK:(t + 1) * BLK] + mb_scr[t] + bias
                s_scr[k % nbuf, t] = piece
                mx = piece if mx is None else jnp.maximum(mx, piece)
            m_old = m_scr[k]
            m_new = jnp.maximum(m_old, jnp.max(mx.astype(F32), axis=0, keepdims=True))
            m_scr[k] = m_new
            return m_old, m_new

        def accumulate(k, m_old, m_new):
            m16 = m_new.astype(BF16)
            p = jnp.concatenate([jnp.exp(s_scr[k % nbuf, t] - m16) for t in range(gt)], axis=0)
            v_ref = va_ref if k * hps < half else vb_ref
            pv = jnp.dot(v_ref[0, g], p, preferred_element_type=F32)
            acc_scr[k] = jnp.exp(m_old - m_new) * acc_scr[k] + pv

        nstk = C_HEADS // hps
        stats = [logits(k) for k in range(DSA_SKEW)]
        for k in range(nstk):
            if k + DSA_SKEW < nstk:
                stats.append(logits(k + DSA_SKEW))
            accumulate(k, *stats[k])
        return carry

    lax.fori_loop(0, ngr, attend_group, 0)

    for p in range(half):
        k0, c0 = slot_cols(p)
        k1, c1 = slot_cols(half + p)
        a0 = acc_scr[k0, :, c0]
        a1 = acc_scr[k1, :, c1]
        ot = jnp.where(upper_r, a1 / pltpu.roll(a1, HEAD_DIM, 0), a0 / pltpu.roll(a0, HEAD_DIM, 0))
        o_ref[0, :, p * LANES:(p + 1) * LANES] = ot.T.astype(o_ref.dtype)


def dsa_attention(q, qi, wi, ki, kc, vc, table):
    B, S, _ = q.shape
    ck = DSA_CK
    assert S % DSA_GK == 0
    nchunks = S // ck
    k_sel = min(TOPK_MAX, S // 4)

    ngroups = S // DSA_GK
    ones = jnp.ones_like(vc)
    kid = jnp.concatenate([ki, ki], axis=-1).reshape(B, nchunks, ck, LANES)
    kd = jnp.concatenate([kc, kc], axis=-1).reshape(B, ngroups, DSA_GK, LANES)
    wit = jnp.transpose(wi.reshape(B, S // BLK, BLK, IDX_HEADS), (0, 1, 3, 2))

    def grouped_t(t):
        return jnp.transpose(t.reshape(B, ngroups, DSA_GK, LANES), (0, 1, 3, 2))

    va = grouped_t(jnp.concatenate([vc, ones], axis=-1))
    vb = grouped_t(jnp.concatenate([ones, vc], axis=-1))
    j = np.arange(2 * BLK)
    offs = np.where(j <= BLK, -j, 2 * BLK - j)
    v = table[_bucket_np(np.arange(N_NEAR)[:, None] * BLK + offs[None, :])]
    v = jnp.transpose(v, (2, 0, 1)).astype(F32)
    near = jnp.tile(v, (1, 1, BLK))[..., :BLK * (2 * BLK - 1)]
    near = near.reshape(C_HEADS, N_NEAR, BLK, 2 * BLK - 1)[..., :BLK]
    near = jnp.swapaxes(near, 2, 3)
    far = jnp.broadcast_to(table[REL_BUCKETS - 1].astype(F32)[:, None, None, None],
                           (C_HEADS, 1, BLK, LANES))
    strip = jnp.concatenate([near, far], axis=1).astype(BF16)

    once = pl.Buffered(1)
    qblk = lambda b, i: (b, i, 0)
    per_b = lambda b, i: (b, 0, 0, 0)
    return pl.pallas_call(
        functools.partial(_dsa_kernel, k_sel=k_sel),
        grid=(B, S // BLK),
        in_specs=[pl.BlockSpec((1, BLK, C_HEADS * HEAD_DIM), qblk),
                  pl.BlockSpec((1, BLK, IDX_HEADS * IDX_DIM), qblk),
                  pl.BlockSpec((1, 1, IDX_HEADS, BLK), lambda b, i: (b, i, 0, 0)),
                  pl.BlockSpec((1, nchunks, ck, LANES), per_b, pipeline_mode=once),
                  pl.BlockSpec((1, ngroups, DSA_GK, LANES), per_b, pipeline_mode=once),
                  pl.BlockSpec((1, ngroups, LANES, DSA_GK), per_b, pipeline_mode=once),
                  pl.BlockSpec((1, ngroups, LANES, DSA_GK), per_b, pipeline_mode=once),
                  pl.BlockSpec(strip.shape, lambda b, i: (0, 0, 0, 0), pipeline_mode=once)],
        out_specs=pl.BlockSpec((1, BLK, C_HEADS * HEAD_DIM), qblk),
        scratch_shapes=[pltpu.VMEM((S // LANES, LANES, BLK), I32),
                        pltpu.VMEM((S // LANES, LANES, BLK), I16),
                        pltpu.VMEM((S // LANES, LANES, BLK), I16),
                        pltpu.VMEM((C_HEADS // DSA_HPS, LANES, DSA_HPS * BLK), BF16),
                        pltpu.VMEM((LANES, IDX_HEADS * BLK), BF16),
                        pltpu.VMEM((DSA_GK // LANES, LANES, DSA_HPS * BLK), BF16),
                        pltpu.VMEM((DSA_SKEW + 1, DSA_GK // LANES, LANES, DSA_HPS * BLK), BF16),
                        pltpu.VMEM((C_HEADS // DSA_HPS, 1, DSA_HPS * BLK), F32),
                        pltpu.VMEM((C_HEADS // DSA_HPS, LANES, DSA_HPS * BLK), F32)],
        out_shape=jax.ShapeDtypeStruct((B, S, C_HEADS * HEAD_DIM), BF16),
        compiler_params=_cparams(("parallel", "arbitrary")),
        name="dsa_attention",
    )(q, qi, wit, kid, kd, va, vb, strip)


def _ffn_step(x2d, w1_ref, w3_ref, w2_ref, acc):
    xb = x2d[...]
    h1 = jnp.dot(xb, w1_ref[0], preferred_element_type=F32)
    h3 = jnp.dot(xb, w3_ref[0], preferred_element_type=F32)
    a = (h1 * jax.nn.sigmoid(h1) * h3).astype(BF16)
    acc[...] += jnp.dot(a, w2_ref[0], preferred_element_type=F32)


def _ffn_dense_kernel(x_ref, sc_ref, sh_ref, gate_ref, g_ref, b_ref, w1_ref, w3_ref, w2_ref, out_ref,
                      x2d, acc, *, nf):
    f = pl.program_id(2)

    @pl.when(f == 0)
    def _():
        x2d[...] = (x_ref[0] * sc_ref[0] + sh_ref[0]).astype(BF16)
        acc[...] = jnp.zeros(acc.shape, F32)

    _ffn_step(x2d, w1_ref, w3_ref, w2_ref, acc)

    @pl.when(f == nf - 1)
    def _():
        z = ALPHA * x_ref[0] + gate_ref[0] * acc[...]
        out_ref[0] = _layer_norm(z, g_ref[...], b_ref[...])


def ffn_dense_ln(x, scale, shift, gate, ln_g, ln_b, w1, w3, w2, tm=512, tf=FFN_TF):
    B, S, D = x.shape
    nf = D_FF // tf
    xi = lambda b, i, f: (b, i, 0)
    bi = lambda b, i, f: (b, 0, 0)
    c2 = lambda b, i, f: (0, 0)
    return pl.pallas_call(
        functools.partial(_ffn_dense_kernel, nf=nf),
        grid=(B, S // tm, nf),
        in_specs=[pl.BlockSpec((1, tm, D), xi),
                  pl.BlockSpec((1, 1, D), bi), pl.BlockSpec((1, 1, D), bi), pl.BlockSpec((1, 1, D), bi),
                  pl.BlockSpec((1, D), c2), pl.BlockSpec((1, D), c2),
                  pl.BlockSpec((1, D, tf), lambda b, i, f: (0, 0, f)),
                  pl.BlockSpec((1, D, tf), lambda b, i, f: (0, 0, f)),
                  pl.BlockSpec((1, tf, D), lambda b, i, f: (0, f, 0))],
        out_specs=pl.BlockSpec((1, tm, D), xi),
        out_shape=jax.ShapeDtypeStruct((B, S, D), F32),
        scratch_shapes=[pltpu.VMEM((tm, D), BF16), pltpu.VMEM((tm, D), F32)],
        compiler_params=_cparams(("parallel", "parallel", "arbitrary")),
        name="ffn_dense_ln",
    )(x, scale, shift, gate, ln_g.reshape(1, D), ln_b.reshape(1, D), w1, w3, w2)


def _ffn_expert_kernel(be_ref, tok_ref, tok_next_ref, dst_ref, h_hbm, w1_ref, w3_ref, w2_ref, y_hbm,
                       xbuf, ybuf, x2d, acc, gsem, ssem, *, nf, tm):
    i = pl.program_id(0)
    f = pl.program_id(1)
    n_blocks = pl.num_programs(0)
    slot = i % 2
    rows = tm * D_TILES

    def start_gather(idx_ref, s):
        def body(r, carry):
            src = pl.multiple_of(idx_ref[0, 0, r] * D_TILES, D_TILES)
            dst = pl.multiple_of(r * D_TILES, D_TILES)
            pltpu.make_async_copy(h_hbm.at[pl.ds(src, D_TILES)], xbuf.at[s, pl.ds(dst, D_TILES)],
                                  gsem.at[s]).start()
            return carry
        lax.fori_loop(0, tm, body, 0, unroll=8)

    def wait_gather(s):
        pltpu.make_async_copy(h_hbm.at[pl.ds(0, rows)], xbuf.at[s], gsem.at[s]).wait()

    def start_scatter():
        def body(r, carry):
            src = pl.multiple_of(r * D_TILES, D_TILES)
            dst = pl.multiple_of(dst_ref[0, 0, r] * D_TILES, D_TILES)
            pltpu.make_async_copy(ybuf.at[pl.ds(src, D_TILES)], y_hbm.at[pl.ds(dst, D_TILES)], ssem).start()
            return carry
        lax.fori_loop(0, tm, body, 0, unroll=8)

    def wait_scatter():
        pltpu.make_async_copy(ybuf, y_hbm.at[pl.ds(0, rows)], ssem).wait()

    @pl.when(f == 0)
    def _():
        @pl.when(i == 0)
        def _():
            start_gather(tok_ref, 0)

        @pl.when(i + 1 < n_blocks)
        def _():
            start_gather(tok_next_ref, 1 - slot)

        wait_gather(slot)
        for j in range(D_TILES):
            x2d[:, j * LANES:(j + 1) * LANES] = xbuf[slot, pl.ds(j, tm, stride=D_TILES), :].astype(BF16)
        acc[...] = jnp.zeros(acc.shape, F32)

    _ffn_step(x2d, w1_ref, w3_ref, w2_ref, acc)

    @pl.when(f == nf - 1)
    def _():
        @pl.when(i > 0)
        def _():
            wait_scatter()

        for j in range(D_TILES):
            ybuf[pl.ds(j, tm, stride=D_TILES), :] = acc[:, j * LANES:(j + 1) * LANES]
        start_scatter()

        @pl.when(i == n_blocks - 1)
        def _():
            wait_scatter()


def ffn_experts(h_tiles, slot_tok, slot_dst, block_expert, w1, w3, w2, tf=FFN_TF):
    tm = MOE_BLOCK
    P = slot_tok.shape[0]
    n_blocks = P // tm
    nf = D_FF // tf
    tok3 = slot_tok.reshape(n_blocks, 1, tm)
    dst3 = slot_dst.reshape(n_blocks, 1, tm)
    cur = lambda i, f, be: (i, 0, 0)
    nxt = lambda i, f, be: (jnp.minimum(i + 1, n_blocks - 1), 0, 0)
    return pl.pallas_call(
        functools.partial(_ffn_expert_kernel, nf=nf, tm=tm),
        grid_spec=pltpu.PrefetchScalarGridSpec(
            num_scalar_prefetch=1,
            grid=(n_blocks, nf),
            in_specs=[pl.BlockSpec((1, 1, tm), cur, memory_space=pltpu.SMEM),
                      pl.BlockSpec((1, 1, tm), nxt, memory_space=pltpu.SMEM),
                      pl.BlockSpec((1, 1, tm), cur, memory_space=pltpu.SMEM),
                      pl.BlockSpec(memory_space=pl.ANY),
                      pl.BlockSpec((1, D_MODEL, tf), lambda i, f, be: (be[i], 0, f)),
                      pl.BlockSpec((1, D_MODEL, tf), lambda i, f, be: (be[i], 0, f)),
                      pl.BlockSpec((1, tf, D_MODEL), lambda i, f, be: (be[i], f, 0))],
            out_specs=pl.BlockSpec(memory_space=pl.ANY),
            scratch_shapes=[pltpu.VMEM((2, tm * D_TILES, LANES), F32),
                            pltpu.VMEM((tm * D_TILES, LANES), F32),
                            pltpu.VMEM((tm, D_MODEL), BF16),
                            pltpu.VMEM((tm, D_MODEL), F32),
                            pltpu.SemaphoreType.DMA((2,)),
                            pltpu.SemaphoreType.DMA(())]),
        out_shape=jax.ShapeDtypeStruct((P * D_TILES, LANES), F32),
        compiler_params=_cparams(("arbitrary", "arbitrary")),
        name="ffn_experts",
    )(block_expert, tok3, tok3, dst3, h_tiles, w1, w3, w2)


def _router_kernel(x_ref, sc_ref, sh_ref, wr_ref, h_ref, idx_ref, gate_ref, *, tm):
    h = x_ref[0] * sc_ref[0] + sh_ref[0]
    for j in range(D_TILES):
        h_ref[pl.ds(j, tm, stride=D_TILES), :] = h[:, j * LANES:(j + 1) * LANES]
    logits = jnp.dot(h, wr_ref[...], preferred_element_type=F32, precision=lax.Precision.HIGHEST)
    lane = lax.broadcasted_iota(I32, logits.shape, 1)
    lg = jnp.where(lane < N_EXPERTS, logits, -jnp.inf)
    m1 = jnp.max(lg, axis=1, keepdims=True)
    i1 = jnp.min(jnp.where(lg == m1, lane, LANES), axis=1, keepdims=True)
    lg2 = jnp.where(lane == i1, -jnp.inf, lg)
    m2 = jnp.max(lg2, axis=1, keepdims=True)
    i2 = jnp.min(jnp.where(lg2 == m2, lane, LANES), axis=1, keepdims=True)
    e = jnp.exp(m2 - m1)
    idx_ref[:, 0:1] = i1
    idx_ref[:, 1:2] = i2
    gate_ref[:, 0:1] = 1.0 / (1.0 + e)
    gate_ref[:, 1:2] = e / (1.0 + e)


def route_tokens(x, scale, shift, w_router, tm=512):
    B, S, D = x.shape
    T = B * S
    nb = S // tm
    wr = jnp.pad(w_router, ((0, 0), (0, LANES - N_EXPERTS)))
    return pl.pallas_call(
        functools.partial(_router_kernel, tm=tm),
        grid=(B, nb),
        in_specs=[pl.BlockSpec((1, tm, D), lambda b, i: (b, i, 0)),
                  pl.BlockSpec((1, 1, D), lambda b, i: (b, 0, 0)),
                  pl.BlockSpec((1, 1, D), lambda b, i: (b, 0, 0)),
                  pl.BlockSpec((D, LANES), lambda b, i: (0, 0))],
        out_specs=[pl.BlockSpec((tm * D_TILES, LANES), lambda b, i: (b * nb + i, 0)),
                   pl.BlockSpec((tm, TOP_K), lambda b, i: (b * nb + i, 0)),
                   pl.BlockSpec((tm, TOP_K), lambda b, i: (b * nb + i, 0))],
        out_shape=[jax.ShapeDtypeStruct((T * D_TILES, LANES), F32),
                   jax.ShapeDtypeStruct((T, TOP_K), I32),
                   jax.ShapeDtypeStruct((T, TOP_K), F32)],
        compiler_params=_cparams(("parallel", "parallel")),
        name="route_tokens",
    )(x, scale, shift, wr)


def _moe_combine_kernel(y_ref, gt_ref, x_ref, gate_ref, g_ref, b_ref, out_ref, *, tm):
    g0 = gt_ref[:, 0:1]
    g1 = gt_ref[:, 1:2]
    stride = TOP_K * D_TILES
    pieces = []
    for j in range(D_TILES):
        sl = slice(j * LANES, (j + 1) * LANES)
        ff = g0 * y_ref[pl.ds(j, tm, stride=stride), :] + g1 * y_ref[pl.ds(D_TILES + j, tm, stride=stride), :]
        pieces.append(ALPHA * x_ref[0, :, sl] + gate_ref[0, :, sl] * ff)
    z = jnp.concatenate(pieces, axis=1)
    out_ref[0] = _layer_norm(z, g_ref[...], b_ref[...])


def moe_combine_ln(yg, gates, x, gate, ln_g, ln_b, tm=512):
    B, S, D = x.shape
    nb = S // tm
    return pl.pallas_call(
        functools.partial(_moe_combine_kernel, tm=tm),
        grid=(B, nb),
        in_specs=[pl.BlockSpec((tm * TOP_K * D_TILES, LANES), lambda b, i: (b * nb + i, 0)),
                  pl.BlockSpec((tm, TOP_K), lambda b, i: (b * nb + i, 0)),
                  pl.BlockSpec((1, tm, D), lambda b, i: (b, i, 0)),
                  pl.BlockSpec((1, 1, D), lambda b, i: (b, 0, 0)),
                  pl.BlockSpec((1, D), lambda b, i: (0, 0)),
                  pl.BlockSpec((1, D), lambda b, i: (0, 0))],
        out_specs=pl.BlockSpec((1, tm, D), lambda b, i: (b, i, 0)),
        out_shape=jax.ShapeDtypeStruct((B, S, D), F32),
        compiler_params=_cparams(("parallel", "parallel")),
        name="moe_combine_ln",
    )(yg, gates, x, gate, ln_g.reshape(1, D), ln_b.reshape(1, D))


def _scale_cols(w, start, width, factor):
    return w.at[:, start:start + width].multiply(factor)


def even_mixer_layer(x, mod, w_in, w_out, rel_table, sinks, ln_g, ln_b):
    B, S, D = x.shape
    assert S % A_PAD == 0
    shift, scale, gate = mod
    ah, bh, bk = A_HEADS * HEAD_DIM, B_HEADS * HEAD_DIM, B_KV_HEADS * HEAD_DIM
    w = _scale_cols(w_in, 0, ah, HEAD_DIM ** -0.5)
    w = _scale_cols(w, 3 * ah, bh, HEAD_DIM ** -0.5).astype(BF16)
    groups = [(ah, BF16)] * 3 + [(bh, BF16), (bk, BF16), (bk, BF16)]
    qa, ka, va, qb, kb, vb = in_projection(x, scale, shift, w, groups)

    outs, lses = [], []
    for window, d in A_PATTERNS:
        bias = band_bias(rel_table[:, :A_HEADS], d, window // d)
        o, lse = banded_attention(qa, ka, va, bias, None, F32, dil=d)
        outs.append(o)
        lses.append(jnp.transpose(lse, (0, 2, 1, 3)).reshape(B, S, A_HEADS))
    oa = combine_patterns(outs, lses)

    bias_b = band_bias(rel_table[:, A_HEADS:A_HEADS + B_HEADS], 1, B_WINDOW - 1)
    ob, _ = banded_attention(qb, kb, vb, bias_b, sinks.astype(F32), BF16)

    wo = w_out.astype(BF16)
    return out_projection_ln([oa, ob], [wo[:ah], wo[ah:]], x, gate, ln_g, ln_b)


def dsa_mixer_layer(x, mod, w_in, w_out, rel_table, ln_g, ln_b):
    shift, scale, gate = mod
    qw = C_HEADS * HEAD_DIM
    iw = IDX_HEADS * IDX_DIM
    q_w, kc_w, vc_w, qi_w, ki_w, wi_w = jnp.split(
        w_in, [qw, qw + HEAD_DIM, qw + 2 * HEAD_DIM, qw + 2 * HEAD_DIM + iw, qw + 2 * HEAD_DIM + iw + IDX_DIM],
        axis=1)
    pad = jnp.zeros((D_MODEL, LANES - IDX_DIM - IDX_HEADS), w_in.dtype)
    w = jnp.concatenate([q_w * HEAD_DIM ** -0.5, kc_w, vc_w, qi_w, ki_w, wi_w, pad], axis=1).astype(BF16)
    groups = [(qw, BF16), (2 * HEAD_DIM, BF16), (iw, BF16), (LANES, F32)]
    q, kv, qi, kw = in_projection(x, scale, shift, w, groups)
    kc, vc = kv[..., :HEAD_DIM], kv[..., HEAD_DIM:]
    ki = kw[..., :IDX_DIM].astype(BF16)
    wi = kw[..., IDX_DIM:IDX_DIM + IDX_HEADS]
    o = dsa_attention(q, qi, wi, ki, kc, vc, rel_table[:, :C_HEADS])
    return out_projection_ln([o], [w_out.astype(BF16)], x, gate, ln_g, ln_b)


def moe_layer(x, mod, w_router, w1, w3, w2, ln_g, ln_b):
    B, S, D = x.shape
    shift, scale, gate = mod
    T = B * S
    A = T * TOP_K
    h_tiles, top_idx, gates = route_tokens(x, scale, shift, w_router)

    e_flat = top_idx.reshape(-1)
    onehot = (e_flat[:, None] == jnp.arange(N_EXPERTS)[None, :]).astype(I32)
    counts = jnp.sum(onehot, axis=0)
    padded = (counts + MOE_BLOCK - 1) // MOE_BLOCK * MOE_BLOCK
    pends = jnp.cumsum(padded)
    dest = jnp.sum(onehot * (jnp.cumsum(onehot, axis=0) - onehot + (pends - padded)[None, :]), axis=1)
    n_blocks = -(-A // MOE_BLOCK) + N_EXPERTS
    P = n_blocks * MOE_BLOCK
    slot_src = jnp.full((P,), -1, I32).at[dest].set(jnp.arange(A, dtype=I32))
    is_pad = slot_src < 0
    slot_tok = jnp.where(is_pad, 0, slot_src // TOP_K)
    slot_dst = jnp.where(is_pad, A - 1 + jnp.cumsum(is_pad.astype(I32)), slot_src)
    block_expert = jnp.minimum(
        jnp.searchsorted(pends, jnp.arange(n_blocks) * MOE_BLOCK, side='right'), N_EXPERTS - 1).astype(I32)

    yg = ffn_experts(h_tiles, slot_tok, slot_dst, block_expert,
                     w1.astype(BF16), w3.astype(BF16), w2.astype(BF16))
    return moe_combine_ln(yg, gates, x, gate, ln_g, ln_b)


def kernel(x, c, rel_table, w_in_even, w_out_even, sinks, w_in_odd, w_out_odd, ffn_w1, ffn_w3, ffn_w2,
           router, exp_w1, exp_w3, exp_w2, ada_w, ada_b, ln_g, ln_b):
    D = D_MODEL
    mods = ada_modulation_all(c, ada_w, ada_b)

    def mod(layer, sub):
        m = mods[2 * layer + sub]
        return m[:, None, :D], m[:, None, D:2 * D], m[:, None, 2 * D:]

    for layer in range(DEPTH):
        i = layer // 2
        if layer % 2 == 0:
            x = even_mixer_layer(x, mod(layer, 0), w_in_even[i], w_out_even[i], rel_table, sinks[i],
                                 ln_g[layer, 0], ln_b[layer, 0])
            shift, scale, gate = mod(layer, 1)
            x = ffn_dense_ln(x, scale, shift, gate, ln_g[layer, 1], ln_b[layer, 1],
                             ffn_w1[i][None].astype(BF16), ffn_w3[i][None].astype(BF16),
                             ffn_w2[i][None].astype(BF16))
        else:
            x = dsa_mixer_layer(x, mod(layer, 0), w_in_odd[i], w_out_odd[i], rel_table,
                                ln_g[layer, 0], ln_b[layer, 0])
            x = moe_layer(x, mod(layer, 1), router[i], exp_w1[i], exp_w3[i], exp_w2[i],
                          ln_g[layer, 1], ln_b[layer, 1])
    return x
```

```python
import functools
import math

import numpy as np
import jax
import jax.numpy as jnp
from jax import lax
from jax.experimental import pallas as pl
from jax.experimental.pallas import tpu as pltpu

F32 = jnp.float32
BF16 = jnp.bfloat16
I32 = jnp.int32
I16 = jnp.int16

D_MODEL = 1024
HEAD_DIM = 64
BLK = 128
A_HEADS = 8
A_PATTERNS = ((128, 1), (512, 4), (2048, 16))
A_PAD = BLK * 16
B_HEADS = 8
B_KV_HEADS = 2
B_WINDOW = 128
C_HEADS = 16
IDX_HEADS = 8
IDX_DIM = 64
TOPK_MAX = 256
REL_BUCKETS = 32
REL_MAX_DIST = 2048
D_FF = 3584
N_EXPERTS = 8
TOP_K = 2
MOE_BLOCK = 512
DEPTH = 4
ALPHA = (2 * DEPTH) ** 0.25
LN_EPS = 1e-5

LANES = 128
SUBLANES = 8
D_TILES = D_MODEL // LANES
NEG = -(2.0 ** 100)
INT_MIN = -(2 ** 31)
HALF = 2 ** 15
VMEM_LIMIT = 56 * 1024 * 1024

FFN_TF = 1792
DSA_CK = 512
DSA_GK = 512
DSA_HPS = 4
DSA_SKEW = 2


def _bucket_np(dist):
    n = np.maximum(dist, 0)
    max_exact = REL_BUCKETS // 2
    nf = np.maximum(n, 1).astype(np.float32)
    large = max_exact + (np.log(nf / np.float32(max_exact)) / np.float32(math.log(REL_MAX_DIST / max_exact))
                         * np.float32(REL_BUCKETS - max_exact)).astype(np.int32)
    large = np.minimum(large, REL_BUCKETS - 1)
    return np.where(n < max_exact, n, large).astype(np.int32)


def _far_distance():
    b = _bucket_np(np.arange(0, 2 * REL_MAX_DIST))
    return int(np.max(np.nonzero(b != REL_BUCKETS - 1)[0])) + 1


FAR_DIST = _far_distance()
N_NEAR = -(-(FAR_DIST + BLK - 1) // BLK)


def _cparams(sem):
    return pltpu.CompilerParams(dimension_semantics=sem, vmem_limit_bytes=VMEM_LIMIT)


def _layer_norm(z, g, b):
    mu = jnp.mean(z, axis=-1, keepdims=True)
    zc = z - mu
    var = jnp.mean(zc * zc, axis=-1, keepdims=True)
    return zc * lax.rsqrt(var + LN_EPS) * g + b


def _ada_kernel(c_ref, w_ref, b_ref, o_ref):
    j = pl.program_id(1)
    c = c_ref[...]
    sc = c * jax.nn.sigmoid(c)
    mod = jnp.dot(sc, w_ref[0], preferred_element_type=F32, precision=lax.Precision.HIGHEST)
    o_ref[0] = mod + b_ref[0] + jnp.where(j >= 1, 1.0, 0.0)


def ada_modulation_all(c, ada_w, ada_b):
    B, D = c.shape
    n = ada_w.shape[0] * ada_w.shape[1]
    rows = -(-B // SUBLANES) * SUBLANES
    cp = jnp.pad(c, ((0, rows - B), (0, 0)))
    w = ada_w.reshape(n, D, 3 * D)
    b = ada_b.reshape(n, 1, 3 * D)
    out = pl.pallas_call(
        _ada_kernel,
        grid=(n, 3),
        in_specs=[pl.BlockSpec((rows, D), lambda l, j: (0, 0)),
                  pl.BlockSpec((1, D, D), lambda l, j: (l, 0, j)),
                  pl.BlockSpec((1, 1, D), lambda l, j: (l, 0, j))],
        out_specs=pl.BlockSpec((1, rows, D), lambda l, j: (l, 0, j)),
        out_shape=jax.ShapeDtypeStruct((n, rows, 3 * D), F32),
        compiler_params=_cparams(("arbitrary", "arbitrary")),
        name="ada_modulation",
    )(cp, w, b)
    return out[:, :B]


def _inproj_kernel(x_ref, sc_ref, sh_ref, w_ref, *o_refs, splits):
    h = (x_ref[0] * sc_ref[0] + sh_ref[0]).astype(BF16)
    for o_ref, (start, width) in zip(o_refs, splits):
        o_ref[0] = jnp.dot(h, w_ref[:, start:start + width],
                           preferred_element_type=F32).astype(o_ref.dtype)


def in_projection(x, scale, shift, w, groups, tm=512):
    B, S, D = x.shape
    splits, start = [], 0
    for width, _ in groups:
        splits.append((start, width))
        start += width
    assert start == w.shape[1] and S % tm == 0
    return pl.pallas_call(
        functools.partial(_inproj_kernel, splits=tuple(splits)),
        grid=(B, S // tm),
        in_specs=[pl.BlockSpec((1, tm, D), lambda b, i: (b, i, 0)),
                  pl.BlockSpec((1, 1, D), lambda b, i: (b, 0, 0)),
                  pl.BlockSpec((1, 1, D), lambda b, i: (b, 0, 0)),
                  pl.BlockSpec(w.shape, lambda b, i: (0, 0))],
        out_specs=[pl.BlockSpec((1, tm, width), lambda b, i: (b, i, 0)) for width, _ in groups],
        out_shape=[jax.ShapeDtypeStruct((B, S, width), dt) for width, dt in groups],
        compiler_params=_cparams(("parallel", "parallel")),
        name="in_projection",
    )(x, scale, shift, w)


def _band_kernel(sink_ref, q_ref, kp_ref, ko_ref, vp_ref, vo_ref, bias_ref, o_ref, lse_ref, *,
                 hq, hk, use_sinks):
    b = pl.program_id(2)
    q = q_ref[0]
    kk = jnp.concatenate([kp_ref[0], ko_ref[0]], axis=0)
    vv = jnp.concatenate([vp_ref[0], vo_ref[0]], axis=0)
    col = lax.broadcasted_iota(I32, (BLK, 2 * BLK), 1)
    first_mask = jnp.where(jnp.logical_and(b == 0, col < BLK), NEG, 0.0)
    group = hq // hk
    for h in range(hq):
        g = h // group
        qh = q[:, h * HEAD_DIM:(h + 1) * HEAD_DIM]
        kh = kk[:, g * HEAD_DIM:(g + 1) * HEAD_DIM]
        vh = vv[:, g * HEAD_DIM:(g + 1) * HEAD_DIM]
        s = lax.dot_general(qh, kh, (((1,), (1,)), ((), ())), preferred_element_type=F32)
        s = s + bias_ref[h] + first_mask
        m = jnp.max(s, axis=-1, keepdims=True)
        if use_sinks:
            m = jnp.maximum(m, sink_ref[h])
        p = jnp.exp(s - m)
        l = jnp.sum(p, axis=-1, keepdims=True)
        if use_sinks:
            l = l + jnp.exp(sink_ref[h] - m)
        o = jnp.dot(p.astype(BF16), vh, preferred_element_type=F32)
        o_ref[0, :, h * HEAD_DIM:(h + 1) * HEAD_DIM] = (o / l).astype(o_ref.dtype)
        lse_ref[0, 0, :, h:h + 1] = m + jnp.log(l)


def banded_attention(q, k, v, bias, sinks, out_dtype, dil=1):
    B, S, qc = q.shape
    kc = k.shape[2]
    hq = qc // HEAD_DIM
    hk = kc // HEAD_DIM
    ld = S // dil
    use_sinks = sinks is not None
    if sinks is None:
        sinks = jnp.zeros((hq,), F32)
    view = lambda t: t.reshape(B, ld, dil * t.shape[2])
    cur = lambda n, r, b, s: (n, b, r)
    prev = lambda n, r, b, s: (n, jnp.maximum(b - 1, 0), r)
    o, lse = pl.pallas_call(
        functools.partial(_band_kernel, hq=hq, hk=hk, use_sinks=use_sinks),
        grid_spec=pltpu.PrefetchScalarGridSpec(
            num_scalar_prefetch=1,
            grid=(B, dil, ld // BLK),
            in_specs=[pl.BlockSpec((1, BLK, qc), cur),
                      pl.BlockSpec((1, BLK, kc), prev),
                      pl.BlockSpec((1, BLK, kc), cur),
                      pl.BlockSpec((1, BLK, kc), prev),
                      pl.BlockSpec((1, BLK, kc), cur),
                      pl.BlockSpec(bias.shape, lambda n, r, b, s: (0, 0, 0))],
            out_specs=[pl.BlockSpec((1, BLK, qc), cur),
                       pl.BlockSpec((1, 1, BLK, hq), lambda n, r, b, s: (n, r, b, 0))]),
        out_shape=[jax.ShapeDtypeStruct((B, ld, dil * qc), out_dtype),
                   jax.ShapeDtypeStruct((B, dil, ld, hq), F32)],
        compiler_params=_cparams(("parallel", "parallel", "arbitrary")),
        name="banded_attention",
    )(sinks, view(q), view(k), view(k), view(v), view(v), bias)
    return o.reshape(B, S, qc), lse


def band_bias(table, dilation, max_dist):
    n = 4 * BLK
    j = np.arange(n)
    dist = BLK - np.where(j < 2 * BLK, j, j - n)
    allowed = (dist >= 0) & (dist <= max_dist)
    v = jnp.where(allowed[:, None], table[_bucket_np(dist * dilation)], NEG)
    v = jnp.transpose(v, (1, 0)).astype(F32)
    tile = jnp.tile(v, (1, BLK))[:, :BLK * (n - 1)].reshape(-1, BLK, n - 1)
    return tile[:, :, :2 * BLK]


def _combine_kernel(*refs, n_pat, heads):
    o_refs, l_refs, out_ref = refs[:n_pat], refs[n_pat:2 * n_pat], refs[2 * n_pat]
    lses = [r[0] for r in l_refs]
    m = functools.reduce(jnp.maximum, lses)
    es = [jnp.exp(l - m) for l in lses]
    tot = functools.reduce(lambda a, b: a + b, es)
    ws = [e / tot for e in es]
    for h in range(heads):
        sl = slice(h * HEAD_DIM, (h + 1) * HEAD_DIM)
        acc = ws[0][:, h:h + 1] * o_refs[0][0, :, sl]
        for p in range(1, n_pat):
            acc = acc + ws[p][:, h:h + 1] * o_refs[p][0, :, sl]
        out_ref[0, :, sl] = acc.astype(out_ref.dtype)


def combine_patterns(outs, lses, tm=512):
    B, S, C = outs[0].shape
    heads = C // HEAD_DIM
    n_pat = len(outs)
    idx = lambda b, i: (b, i, 0)
    return pl.pallas_call(
        functools.partial(_combine_kernel, n_pat=n_pat, heads=heads),
        grid=(B, S // tm),
        in_specs=[pl.BlockSpec((1, tm, C), idx)] * n_pat + [pl.BlockSpec((1, tm, heads), idx)] * n_pat,
        out_specs=pl.BlockSpec((1, tm, C), idx),
        out_shape=jax.ShapeDtypeStruct((B, S, C), BF16),
        compiler_params=_cparams(("parallel", "parallel")),
        name="combine_patterns",
    )(*outs, *lses)


def _outproj_kernel(*refs, n_parts):
    o_refs, w_refs = refs[:n_parts], refs[n_parts:2 * n_parts]
    x_ref, gate_ref, g_ref, b_ref, out_ref = refs[2 * n_parts:]
    mix = jnp.dot(o_refs[0][0], w_refs[0][...], preferred_element_type=F32)
    for o_ref, w_ref in zip(o_refs[1:], w_refs[1:]):
        mix = mix + jnp.dot(o_ref[0], w_ref[...], preferred_element_type=F32)
    z = ALPHA * x_ref[0] + gate_ref[0] * mix
    out_ref[0] = _layer_norm(z, g_ref[...], b_ref[...])


def out_projection_ln(parts, weights, x, gate, ln_g, ln_b, tm=512):
    B, S, D = x.shape
    idx = lambda b, i: (b, i, 0)
    const2 = lambda b, i: (0, 0)
    return pl.pallas_call(
        functools.partial(_outproj_kernel, n_parts=len(parts)),
        grid=(B, S // tm),
        in_specs=([pl.BlockSpec((1, tm, p.shape[2]), idx) for p in parts]
                  + [pl.BlockSpec(w.shape, const2) for w in weights]
                  + [pl.BlockSpec((1, tm, D), idx),
                     pl.BlockSpec((1, 1, D), lambda b, i: (b, 0, 0)),
                     pl.BlockSpec((1, D), const2),
                     pl.BlockSpec((1, D), const2)]),
        out_specs=pl.BlockSpec((1, tm, D), idx),
        out_shape=jax.ShapeDtypeStruct((B, S, D), F32),
        compiler_params=_cparams(("parallel", "parallel")),
        name="out_projection_ln",
    )(*parts, *weights, x, gate, ln_g.reshape(1, D), ln_b.reshape(1, D))


def _dsa_kernel(q_ref, qi_ref, wi_ref, kid_ref, kd_ref, va_ref, vb_ref, strip_ref, o_ref,
                skt_scr, hi_scr, lo_scr, qm_scr, qim_scr, mb_scr, s_scr, m_scr, acc_scr, *, k_sel):
    i = pl.program_id(1)
    ck = DSA_CK
    sub = ck // LANES
    gt = DSA_GK // LANES
    gc = DSA_GK // ck
    nbuf = DSA_SKEW + 1
    ngr = (i * BLK + BLK + DSA_GK - 1) // DSA_GK
    nck = ngr * gc

    half = C_HEADS // 2
    hps = DSA_HPS
    upper_r = lax.broadcasted_iota(I32, (LANES, BLK), 0) >= HEAD_DIM

    def slot_cols(slot):
        return slot // hps, slice((slot % hps) * BLK, (slot % hps + 1) * BLK)

    def slot_head(slot):
        return 2 * (slot % half) + slot // half

    for p in range(half):
        qt = q_ref[0, :, p * LANES:(p + 1) * LANES].astype(F32).T
        k0, c0 = slot_cols(p)
        k1, c1 = slot_cols(half + p)
        qm_scr[k0, :, c0] = jnp.where(upper_r, 0.0, qt).astype(BF16)
        qm_scr[k1, :, c1] = jnp.where(upper_r, qt, 0.0).astype(BF16)
    for p in range(IDX_HEADS // 2):
        qt = qi_ref[0, :, p * LANES:(p + 1) * LANES].astype(F32).T
        qim_scr[:, (2 * p) * BLK:(2 * p + 1) * BLK] = jnp.where(upper_r, 0.0, qt).astype(BF16)
        qim_scr[:, (2 * p + 1) * BLK:(2 * p + 2) * BLK] = jnp.where(upper_r, qt, 0.0).astype(BF16)
    wscale = IDX_HEADS ** -0.5 * IDX_DIM ** -0.5
    wt = wi_ref[0, 0] * wscale

    key_pos = lax.broadcasted_iota(I32, (ck, BLK), 0)
    qry_pos = lax.broadcasted_iota(I32, (ck, BLK), 1) + i * BLK

    def score_chunk(j, carry):
        kblk = kid_ref[0, j]
        sc = jnp.zeros((ck, BLK), F32)
        for p in range(IDX_HEADS // 2):
            d = jnp.dot(kblk, qim_scr[:, 2 * p * BLK:(2 * p + 2) * BLK], preferred_element_type=F32)
            for r in range(2):
                h = 2 * p + r
                sc = sc + jnp.maximum(d[:, r * BLK:(r + 1) * BLK], 0.0) * wt[h:h + 1]
        bits = pltpu.bitcast(sc, I32)
        key = bits ^ ((bits >> 31) & 0x7FFFFFFF)
        key = jnp.where(key_pos + j * ck <= qry_pos, key, INT_MIN)
        for u in range(sub):
            tile_t = key[u * LANES:(u + 1) * LANES]
            skt_scr[j * sub + u] = tile_t
            hi_scr[j * sub + u] = (tile_t >> 16).astype(I16)
            lo_scr[j * sub + u] = ((tile_t & 0xFFFF) - HALF).astype(I16)
        return carry

    lax.fori_loop(0, nck, score_chunk, 0)

    packed = 2 * SUBLANES

    def count(scr, cond):
        def group(g, acc):
            parts = []
            for u in range(gt):
                hit = jnp.where(cond(scr[g * gt + u]), jnp.int16(1), jnp.int16(0))
                parts += [hit[r:r + packed] for r in range(0, LANES, packed)]
            while len(parts) > 1:
                parts = [a + b for a, b in zip(parts[0::2], parts[1::2])]
            return acc + parts[0]
        acc = lax.fori_loop(0, ngr, group, jnp.zeros((packed, LANES), I16))
        return jnp.sum(acc.astype(I32), axis=0, keepdims=True)

    def select16(scr, target):
        def bit_pass(t, v):
            c = v | lax.shift_left(jnp.int32(1), 15 - t)
            c16 = (c - HALF).astype(I16)
            return jnp.where(count(scr, lambda x: x >= c16) >= target, c, v)
        return lax.fori_loop(0, 16, bit_pass, jnp.zeros((1, LANES), I32))

    v_hi = select16(hi_scr, k_sel)
    hi16 = (v_hi - HALF).astype(I16)
    above = count(hi_scr, lambda x: x > hi16)
    def keep_low(g, carry):
        for u in range(gt):
            idx = g * gt + u
            lo_scr[idx] = jnp.where(hi_scr[idx] == hi16, lo_scr[idx], jnp.int16(-HALF))
        return carry
    lax.fori_loop(0, ngr, keep_low, 0)
    v_lo = select16(lo_scr, k_sel - above)
    thr_q = jnp.maximum(lax.shift_left(v_hi - HALF, 16) | v_lo, INT_MIN + 1)

    m_scr[...] = jnp.full(m_scr.shape, NEG, F32)
    acc_scr[...] = jnp.zeros(acc_scr.shape, F32)

    def attend_group(g, carry):
        for u in range(gt):
            mb = jnp.where(skt_scr[g * gt + u] >= thr_q, 0.0, NEG).astype(BF16)
            for r in range(hps):
                mb_scr[u, :, r * BLK:(r + 1) * BLK] = mb
        tiles = [jnp.clip(i - (g * gt + u), 0, N_NEAR) for u in range(gt)]

        def logits(k):
            heads = [slot_head(k * hps + r) for r in range(hps)]
            s = jnp.dot(kd_ref[0, g], qm_scr[k], preferred_element_type=F32).astype(BF16)
            mx = None
            for t in range(gt):
                bias = jnp.concatenate([strip_ref[h, tiles[t]] for h in heads], axis=1)
                piece = s[t * BLK:(t + 1) * BLK] + mb_scr[t] + bias
                s_scr[k % nbuf, t] = piece
                mx = piece if mx is None else jnp.maximum(mx, piece)
            m_old = m_scr[k]
            m_new = jnp.maximum(m_old, jnp.max(mx.astype(F32), axis=0, keepdims=True))
            m_scr[k] = m_new
            return m_old, m_new

        def accumulate(k, m_old, m_new):
            m16 = m_new.astype(BF16)
            p = jnp.concatenate([jnp.exp(s_scr[k % nbuf, t] - m16) for t in range(gt)], axis=0)
            v_ref = va_ref if k * hps < half else vb_ref
            pv = jnp.dot(v_ref[0, g], p, preferred_element_type=F32)
            acc_scr[k] = jnp.exp(m_old - m_new) * acc_scr[k] + pv

        nstk = C_HEADS // hps
        stats = [logits(k) for k in range(DSA_SKEW)]
        for k in range(nstk):
            if k + DSA_SKEW < nstk:
                stats.append(logits(k + DSA_SKEW))
            accumulate(k, *stats[k])
        return carry

    lax.fori_loop(0, ngr, attend_group, 0)

    for p in range(half):
        k0, c0 = slot_cols(p)
        k1, c1 = slot_cols(half + p)
        a0 = acc_scr[k0, :, c0]
        a1 = acc_scr[k1, :, c1]
        ot = jnp.where(upper_r, a1 / pltpu.roll(a1, HEAD_DIM, 0), a0 / pltpu.roll(a0, HEAD_DIM, 0))
        o_ref[0, :, p * LANES:(p + 1) * LANES] = ot.T.astype(o_ref.dtype)


def dsa_attention(q, qi, wi, ki, kc, vc, table):
    B, S, _ = q.shape
    ck = DSA_CK
    assert S % DSA_GK == 0
    nchunks = S // ck
    k_sel = min(TOPK_MAX, S // 4)

    ngroups = S // DSA_GK
    ones = jnp.ones_like(vc)
    kid = jnp.concatenate([ki, ki], axis=-1).reshape(B, nchunks, ck, LANES)
    kd = jnp.concatenate([kc, kc], axis=-1).reshape(B, ngroups, DSA_GK, LANES)
    wit = jnp.transpose(wi.reshape(B, S // BLK, BLK, IDX_HEADS), (0, 1, 3, 2))

    def grouped_t(t):
        return jnp.transpose(t.reshape(B, ngroups, DSA_GK, LANES), (0, 1, 3, 2))

    va = grouped_t(jnp.concatenate([vc, ones], axis=-1))
    vb = grouped_t(jnp.concatenate([ones, vc], axis=-1))
    j = np.arange(2 * BLK)
    offs = np.where(j <= BLK, -j, 2 * BLK - j)
    v = table[_bucket_np(np.arange(N_NEAR)[:, None] * BLK + offs[None, :])]
    v = jnp.transpose(v, (2, 0, 1)).astype(F32)
    near = jnp.tile(v, (1, 1, BLK))[..., :BLK * (2 * BLK - 1)]
    near = near.reshape(C_HEADS, N_NEAR, BLK, 2 * BLK - 1)[..., :BLK]
    near = jnp.swapaxes(near, 2, 3)
    far = jnp.broadcast_to(table[REL_BUCKETS - 1].astype(F32)[:, None, None, None],
                           (C_HEADS, 1, BLK, LANES))
    strip = jnp.concatenate([near, far], axis=1).astype(BF16)

    once = pl.Buffered(1)
    qblk = lambda b, i: (b, i, 0)
    per_b = lambda b, i: (b, 0, 0, 0)
    return pl.pallas_call(
        functools.partial(_dsa_kernel, k_sel=k_sel),
        grid=(B, S // BLK),
        in_specs=[pl.BlockSpec((1, BLK, C_HEADS * HEAD_DIM), qblk),
                  pl.BlockSpec((1, BLK, IDX_HEADS * IDX_DIM), qblk),
                  pl.BlockSpec((1, 1, IDX_HEADS, BLK), lambda b, i: (b, i, 0, 0)),
                  pl.BlockSpec((1, nchunks, ck, LANES), per_b, pipeline_mode=once),
                  pl.BlockSpec((1, ngroups, DSA_GK, LANES), per_b, pipeline_mode=once),
                  pl.BlockSpec((1, ngroups, LANES, DSA_GK), per_b, pipeline_mode=once),
                  pl.BlockSpec((1, ngroups, LANES, DSA_GK), per_b, pipeline_mode=once),
                  pl.BlockSpec(strip.shape, lambda b, i: (0, 0, 0, 0), pipeline_mode=once)],
        out_specs=pl.BlockSpec((1, BLK, C_HEADS * HEAD_DIM), qblk),
        scratch_shapes=[pltpu.VMEM((S // LANES, LANES, BLK), I32),
                        pltpu.VMEM((S // LANES, LANES, BLK), I16),
                        pltpu.VMEM((S // LANES, LANES, BLK), I16),
                        pltpu.VMEM((C_HEADS // DSA_HPS, LANES, DSA_HPS * BLK), BF16),
                        pltpu.VMEM((LANES, IDX_HEADS * BLK), BF16),
                        pltpu.VMEM((DSA_GK // LANES, LANES, DSA_HPS * BLK), BF16),
                        pltpu.VMEM((DSA_SKEW + 1, DSA_GK // LANES, LANES, DSA_HPS * BLK), BF16),
                        pltpu.VMEM((C_HEADS // DSA_HPS, 1, DSA_HPS * BLK), F32),
                        pltpu.VMEM((C_HEADS // DSA_HPS, LANES, DSA_HPS * BLK), F32)],
        out_shape=jax.ShapeDtypeStruct((B, S, C_HEADS * HEAD_DIM), BF16),
        compiler_params=_cparams(("parallel", "arbitrary")),
        name="dsa_attention",
    )(q, qi, wit, kid, kd, va, vb, strip)


def _ffn_step(x2d, w1_ref, w3_ref, w2_ref, acc):
    xb = x2d[...]
    h1 = jnp.dot(xb, w1_ref[0], preferred_element_type=F32)
    h3 = jnp.dot(xb, w3_ref[0], preferred_element_type=F32)
    a = (h1 * jax.nn.sigmoid(h1) * h3).astype(BF16)
    acc[...] += jnp.dot(a, w2_ref[0], preferred_element_type=F32)


def _ffn_dense_kernel(x_ref, sc_ref, sh_ref, gate_ref, g_ref, b_ref, w1_ref, w3_ref, w2_ref, out_ref,
                      x2d, acc, *, nf):
    f = pl.program_id(2)

    @pl.when(f == 0)
    def _():
        x2d[...] = (x_ref[0] * sc_ref[0] + sh_ref[0]).astype(BF16)
        acc[...] = jnp.zeros(acc.shape, F32)

    _ffn_step(x2d, w1_ref, w3_ref, w2_ref, acc)

    @pl.when(f == nf - 1)
    def _():
        z = ALPHA * x_ref[0] + gate_ref[0] * acc[...]
        out_ref[0] = _layer_norm(z, g_ref[...], b_ref[...])


def ffn_dense_ln(x, scale, shift, gate, ln_g, ln_b, w1, w3, w2, tm=512, tf=FFN_TF):
    B, S, D = x.shape
    nf = D_FF // tf
    xi = lambda b, i, f: (b, i, 0)
    bi = lambda b, i, f: (b, 0, 0)
    c2 = lambda b, i, f: (0, 0)
    return pl.pallas_call(
        functools.partial(_ffn_dense_kernel, nf=nf),
        grid=(B, S // tm, nf),
        in_specs=[pl.BlockSpec((1, tm, D), xi),
                  pl.BlockSpec((1, 1, D), bi), pl.BlockSpec((1, 1, D), bi), pl.BlockSpec((1, 1, D), bi),
                  pl.BlockSpec((1, D), c2), pl.BlockSpec((1, D), c2),
                  pl.BlockSpec((1, D, tf), lambda b, i, f: (0, 0, f)),
                  pl.BlockSpec((1, D, tf), lambda b, i, f: (0, 0, f)),
                  pl.BlockSpec((1, tf, D), lambda b, i, f: (0, f, 0))],
        out_specs=pl.BlockSpec((1, tm, D), xi),
        out_shape=jax.ShapeDtypeStruct((B, S, D), F32),
        scratch_shapes=[pltpu.VMEM((tm, D), BF16), pltpu.VMEM((tm, D), F32)],
        compiler_params=_cparams(("parallel", "parallel", "arbitrary")),
        name="ffn_dense_ln",
    )(x, scale, shift, gate, ln_g.reshape(1, D), ln_b.reshape(1, D), w1, w3, w2)


def _ffn_expert_kernel(be_ref, tok_ref, tok_next_ref, dst_ref, h_hbm, w1_ref, w3_ref, w2_ref, y_hbm,
                       xbuf, ybuf, x2d, acc, gsem, ssem, *, nf, tm):
    i = pl.program_id(0)
    f = pl.program_id(1)
    n_blocks = pl.num_programs(0)
    slot = i % 2
    rows = tm * D_TILES

    def start_gather(idx_ref, s):
        def body(r, carry):
            src = pl.multiple_of(idx_ref[0, 0, r] * D_TILES, D_TILES)
            dst = pl.multiple_of(r * D_TILES, D_TILES)
            pltpu.make_async_copy(h_hbm.at[pl.ds(src, D_TILES)], xbuf.at[s, pl.ds(dst, D_TILES)],
                                  gsem.at[s]).start()
            return carry
        lax.fori_loop(0, tm, body, 0, unroll=8)

    def wait_gather(s):
        pltpu.make_async_copy(h_hbm.at[pl.ds(0, rows)], xbuf.at[s], gsem.at[s]).wait()

    def start_scatter():
        def body(r, carry):
            src = pl.multiple_of(r * D_TILES, D_TILES)
            dst = pl.multiple_of(dst_ref[0, 0, r] * D_TILES, D_TILES)
            pltpu.make_async_copy(ybuf.at[pl.ds(src, D_TILES)], y_hbm.at[pl.ds(dst, D_TILES)], ssem).start()
            return carry
        lax.fori_loop(0, tm, body, 0, unroll=8)

    def wait_scatter():
        pltpu.make_async_copy(ybuf, y_hbm.at[pl.ds(0, rows)], ssem).wait()

    @pl.when(f == 0)
    def _():
        @pl.when(i == 0)
        def _():
            start_gather(tok_ref, 0)

        @pl.when(i + 1 < n_blocks)
        def _():
            start_gather(tok_next_ref, 1 - slot)

        wait_gather(slot)
        for j in range(D_TILES):
            x2d[:, j * LANES:(j + 1) * LANES] = xbuf[slot, pl.ds(j, tm, stride=D_TILES), :].astype(BF16)
        acc[...] = jnp.zeros(acc.shape, F32)

    _ffn_step(x2d, w1_ref, w3_ref, w2_ref, acc)

    @pl.when(f == nf - 1)
    def _():
        @pl.when(i > 0)
        def _():
            wait_scatter()

        for j in range(D_TILES):
            ybuf[pl.ds(j, tm, stride=D_TILES), :] = acc[:, j * LANES:(j + 1) * LANES]
        start_scatter()

        @pl.when(i == n_blocks - 1)
        def _():
            wait_scatter()


def ffn_experts(h_tiles, slot_tok, slot_dst, block_expert, w1, w3, w2, tf=FFN_TF):
    tm = MOE_BLOCK
    P = slot_tok.shape[0]
    n_blocks = P // tm
    nf = D_FF // tf
    tok3 = slot_tok.reshape(n_blocks, 1, tm)
    dst3 = slot_dst.reshape(n_blocks, 1, tm)
    cur = lambda i, f, be: (i, 0, 0)
    nxt = lambda i, f, be: (jnp.minimum(i + 1, n_blocks - 1), 0, 0)
    return pl.pallas_call(
        functools.partial(_ffn_expert_kernel, nf=nf, tm=tm),
        grid_spec=pltpu.PrefetchScalarGridSpec(
            num_scalar_prefetch=1,
            grid=(n_blocks, nf),
            in_specs=[pl.BlockSpec((1, 1, tm), cur, memory_space=pltpu.SMEM),
                      pl.BlockSpec((1, 1, tm), nxt, memory_space=pltpu.SMEM),
                      pl.BlockSpec((1, 1, tm), cur, memory_space=pltpu.SMEM),
                      pl.BlockSpec(memory_space=pl.ANY),
                      pl.BlockSpec((1, D_MODEL, tf), lambda i, f, be: (be[i], 0, f)),
                      pl.BlockSpec((1, D_MODEL, tf), lambda i, f, be: (be[i], 0, f)),
                      pl.BlockSpec((1, tf, D_MODEL), lambda i, f, be: (be[i], f, 0))],
            out_specs=pl.BlockSpec(memory_space=pl.ANY),
            scratch_shapes=[pltpu.VMEM((2, tm * D_TILES, LANES), F32),
                            pltpu.VMEM((tm * D_TILES, LANES), F32),
                            pltpu.VMEM((tm, D_MODEL), BF16),
                            pltpu.VMEM((tm, D_MODEL), F32),
                            pltpu.SemaphoreType.DMA((2,)),
                            pltpu.SemaphoreType.DMA(())]),
        out_shape=jax.ShapeDtypeStruct((P * D_TILES, LANES), F32),
        compiler_params=_cparams(("arbitrary", "arbitrary")),
        name="ffn_experts",
    )(block_expert, tok3, tok3, dst3, h_tiles, w1, w3, w2)


def _router_kernel(x_ref, sc_ref, sh_ref, wr_ref, h_ref, idx_ref, gate_ref, *, tm):
    h = x_ref[0] * sc_ref[0] + sh_ref[0]
    for j in range(D_TILES):
        h_ref[pl.ds(j, tm, stride=D_TILES), :] = h[:, j * LANES:(j + 1) * LANES]
    logits = jnp.dot(h, wr_ref[...], preferred_element_type=F32, precision=lax.Precision.HIGHEST)
    lane = lax.broadcasted_iota(I32, logits.shape, 1)
    lg = jnp.where(lane < N_EXPERTS, logits, -jnp.inf)
    m1 = jnp.max(lg, axis=1, keepdims=True)
    i1 = jnp.min(jnp.where(lg == m1, lane, LANES), axis=1, keepdims=True)
    lg2 = jnp.where(lane == i1, -jnp.inf, lg)
    m2 = jnp.max(lg2, axis=1, keepdims=True)
    i2 = jnp.min(jnp.where(lg2 == m2, lane, LANES), axis=1, keepdims=True)
    e = jnp.exp(m2 - m1)
    idx_ref[:, 0:1] = i1
    idx_ref[:, 1:2] = i2
    gate_ref[:, 0:1] = 1.0 / (1.0 + e)
    gate_ref[:, 1:2] = e / (1.0 + e)


def route_tokens(x, scale, shift, w_router, tm=512):
    B, S, D = x.shape
    T = B * S
    nb = S // tm
    wr = jnp.pad(w_router, ((0, 0), (0, LANES - N_EXPERTS)))
    return pl.pallas_call(
        functools.partial(_router_kernel, tm=tm),
        grid=(B, nb),
        in_specs=[pl.BlockSpec((1, tm, D), lambda b, i: (b, i, 0)),
                  pl.BlockSpec((1, 1, D), lambda b, i: (b, 0, 0)),
                  pl.BlockSpec((1, 1, D), lambda b, i: (b, 0, 0)),
                  pl.BlockSpec((D, LANES), lambda b, i: (0, 0))],
        out_specs=[pl.BlockSpec((tm * D_TILES, LANES), lambda b, i: (b * nb + i, 0)),
                   pl.BlockSpec((tm, TOP_K), lambda b, i: (b * nb + i, 0)),
                   pl.BlockSpec((tm, TOP_K), lambda b, i: (b * nb + i, 0))],
        out_shape=[jax.ShapeDtypeStruct((T * D_TILES, LANES), F32),
                   jax.ShapeDtypeStruct((T, TOP_K), I32),
                   jax.ShapeDtypeStruct((T, TOP_K), F32)],
        compiler_params=_cparams(("parallel", "parallel")),
        name="route_tokens",
    )(x, scale, shift, wr)


def _moe_combine_kernel(y_ref, gt_ref, x_ref, gate_ref, g_ref, b_ref, out_ref, *, tm):
    g0 = gt_ref[:, 0:1]
    g1 = gt_ref[:, 1:2]
    stride = TOP_K * D_TILES
    pieces = []
    for j in range(D_TILES):
        sl = slice(j * LANES, (j + 1) * LANES)
        ff = g0 * y_ref[pl.ds(j, tm, stride=stride), :] + g1 * y_ref[pl.ds(D_TILES + j, tm, stride=stride), :]
        pieces.append(ALPHA * x_ref[0, :, sl] + gate_ref[0, :, sl] * ff)
    z = jnp.concatenate(pieces, axis=1)
    out_ref[0] = _layer_norm(z, g_ref[...], b_ref[...])


def moe_combine_ln(yg, gates, x, gate, ln_g, ln_b, tm=512):
    B, S, D = x.shape
    nb = S // tm
    return pl.pallas_call(
        functools.partial(_moe_combine_kernel, tm=tm),
        grid=(B, nb),
        in_specs=[pl.BlockSpec((tm * TOP_K * D_TILES, LANES), lambda b, i: (b * nb + i, 0)),
                  pl.BlockSpec((tm, TOP_K), lambda b, i: (b * nb + i, 0)),
                  pl.BlockSpec((1, tm, D), lambda b, i: (b, i, 0)),
                  pl.BlockSpec((1, 1, D), lambda b, i: (b, 0, 0)),
                  pl.BlockSpec((1, D), lambda b, i: (0, 0)),
                  pl.BlockSpec((1, D), lambda b, i: (0, 0))],
        out_specs=pl.BlockSpec((1, tm, D), lambda b, i: (b, i, 0)),
        out_shape=jax.ShapeDtypeStruct((B, S, D), F32),
        compiler_params=_cparams(("parallel", "parallel")),
        name="moe_combine_ln",
    )(yg, gates, x, gate, ln_g.reshape(1, D), ln_b.reshape(1, D))


def _scale_cols(w, start, width, factor):
    return w.at[:, start:start + width].multiply(factor)


def even_mixer_layer(x, mod, w_in, w_out, rel_table, sinks, ln_g, ln_b):
    B, S, D = x.shape
    assert S % A_PAD == 0
    shift, scale, gate = mod
    ah, bh, bk = A_HEADS * HEAD_DIM, B_HEADS * HEAD_DIM, B_KV_HEADS * HEAD_DIM
    w = _scale_cols(w_in, 0, ah, HEAD_DIM ** -0.5)
    w = _scale_cols(w, 3 * ah, bh, HEAD_DIM ** -0.5).astype(BF16)
    groups = [(ah, BF16)] * 3 + [(bh, BF16), (bk, BF16), (bk, BF16)]
    qa, ka, va, qb, kb, vb = in_projection(x, scale, shift, w, groups)

    outs, lses = [], []
    for window, d in A_PATTERNS:
        bias = band_bias(rel_table[:, :A_HEADS], d, window // d)
        o, lse = banded_attention(qa, ka, va, bias, None, F32, dil=d)
        outs.append(o)
        lses.append(jnp.transpose(lse, (0, 2, 1, 3)).reshape(B, S, A_HEADS))
    oa = combine_patterns(outs, lses)

    bias_b = band_bias(rel_table[:, A_HEADS:A_HEADS + B_HEADS], 1, B_WINDOW - 1)
    ob, _ = banded_attention(qb, kb, vb, bias_b, sinks.astype(F32), BF16)

    wo = w_out.astype(BF16)
    return out_projection_ln([oa, ob], [wo[:ah], wo[ah:]], x, gate, ln_g, ln_b)


def dsa_mixer_layer(x, mod, w_in, w_out, rel_table, ln_g, ln_b):
    shift, scale, gate = mod
    qw = C_HEADS * HEAD_DIM
    iw = IDX_HEADS * IDX_DIM
    q_w, kc_w, vc_w, qi_w, ki_w, wi_w = jnp.split(
        w_in, [qw, qw + HEAD_DIM, qw + 2 * HEAD_DIM, qw + 2 * HEAD_DIM + iw, qw + 2 * HEAD_DIM + iw + IDX_DIM],
        axis=1)
    pad = jnp.zeros((D_MODEL, LANES - IDX_DIM - IDX_HEADS), w_in.dtype)
    w = jnp.concatenate([q_w * HEAD_DIM ** -0.5, kc_w, vc_w, qi_w, ki_w, wi_w, pad], axis=1).astype(BF16)
    groups = [(qw, BF16), (2 * HEAD_DIM, BF16), (iw, BF16), (LANES, F32)]
    q, kv, qi, kw = in_projection(x, scale, shift, w, groups)
    kc, vc = kv[..., :HEAD_DIM], kv[..., HEAD_DIM:]
    ki = kw[..., :IDX_DIM].astype(BF16)
    wi = kw[..., IDX_DIM:IDX_DIM + IDX_HEADS]
    o = dsa_attention(q, qi, wi, ki, kc, vc, rel_table[:, :C_HEADS])
    return out_projection_ln([o], [w_out.astype(BF16)], x, gate, ln_g, ln_b)


def moe_layer(x, mod, w_router, w1, w3, w2, ln_g, ln_b):
    B, S, D = x.shape
    shift, scale, gate = mod
    T = B * S
    A = T * TOP_K
    h_tiles, top_idx, gates = route_tokens(x, scale, shift, w_router)

    e_flat = top_idx.reshape(-1)
    onehot = (e_flat[:, None] == jnp.arange(N_EXPERTS)[None, :]).astype(I32)
    counts = jnp.sum(onehot, axis=0)
    padded = (counts + MOE_BLOCK - 1) // MOE_BLOCK * MOE_BLOCK
    pends = jnp.cumsum(padded)
    dest = jnp.sum(onehot * (jnp.cumsum(onehot, axis=0) - onehot + (pends - padded)[None, :]), axis=1)
    n_blocks = -(-A // MOE_BLOCK) + N_EXPERTS
    P = n_blocks * MOE_BLOCK
    slot_src = jnp.full((P,), -1, I32).at[dest].set(jnp.arange(A, dtype=I32))
    is_pad = slot_src < 0
    slot_tok = jnp.where(is_pad, 0, slot_src // TOP_K)
    slot_dst = jnp.where(is_pad, A - 1 + jnp.cumsum(is_pad.astype(I32)), slot_src)
    block_expert = jnp.minimum(
        jnp.searchsorted(pends, jnp.arange(n_blocks) * MOE_BLOCK, side='right'), N_EXPERTS - 1).astype(I32)

    yg = ffn_experts(h_tiles, slot_tok, slot_dst, block_expert,
                     w1.astype(BF16), w3.astype(BF16), w2.astype(BF16))
    return moe_combine_ln(yg, gates, x, gate, ln_g, ln_b)


def kernel(x, c, rel_table, w_in_even, w_out_even, sinks, w_in_odd, w_out_odd, ffn_w1, ffn_w3, ffn_w2,
           router, exp_w1, exp_w3, exp_w2, ada_w, ada_b, ln_g, ln_b):
    D = D_MODEL
    mods = ada_modulation_all(c, ada_w, ada_b)

    def mod(layer, sub):
        m = mods[2 * layer + sub]
        return m[:, None, :D], m[:, None, D:2 * D], m[:, None, 2 * D:]

    for layer in range(DEPTH):
        i = layer // 2
        if layer % 2 == 0:
            x = even_mixer_layer(x, mod(layer, 0), w_in_even[i], w_out_even[i], rel_table, sinks[i],
                                 ln_g[layer, 0], ln_b[layer, 0])
            shift, scale, gate = mod(layer, 1)
            x = ffn_dense_ln(x, scale, shift, gate, ln_g[layer, 1], ln_b[layer, 1],
                             ffn_w1[i][None].astype(BF16), ffn_w3[i][None].astype(BF16),
                             ffn_w2[i][None].astype(BF16))
        else:
            x = dsa_mixer_layer(x, mod(layer, 0), w_in_odd[i], w_out_odd[i], rel_table,
                                ln_g[layer, 0], ln_b[layer, 0])
            x = moe_layer(x, mod(layer, 1), router[i], exp_w1[i], exp_w3[i], exp_w2[i],
                          ln_g[layer, 1], ln_b[layer, 1])
    return x
```
